```python
import jax, jax.numpy as jnp
from jax import lax
import numpy as np

D_MODEL = 1024
BATCH = 2
SEQ = 8192
DEPTH = 1

FOURIER_GROUPS = 4
FOURIER_GROUP_DIM = 128
D_FOURIER = FOURIER_GROUPS * FOURIER_GROUP_DIM
D_CONV = D_MODEL
CONV_WIDTH = 3
N_BRANCHES = 2
D_IN_PROJ = D_FOURIER + 3 * D_CONV + N_BRANCHES * D_MODEL
N_EXPERTS = 16
EC_CAPACITY = 2
D_EXPERT = 2048
N_MOD = 6
RMS_EPS = 1e-6

kernel_name = "hybrid_fourier_shortconv_ecmoe_block"


def rms_norm(x, g):
    xf = x.astype(jnp.float32)
    y = xf * lax.rsqrt(jnp.mean(xf * xf, axis=-1, keepdims=True) + RMS_EPS)
    return (y * g.astype(jnp.float32)).astype(x.dtype)


def modulate(h, shift, scale):
    return h * (1 + scale[:, None, :]) + shift[:, None, :]


def fourier_mix(u):
    b, s, _ = u.shape
    ug = u.reshape(b, s, FOURIER_GROUPS, FOURIER_GROUP_DIM).astype(jnp.float32)
    f = jnp.fft.fftn(ug, axes=(1, 3), norm="ortho")
    return jnp.real(f).reshape(b, s, D_FOURIER).astype(u.dtype)


def centred_conv3(u, w):
    s = u.shape[1]
    up = jnp.pad(u, ((0, 0), (1, 1), (0, 0)))
    return up[:, 0:s] * w[0] + up[:, 1:s + 1] * w[1] + up[:, 2:s + 2] * w[2]


def expert_choice_moe(h, w_router, b_router, w_gate_e, w_up_e, w_down_e):
    b, s, d = h.shape
    cap = EC_CAPACITY * s // N_EXPERTS
    logits = jnp.einsum('bsd,de->bse', h, w_router).astype(jnp.float32) + b_router.astype(jnp.float32)
    probs = jax.nn.softmax(logits, axis=-1)
    vals, idx = lax.top_k(jnp.swapaxes(probs, 1, 2), cap)
    xe = jax.vmap(lambda hb, ib: hb[ib])(h, idx)
    a = jnp.einsum('becd,edf->becf', xe, w_gate_e)
    u = jnp.einsum('becd,edf->becf', xe, w_up_e)
    y = jnp.einsum('becf,efd->becd', jax.nn.silu(a) * u, w_down_e)
    y = y * vals[..., None].astype(y.dtype)
    return jax.vmap(lambda yb, ib: jnp.zeros((s, d), yb.dtype).at[ib.reshape(-1)].add(yb.reshape(-1, d)))(y, idx)


def setup_inputs(seed: int = 0) -> dict:
    key = jax.random.key(seed)
    ks = jax.random.split(key, 20)
    f32 = jnp.float32
    nrm = lambda k, shape, scale: (jax.random.normal(k, shape, f32) * scale).astype(f32)
    D = D_MODEL
    return {
        "x": nrm(ks[0], (BATCH, SEQ, D), 1.0),
        "c": nrm(ks[1], (BATCH, D), 1.0),
        "w_ada": nrm(ks[2], (DEPTH, D, N_MOD * D), 0.5 * D ** -0.5),
        "b_ada": nrm(ks[3], (DEPTH, N_MOD * D), 0.02),
        "g_norm_mix": 1.0 + nrm(ks[4], (DEPTH, D), 0.02),
        "w_in": nrm(ks[5], (DEPTH, D, D_IN_PROJ), D ** -0.5),
        "b_gate": nrm(ks[6], (DEPTH, N_BRANCHES * D), 0.02),
        "w_fourier": nrm(ks[7], (DEPTH, D_FOURIER, D), D_FOURIER ** -0.5),
        "w_conv": nrm(ks[8], (DEPTH, CONV_WIDTH, D_CONV), CONV_WIDTH ** -0.5),
        "w_conv_out": nrm(ks[9], (DEPTH, D_CONV, D), D_CONV ** -0.5),
        "w_o": nrm(ks[10], (DEPTH, D, D), D ** -0.5),
        "g_norm_moe": 1.0 + nrm(ks[11], (DEPTH, D), 0.02),
        "w_router": nrm(ks[12], (DEPTH, D, N_EXPERTS), D ** -0.5),
        "b_router": nrm(ks[13], (DEPTH, N_EXPERTS), 0.01),
        "w_gate_e": nrm(ks[14], (DEPTH, N_EXPERTS, D, D_EXPERT), D ** -0.5),
        "w_up_e": nrm(ks[15], (DEPTH, N_EXPERTS, D, D_EXPERT), D ** -0.5),
        "w_down_e": nrm(ks[16], (DEPTH, N_EXPERTS, D_EXPERT, D), D_EXPERT ** -0.5),
        "g_final": 1.0 + nrm(ks[17], (D,), 0.02),
    }


def reference(x, c, w_ada, b_ada, g_norm_mix, w_in, b_gate, w_fourier, w_conv, w_conv_out, w_o,
              g_norm_moe, w_router, b_router, w_gate_e, w_up_e, w_down_e, g_final):
    D = D_MODEL
    c_act = jax.nn.silu(c)
    for l in range(DEPTH):
        mod = jnp.einsum('bd,dm->bm', c_act, w_ada[l]) + b_ada[l]
        shift_m, scale_m, gate_m, shift_f, scale_f, gate_f = jnp.split(mod, N_MOD, axis=-1)

        h = modulate(rms_norm(x, g_norm_mix[l]), shift_m, scale_m)
        p = jnp.einsum('bsd,dk->bsk', h, w_in[l])
        o1 = D_FOURIER
        o2 = o1 + D_CONV
        o3 = o2 + D_CONV
        o4 = o3 + D_CONV
        u_f = p[..., :o1]
        v, b_g, c_g = p[..., o1:o2], p[..., o2:o3], p[..., o3:o4]
        g_logits = p[..., o4:] + b_gate[l]
        gate_a = jax.nn.sigmoid(g_logits[..., :D])
        gate_b = jax.nn.sigmoid(g_logits[..., D:])
        y_a = jnp.einsum('bsk,kd->bsd', fourier_mix(u_f), w_fourier[l])
        y_b = jnp.einsum('bsk,kd->bsd', b_g * centred_conv3(c_g * v, w_conv[l]), w_conv_out[l])
        z = gate_a * y_a + gate_b * y_b
        mix_out = jnp.einsum('bsd,de->bse', z, w_o[l])
        x = x + gate_m[:, None, :] * mix_out

        h2 = modulate(rms_norm(x, g_norm_moe[l]), shift_f, scale_f)
        moe_out = expert_choice_moe(h2, w_router[l], b_router[l], w_gate_e[l], w_up_e[l], w_down_e[l])
        x = x + gate_f[:, None, :] * moe_out
    return rms_norm(x, g_final)
```

```python
import functools

import numpy as np
import jax
import jax.numpy as jnp
from jax import lax
from jax.experimental import pallas as pl
from jax.experimental.pallas import tpu as pltpu

F32 = jnp.float32
BF16 = jnp.bfloat16
I32 = jnp.int32
HIGHEST = lax.Precision.HIGHEST

FOURIER_GROUPS = 4
FOURIER_GROUP_DIM = 128
D_FOURIER = FOURIER_GROUPS * FOURIER_GROUP_DIM
N_MOD = 6
EC_CAPACITY = 2
RMS_EPS = 1e-6

V7X_LANES = 128
V7X_SUBLANES_F32 = 8
V7X_SUBLANES_BF16 = 16
V7X_VMEM_BYTES = 64 * 1024 * 1024
VMEM_LIMIT_BYTES = V7X_VMEM_BYTES - 6 * 1024 * 1024

ROW_TILE = 512
FFT_N1 = 128
FFT_COL_TILE = 4096
FFT2_K_TILE = 8
TOKEN_BLOCK = 256
FFN_CHUNK = 512


def _dot(a, b):
    return jnp.dot(a, b, preferred_element_type=F32)


def _params(semantics):
    return pltpu.CompilerParams(dimension_semantics=semantics,
                                vmem_limit_bytes=VMEM_LIMIT_BYTES)


def _resident(block_shape, index_map):
    return pl.BlockSpec(block_shape, index_map, pipeline_mode=pl.Buffered(1))


def _mod_kernel(c_ref, w_ref, b_ref, o_ref):
    c = c_ref[...]
    c_act = c * jax.nn.sigmoid(c)
    o_ref[...] = jnp.dot(c_act, w_ref[...], preferred_element_type=F32,
                         precision=HIGHEST) + b_ref[...]


def _mod_call(c_pad, w_ada, b_ada):
    rows, d = c_pad.shape
    n = w_ada.shape[1]
    tn = 1536
    return pl.pallas_call(
        _mod_kernel,
        grid=(n // tn,),
        in_specs=[pl.BlockSpec((rows, d), lambda j: (0, 0)),
                  pl.BlockSpec((d, tn), lambda j: (0, j)),
                  pl.BlockSpec((1, tn), lambda j: (0, j))],
        out_specs=pl.BlockSpec((rows, tn), lambda j: (0, j)),
        out_shape=jax.ShapeDtypeStruct((rows, n), F32),
        compiler_params=_params(("arbitrary",)),
        name="mod",
    )(c_pad, w_ada, b_ada)


def _rms_modulate(x, g, shift, scale):
    ms = jnp.mean(x * x, axis=-1, keepdims=True)
    y = x * lax.rsqrt(ms + RMS_EPS) * g
    return y * (1.0 + scale) + shift


def _proj_kernel(x_ref, mod_ref, g_ref, win_ref, bgate_ref, fc_ref,
                 wre_ref, wim_ref, q_ref, bg_ref, ga_ref, gb_ref, *, d, d_conv):
    shift = mod_ref[0, 0:1, :]
    scale = mod_ref[0, 1:2, :]
    h = _rms_modulate(x_ref[...], g_ref[...], shift, scale).astype(BF16)

    o1 = D_FOURIER
    o2 = o1 + d_conv
    o3 = o2 + d_conv
    o4 = o3 + d_conv
    uf = _dot(h, win_ref[:, 0:o1]).astype(BF16)
    gd = FOURIER_GROUP_DIM
    for g in range(FOURIER_GROUPS):
        w = _dot(uf[:, g * gd:(g + 1) * gd], fc_ref[...])
        wre_ref[:, g * gd:(g + 1) * gd] = w[:, :gd].astype(BF16)
        wim_ref[:, g * gd:(g + 1) * gd] = w[:, gd:].astype(BF16)
    v = _dot(h, win_ref[:, o1:o2])
    cg = _dot(h, win_ref[:, o3:o4])
    q_ref[...] = (cg * v).astype(BF16)
    bg_ref[...] = _dot(h, win_ref[:, o2:o3]).astype(BF16)
    ga_ref[...] = jax.nn.sigmoid(_dot(h, win_ref[:, o4:o4 + d]) + bgate_ref[:, 0:d]).astype(BF16)
    gb_ref[...] = jax.nn.sigmoid(
        _dot(h, win_ref[:, o4 + d:o4 + 2 * d]) + bgate_ref[:, d:2 * d]).astype(BF16)


def _proj_call(x2, mod3, g_mix, w_in, b_gate, fc, *, seq):
    t, d = x2.shape
    k_in = w_in.shape[1]
    d_conv = (k_in - D_FOURIER - 2 * d) // 3
    tm = ROW_TILE
    tpb = seq // tm
    row = lambda i: (i, 0)
    const = lambda i: (0, 0)
    out_sds = lambda n: jax.ShapeDtypeStruct((t, n), BF16)
    return pl.pallas_call(
        functools.partial(_proj_kernel, d=d, d_conv=d_conv),
        grid=(t // tm,),
        in_specs=[pl.BlockSpec((tm, d), row),
                  pl.BlockSpec((1, N_MOD, d), lambda i: (i // tpb, 0, 0)),
                  pl.BlockSpec((1, d), const),
                  _resident((d, k_in), const),
                  pl.BlockSpec((1, 2 * d), const),
                  pl.BlockSpec(fc.shape, const)],
        out_specs=[pl.BlockSpec((tm, D_FOURIER), row), pl.BlockSpec((tm, D_FOURIER), row),
                   pl.BlockSpec((tm, d_conv), row), pl.BlockSpec((tm, d_conv), row),
                   pl.BlockSpec((tm, d), row), pl.BlockSpec((tm, d), row)],
        out_shape=[out_sds(D_FOURIER), out_sds(D_FOURIER), out_sds(d_conv), out_sds(d_conv),
                   out_sds(d), out_sds(d)],
        compiler_params=_params(("parallel",)),
        name="proj",
    )(x2, mod3, g_mix, w_in, b_gate, fc)


def _fft1_kernel(wre_ref, wim_ref, f1_ref, cos_ref, sin_ref, tr_ref, ti_ref, *, n1, nt):
    w = jnp.concatenate([wre_ref[0], wim_ref[0]], axis=0)
    t = _dot(f1_ref[...], w)
    reps = D_FOURIER // V7X_LANES
    for j in range(nt):
        cols = slice(j * D_FOURIER, (j + 1) * D_FOURIER)
        lanes = slice(j * V7X_LANES, (j + 1) * V7X_LANES)
        c = jnp.concatenate([cos_ref[:, lanes]] * reps, axis=1)
        s = jnp.concatenate([sin_ref[:, lanes]] * reps, axis=1)
        a = t[:n1, cols]
        b = t[n1:, cols]
        tr_ref[0, :, cols] = (a * c + b * s).astype(BF16)
        ti_ref[0, :, cols] = (b * c - a * s).astype(BF16)


def _fft1_call(wre3, wim3, f1, cos_t, sin_t):
    bsz, n1, cols = wre3.shape
    tn = min(FFT_COL_TILE, cols)
    nt = tn // D_FOURIER
    blk = lambda b, j: (b, 0, j)
    sds = jax.ShapeDtypeStruct((bsz, n1, cols), BF16)
    return pl.pallas_call(
        functools.partial(_fft1_kernel, n1=n1, nt=nt),
        grid=(bsz, cols // tn),
        in_specs=[pl.BlockSpec((1, n1, tn), blk), pl.BlockSpec((1, n1, tn), blk),
                  pl.BlockSpec(f1.shape, lambda b, j: (0, 0)),
                  pl.BlockSpec((n1, nt * V7X_LANES), lambda b, j: (0, j)),
                  pl.BlockSpec((n1, nt * V7X_LANES), lambda b, j: (0, j))],
        out_specs=[pl.BlockSpec((1, n1, tn), blk), pl.BlockSpec((1, n1, tn), blk)],
        out_shape=[sds, sds],
        compiler_params=_params(("parallel", "parallel")),
        name="fft1",
    )(wre3, wim3, f1, cos_t, sin_t)


def _fft2_kernel(tr_ref, ti_ref, f2_ref, o_ref, *, kb):
    for k in range(kb):
        rhs = jnp.concatenate([tr_ref[0, k], ti_ref[0, k]], axis=0)
        o_ref[0, :, k * D_FOURIER:(k + 1) * D_FOURIER] = _dot(f2_ref[...], rhs).astype(BF16)


def _fft2_call(tr4, ti4, f2):
    bsz, n1, n2, dfo = tr4.shape
    kb = FFT2_K_TILE
    blk = lambda b, j: (b, j, 0, 0)
    return pl.pallas_call(
        functools.partial(_fft2_kernel, kb=kb),
        grid=(bsz, n1 // kb),
        in_specs=[pl.BlockSpec((1, kb, n2, dfo), blk), pl.BlockSpec((1, kb, n2, dfo), blk),
                  pl.BlockSpec(f2.shape, lambda b, j: (0, 0))],
        out_specs=pl.BlockSpec((1, n2, kb * dfo), lambda b, j: (b, 0, j)),
        out_shape=jax.ShapeDtypeStruct((bsz, n2, n1 * dfo), BF16),
        compiler_params=_params(("parallel", "parallel")),
        name="fft2",
    )(tr4, ti4, f2)


def _dft_tables(seq):
    n1 = FFT_N1
    n2 = seq // n1
    gd = FOURIER_GROUP_DIM
    total_scale = 1.0 / np.sqrt(float(seq) * gd)
    s_c = 2.0 ** -4
    s_1 = 2.0 ** -3
    s_2 = total_scale / (s_c * s_1)

    def cs(n):
        ang = 2.0 * np.pi * np.outer(np.arange(n), np.arange(n)) / n
        return np.cos(ang), np.sin(ang)

    cc, sc = cs(gd)
    fc = np.concatenate([cc, -sc], axis=1) * s_c
    c1, s1 = cs(n1)
    f1 = np.block([[c1, s1], [-s1, c1]]) * s_1
    c2, s2 = cs(n2)
    f2 = np.concatenate([c2, s2], axis=1) * s_2
    ang = 2.0 * np.pi * np.outer(np.arange(n1), np.arange(n2)) / seq
    cos_t = np.repeat(np.cos(ang), V7X_LANES, axis=1)
    sin_t = np.repeat(np.sin(ang), V7X_LANES, axis=1)
    as_bf16 = lambda m: jnp.asarray(m, F32).astype(BF16)
    return (as_bf16(fc), as_bf16(f1), as_bf16(f2),
            jnp.asarray(cos_t, F32), jnp.asarray(sin_t, F32))


def _mix_kernel(a_ref, q_ref, qprev_ref, qnext_ref, bg_ref, ga_ref, gb_ref, x_ref, mod_ref,
                wconv_ref, wf_ref, wco_ref, wo_ref, gmoe_ref, wrt_ref, br_ref,
                x1_ref, h2_ref, pt_ref, p_ref, *, tm, tpb):
    i = pl.program_id(0)
    first = (i % tpb) == 0
    last = (i % tpb) == tpb - 1
    q = q_ref[...].astype(F32)
    nh = qprev_ref.shape[0]
    hp = jnp.where(first, 0.0, qprev_ref[...].astype(F32)[nh - 1:nh, :])
    hn = jnp.where(last, 0.0, qnext_ref[...].astype(F32)[0:1, :])
    rows = lax.broadcasted_iota(I32, (tm, 1), 0)
    q_prev = jnp.where(rows == 0, hp, pltpu.roll(q, 1, axis=0))
    q_next = jnp.where(rows == tm - 1, hn, pltpu.roll(q, tm - 1, axis=0))
    conv = q_prev * wconv_ref[0:1, :] + q * wconv_ref[1:2, :] + q_next * wconv_ref[2:3, :]
    y_b = _dot((bg_ref[...].astype(F32) * conv).astype(BF16), wco_ref[...])
    y_a = _dot(a_ref[...], wf_ref[...])
    z = ga_ref[...].astype(F32) * y_a + gb_ref[...].astype(F32) * y_b
    mix = _dot(z.astype(BF16), wo_ref[...])
    gate_m = mod_ref[0, 2:3, :]
    x1 = x_ref[...] + gate_m * mix
    x1_ref[...] = x1

    h2 = _rms_modulate(x1, gmoe_ref[...], mod_ref[0, 3:4, :], mod_ref[0, 4:5, :])
    h2_ref[...] = h2.astype(BF16)
    logits = lax.dot_general(wrt_ref[...], h2, (((1,), (1,)), ((), ())),
                             preferred_element_type=F32, precision=HIGHEST) + br_ref[...]
    m = jnp.max(logits, axis=0, keepdims=True)
    e = jnp.exp(logits - m)
    probs_t = e / jnp.sum(e, axis=0, keepdims=True)
    pt_ref[0] = probs_t
    p_ref[...] = probs_t.T


def _mix_call(a2, q, bg, ga, gb, x2, mod3, w_conv, wf, wco, wo, g_moe, wr_t, b_r, *, seq):
    t, d = x2.shape
    n_exp = wr_t.shape[0]
    tm = ROW_TILE
    tpb = seq // tm
    bsz = t // seq
    hb = V7X_SUBLANES_BF16
    row = lambda i: (i, 0)
    const = lambda i: (0, 0)
    return pl.pallas_call(
        functools.partial(_mix_kernel, tm=tm, tpb=tpb),
        grid=(t // tm,),
        in_specs=[pl.BlockSpec((tm, a2.shape[1]), row),
                  pl.BlockSpec((tm, q.shape[1]), row),
                  pl.BlockSpec((hb, q.shape[1]),
                               lambda i: (jnp.maximum(i * (tm // hb) - 1, 0), 0)),
                  pl.BlockSpec((hb, q.shape[1]),
                               lambda i: (jnp.minimum((i + 1) * (tm // hb), t // hb - 1), 0)),
                  pl.BlockSpec((tm, bg.shape[1]), row),
                  pl.BlockSpec((tm, d), row), pl.BlockSpec((tm, d), row),
                  pl.BlockSpec((tm, d), row),
                  pl.BlockSpec((1, N_MOD, d), lambda i: (i // tpb, 0, 0)),
                  pl.BlockSpec(w_conv.shape, const),
                  _resident(wf.shape, const), _resident(wco.shape, const),
                  _resident(wo.shape, const),
                  pl.BlockSpec((1, d), const),
                  pl.BlockSpec(wr_t.shape, const), pl.BlockSpec(b_r.shape, const)],
        out_specs=[pl.BlockSpec((tm, d), row), pl.BlockSpec((tm, d), row),
                   pl.BlockSpec((1, n_exp, tm), lambda i: (i // tpb, 0, i % tpb)),
                   pl.BlockSpec((tm, n_exp), row)],
        out_shape=[jax.ShapeDtypeStruct((t, d), F32), jax.ShapeDtypeStruct((t, d), BF16),
                   jax.ShapeDtypeStruct((bsz, n_exp, seq), F32),
                   jax.ShapeDtypeStruct((t, n_exp), F32)],
        compiler_params=_params(("parallel",)),
        name="mix",
    )(a2, q, q, q, bg, ga, gb, x2, mod3, w_conv, wf, wco, wo, g_moe, wr_t, b_r)


def _excl_cumsum(mask_f, upper, lower_strict):
    r, rows, lanes = mask_f.shape
    m2 = mask_f.reshape(r * rows, lanes).astype(BF16)
    incl = _dot(m2, upper)
    tot = jnp.broadcast_to(incl[:, lanes - 1:lanes], incl.shape).astype(BF16)
    tot3 = tot.reshape(r, rows, lanes)
    offs = [_dot(lower_strict, tot3[j]) for j in range(r)]
    off = jnp.stack(offs, axis=0)
    return incl.reshape(r, rows, lanes) - mask_f + off


def _select_kernel(p_ref, upper_ref, lower_ref, pos_ref, raw_ref, *, cap):
    p = p_ref[...]
    r = p.shape[0]

    def count(mask):
        c = jnp.sum(mask.astype(F32), axis=2, keepdims=True)
        return jnp.sum(c, axis=1, keepdims=True)

    prefix = jnp.zeros(p.shape, I32)
    for bit in range(29, -1, -1):
        cand = prefix | (1 << bit)
        keep = count(p >= pltpu.bitcast(cand, F32)) >= cap
        prefix = jnp.where(keep, cand, prefix)
    thr = pltpu.bitcast(prefix, F32)
    gt = p > thr
    eq = (p == thr).astype(F32)
    need = cap - count(gt)
    upper = upper_ref[...]
    lower = lower_ref[...]
    rank_eq = _excl_cumsum(eq, upper, lower)
    sel = jnp.where(gt, 1.0, jnp.where(rank_eq < need, eq, 0.0))
    raw = _excl_cumsum(sel, upper, lower)
    raw_i = raw.astype(I32)
    raw_ref[...] = raw_i
    pos_ref[...] = jnp.where(sel > 0.0, raw_i, -1)


def _select_call(p3, cap):
    r, rows, lanes = p3.shape
    upper = jnp.asarray(np.triu(np.ones((lanes, lanes))), BF16)
    lower = jnp.asarray(np.tril(np.ones((rows, rows)), -1), BF16)
    full = lambda shape: pl.BlockSpec(shape, lambda i: (0,) * len(shape))
    sds = jax.ShapeDtypeStruct(p3.shape, I32)
    return pl.pallas_call(
        functools.partial(_select_kernel, cap=cap),
        grid=(1,),
        in_specs=[full(p3.shape), full(upper.shape), full(lower.shape)],
        out_specs=[full(p3.shape), full(p3.shape)],
        out_shape=[sds, sds],
        compiler_params=_params(("arbitrary",)),
        name="select",
    )(p3, upper, lower)


def _moe_kernel(pfx_ref, h2_ref, pos_ref, wg_ref, wu_ref, wd_ref, ye_ref,
                xf_ref, xe_ref, acc_ref, *, n_exp, nb, cap):
    b = pl.program_id(0)
    e = pl.program_id(1)
    f = pl.program_id(2)
    tb = TOKEN_BLOCK
    halves = tb // V7X_LANES

    @pl.when(f == 0)
    def _gather():
        xf_ref[...] = jnp.zeros_like(xf_ref)
        base = (b * n_exp + e) * (nb + 1)
        win_rows = lax.broadcasted_iota(I32, (tb, V7X_LANES), 0)

        def window(k, hk, ws):
            rows = win_rows + ws
            g = [jnp.where(pos_ref[0, pl.ds(halves * k + j, 1), :] == rows, 1.0, 0.0).astype(BF16)
                 for j in range(halves)]
            onehot = jnp.concatenate(g, axis=1)
            xf_ref[pl.ds(ws, tb), :] += _dot(onehot, hk)

        def body(k, carry):
            p0 = pfx_ref[base + k]
            cnt = pfx_ref[base + k + 1] - p0
            start = pl.multiple_of((p0 // V7X_SUBLANES_F32) * V7X_SUBLANES_F32, V7X_SUBLANES_F32)

            @pl.when(cnt > 0)
            def _():
                hk = h2_ref[0, pl.ds(pl.multiple_of(k * tb, tb), tb), :]
                window(k, hk, start)

                @pl.when(p0 - start + cnt > tb)
                def _():
                    window(k, hk, start + tb)
            return carry

        lax.fori_loop(0, nb, body, 0)
        xe_ref[...] = xf_ref[0:cap, :].astype(BF16)
        acc_ref[...] = jnp.zeros_like(acc_ref)

    xe = xe_ref[...]
    a = _dot(xe, wg_ref[0].astype(BF16))
    u = _dot(xe, wu_ref[0].astype(BF16))
    hm = (a * jax.nn.sigmoid(a) * u).astype(BF16)
    acc_ref[...] += _dot(hm, wd_ref[0].astype(BF16))

    @pl.when(f == pl.num_programs(2) - 1)
    def _():
        ye_ref[0, 0] = acc_ref[...].astype(BF16)


def _moe_call(pfx, h2_3, pos3, w_gate, w_up, w_down, *, cap):
    bsz, seq, d = h2_3.shape
    n_exp, _, d_exp = w_gate.shape
    nb = seq // TOKEN_BLOCK
    fc = FFN_CHUNK
    grid_spec = pltpu.PrefetchScalarGridSpec(
        num_scalar_prefetch=1,
        grid=(bsz, n_exp, d_exp // fc),
        in_specs=[_resident((1, seq, d), lambda b, e, f, p: (b, 0, 0)),
                  pl.BlockSpec((1,) + pos3.shape[1:], lambda b, e, f, p: (b * n_exp + e, 0, 0)),
                  pl.BlockSpec((1, d, fc), lambda b, e, f, p: (e, 0, f)),
                  pl.BlockSpec((1, d, fc), lambda b, e, f, p: (e, 0, f)),
                  pl.BlockSpec((1, fc, d), lambda b, e, f, p: (e, f, 0))],
        out_specs=pl.BlockSpec((1, 1, cap, d), lambda b, e, f, p: (b, e, 0, 0)),
        scratch_shapes=[pltpu.VMEM((cap + TOKEN_BLOCK, d), F32),
                        pltpu.VMEM((cap, d), BF16),
                        pltpu.VMEM((cap, d), F32)],
    )
    return pl.pallas_call(
        functools.partial(_moe_kernel, n_exp=n_exp, nb=nb, cap=cap),
        grid_spec=grid_spec,
        out_shape=jax.ShapeDtypeStruct((bsz, n_exp, cap, d), BF16),
        compiler_params=_params(("arbitrary", "arbitrary", "arbitrary")),
        name="moe",
    )(pfx, h2_3, pos3, w_gate, w_up, w_down)


def _combine_kernel(pfx_ref, x1_ref, ye_ref, post_ref, probs_ref, mod_ref, gfin_ref, o_ref,
                    acc_ref, *, n_exp, nb, cap):
    b = pl.program_id(0)
    k = pl.program_id(1)
    tb = TOKEN_BLOCK
    align = V7X_SUBLANES_BF16
    acc_ref[...] = jnp.zeros_like(acc_ref)
    win_cols = lax.broadcasted_iota(I32, (tb, tb), 1)
    for e in range(n_exp):
        base = (b * n_exp + e) * (nb + 1) + k
        p0 = pfx_ref[base]
        end = pfx_ref[base + 1]
        ws1 = pl.multiple_of(jnp.minimum((p0 // align) * align, cap - tb), align)
        ws2 = pl.multiple_of(jnp.minimum(ws1 + tb, cap - tb), align)
        pos_col = post_ref[:, e:e + 1]
        val = probs_ref[:, e:e + 1]

        @pl.when(end > p0)
        def _():
            onehot = jnp.where(pos_col == win_cols + ws1, 1.0, 0.0).astype(BF16)
            acc_ref[...] += val * _dot(onehot, ye_ref[0, e, pl.ds(ws1, tb), :])

        @pl.when(end > ws1 + tb)
        def _():
            cols = win_cols + ws2
            hit = jnp.logical_and(pos_col == cols, cols >= ws1 + tb)
            onehot = jnp.where(hit, 1.0, 0.0).astype(BF16)
            acc_ref[...] += val * _dot(onehot, ye_ref[0, e, pl.ds(ws2, tb), :])

    gate_f = mod_ref[0, 5:6, :]
    x2 = x1_ref[...] + gate_f * acc_ref[...]
    ms = jnp.mean(x2 * x2, axis=-1, keepdims=True)
    o_ref[...] = x2 * lax.rsqrt(ms + RMS_EPS) * gfin_ref[...]


def _combine_call(pfx, x1, ye, pos_t, probs, mod3, g_final, *, seq):
    t, d = x1.shape
    bsz, n_exp, cap, _ = ye.shape
    tb = TOKEN_BLOCK
    nb = seq // tb
    row = lambda b, k, p: (b * nb + k, 0)
    grid_spec = pltpu.PrefetchScalarGridSpec(
        num_scalar_prefetch=1,
        grid=(bsz, nb),
        in_specs=[pl.BlockSpec((tb, d), row),
                  _resident((1, n_exp, cap, d), lambda b, k, p: (b, 0, 0, 0)),
                  pl.BlockSpec((tb, n_exp), row),
                  pl.BlockSpec((tb, n_exp), row),
                  pl.BlockSpec((1, N_MOD, d), lambda b, k, p: (b, 0, 0)),
                  pl.BlockSpec((1, d), lambda b, k, p: (0, 0))],
        out_specs=pl.BlockSpec((tb, d), row),
        scratch_shapes=[pltpu.VMEM((tb, d), F32)],
    )
    return pl.pallas_call(
        functools.partial(_combine_kernel, n_exp=n_exp, nb=nb, cap=cap),
        grid_spec=grid_spec,
        out_shape=jax.ShapeDtypeStruct((t, d), F32),
        compiler_params=_params(("arbitrary", "arbitrary")),
        name="combine",
    )(pfx, x1, ye, pos_t, probs, mod3, g_final)


def _layer(x2, c_act_in, w_ada, b_ada, g_norm_mix, w_in, b_gate, w_fourier, w_conv, w_conv_out,
           w_o, g_norm_moe, w_router, b_router, w_gate_e, w_up_e, w_down_e, *, bsz, seq):
    t, d = x2.shape
    n_exp = w_router.shape[1]
    cap = EC_CAPACITY * seq // n_exp
    n1 = FFT_N1
    n2 = seq // n1
    fc, f1, f2, cos_t, sin_t = _dft_tables(seq)

    mod = _mod_call(c_act_in, w_ada, b_ada.reshape(1, -1))[:bsz]
    mod3 = mod.reshape(bsz, N_MOD, d)

    wre, wim, q, bg, ga, gb = _proj_call(
        x2, mod3, g_norm_mix.reshape(1, d), w_in.astype(BF16), b_gate.reshape(1, -1), fc, seq=seq)

    cols = n2 * D_FOURIER
    tr, ti = _fft1_call(wre.reshape(bsz, n1, cols), wim.reshape(bsz, n1, cols), f1, cos_t, sin_t)
    a3 = _fft2_call(tr.reshape(bsz, n1, n2, D_FOURIER), ti.reshape(bsz, n1, n2, D_FOURIER), f2)
    a2 = a3.reshape(t, D_FOURIER)

    x1, h2, probs_t, probs = _mix_call(
        a2, q, bg, ga, gb, x2, mod3, w_conv, w_fourier.astype(BF16), w_conv_out.astype(BF16),
        w_o.astype(BF16), g_norm_moe.reshape(1, d), w_router.T, b_router.reshape(n_exp, 1),
        seq=seq)

    pos, raw = _select_call(probs_t.reshape(bsz * n_exp, seq // V7X_LANES, V7X_LANES), cap)
    stride = TOKEN_BLOCK // V7X_LANES
    starts = raw[:, ::stride, 0]
    pfx = jnp.concatenate([starts, jnp.full((bsz * n_exp, 1), cap, I32)], axis=1).reshape(-1)
    pos_t = jnp.transpose(pos.reshape(bsz, n_exp, seq), (0, 2, 1)).reshape(t, n_exp)

    ye = _moe_call(pfx, h2.reshape(bsz, seq, d), pos, w_gate_e, w_up_e, w_down_e, cap=cap)
    return ye, pfx, x1, pos_t, probs, mod3


def kernel(x, c, w_ada, b_ada, g_norm_mix, w_in, b_gate, w_fourier, w_conv, w_conv_out, w_o,
           g_norm_moe, w_router, b_router, w_gate_e, w_up_e, w_down_e, g_final):
    bsz, seq, d = x.shape
    assert w_ada.shape[0] == 1
    assert seq % (FFT_N1 * V7X_SUBLANES_F32) == 0 and seq % ROW_TILE == 0
    assert (EC_CAPACITY * seq // w_router.shape[2]) >= TOKEN_BLOCK
    c_pad = jnp.pad(c, ((0, V7X_SUBLANES_F32 - bsz), (0, 0)))
    x2 = x.reshape(bsz * seq, d)
    ye, pfx, x1, pos_t, probs, mod3 = _layer(
        x2, c_pad, w_ada[0], b_ada[0], g_norm_mix[0], w_in[0], b_gate[0], w_fourier[0],
        w_conv[0], w_conv_out[0], w_o[0], g_norm_moe[0], w_router[0], b_router[0],
        w_gate_e[0], w_up_e[0], w_down_e[0], bsz=bsz, seq=seq)
    out = _combine_call(pfx, x1, ye, pos_t, probs, mod3, g_final.reshape(1, d), seq=seq)
    return out.reshape(bsz, seq, d)
```

```python
import functools

import numpy as np
import jax
import jax.numpy as jnp
from jax import lax
from jax.experimental import pallas as pl
from jax.experimental.pallas import tpu as pltpu

F32 = jnp.float32
BF16 = jnp.bfloat16
I32 = jnp.int32
HIGHEST = lax.Precision.HIGHEST

FOURIER_GROUPS = 4
FOURIER_GROUP_DIM = 128
D_FOURIER = FOURIER_GROUPS * FOURIER_GROUP_DIM
N_MOD = 6
EC_CAPACITY = 2
RMS_EPS = 1e-6

V7X_LANES = 128
V7X_SUBLANES_F32 = 8
V7X_SUBLANES_BF16 = 16
V7X_VMEM_BYTES = 64 * 1024 * 1024
VMEM_LIMIT_BYTES = V7X_VMEM_BYTES - 6 * 1024 * 1024

ROW_TILE = 512
FFT_N1 = 128
FFT_COL_TILE = 4096
FFT2_K_TILE = 8
TOKEN_BLOCK = 256
GATHER_CHUNK = 64
COMBINE_WINDOW = 128
FFN_CHUNK = 512


def _dot(a, b):
    return jnp.dot(a, b, preferred_element_type=F32)


def _params(semantics):
    return pltpu.CompilerParams(dimension_semantics=semantics,
                                vmem_limit_bytes=VMEM_LIMIT_BYTES)


def _resident(block_shape, index_map):
    return pl.BlockSpec(block_shape, index_map, pipeline_mode=pl.Buffered(1))


def _mod_kernel(c_ref, w_ref, b_ref, o_ref):
    c = c_ref[...]
    c_act = c * jax.nn.sigmoid(c)
    o_ref[...] = jnp.dot(c_act, w_ref[...], preferred_element_type=F32,
                         precision=HIGHEST) + b_ref[...]


def _mod_call(c_pad, w_ada, b_ada):
    rows, d = c_pad.shape
    n = w_ada.shape[1]
    tn = 1536
    return pl.pallas_call(
        _mod_kernel,
        grid=(n // tn,),
        in_specs=[pl.BlockSpec((rows, d), lambda j: (0, 0)),
                  pl.BlockSpec((d, tn), lambda j: (0, j)),
                  pl.BlockSpec((1, tn), lambda j: (0, j))],
        out_specs=pl.BlockSpec((rows, tn), lambda j: (0, j)),
        out_shape=jax.ShapeDtypeStruct((rows, n), F32),
        compiler_params=_params(("arbitrary",)),
        name="mod",
    )(c_pad, w_ada, b_ada)


def _rms_modulate(x, g, shift, scale):
    ms = jnp.mean(x * x, axis=-1, keepdims=True)
    y = x * lax.rsqrt(ms + RMS_EPS) * g
    return y * (1.0 + scale) + shift


def _proj_kernel(x_ref, mod_ref, g_ref, win_ref, bgate_ref, fc_ref,
                 wre_ref, wim_ref, q_ref, bg_ref, ga_ref, gb_ref, *, d, d_conv):
    shift = mod_ref[0, 0:1, :]
    scale = mod_ref[0, 1:2, :]
    h = _rms_modulate(x_ref[...], g_ref[...], shift, scale).astype(BF16)

    o1 = D_FOURIER
    o2 = o1 + d_conv
    o3 = o2 + d_conv
    o4 = o3 + d_conv
    uf = _dot(h, win_ref[:, 0:o1]).astype(BF16)
    gd = FOURIER_GROUP_DIM
    for g in range(FOURIER_GROUPS):
        w = _dot(uf[:, g * gd:(g + 1) * gd], fc_ref[...])
        wre_ref[:, g * gd:(g + 1) * gd] = w[:, :gd].astype(BF16)
        wim_ref[:, g * gd:(g + 1) * gd] = w[:, gd:].astype(BF16)
    v = _dot(h, win_ref[:, o1:o2])
    cg = _dot(h, win_ref[:, o3:o4])
    q_ref[...] = (cg * v).astype(BF16)
    bg_ref[...] = _dot(h, win_ref[:, o2:o3]).astype(BF16)
    ga_ref[...] = jax.nn.sigmoid(_dot(h, win_ref[:, o4:o4 + d]) + bgate_ref[:, 0:d]).astype(BF16)
    gb_ref[...] = jax.nn.sigmoid(
        _dot(h, win_ref[:, o4 + d:o4 + 2 * d]) + bgate_ref[:, d:2 * d]).astype(BF16)


def _proj_call(x2, mod3, g_mix, w_in, b_gate, fc, *, seq):
    t, d = x2.shape
    k_in = w_in.shape[1]
    d_conv = (k_in - D_FOURIER - 2 * d) // 3
    tm = ROW_TILE
    tpb = seq // tm
    row = lambda i: (i, 0)
    const = lambda i: (0, 0)
    out_sds = lambda n: jax.ShapeDtypeStruct((t, n), BF16)
    return pl.pallas_call(
        functools.partial(_proj_kernel, d=d, d_conv=d_conv),
        grid=(t // tm,),
        in_specs=[pl.BlockSpec((tm, d), row),
                  pl.BlockSpec((1, N_MOD, d), lambda i: (i // tpb, 0, 0)),
                  pl.BlockSpec((1, d), const),
                  _resident((d, k_in), const),
                  pl.BlockSpec((1, 2 * d), const),
                  pl.BlockSpec(fc.shape, const)],
        out_specs=[pl.BlockSpec((tm, D_FOURIER), row), pl.BlockSpec((tm, D_FOURIER), row),
                   pl.BlockSpec((tm, d_conv), row), pl.BlockSpec((tm, d_conv), row),
                   pl.BlockSpec((tm, d), row), pl.BlockSpec((tm, d), row)],
        out_shape=[out_sds(D_FOURIER), out_sds(D_FOURIER), out_sds(d_conv), out_sds(d_conv),
                   out_sds(d), out_sds(d)],
        compiler_params=_params(("parallel",)),
        name="proj",
    )(x2, mod3, g_mix, w_in, b_gate, fc)


def _fft1_kernel(wre_ref, wim_ref, f1_ref, cos_ref, sin_ref, tr_ref, ti_ref, *, n1, nt):
    w = jnp.concatenate([wre_ref[0], wim_ref[0]], axis=0)
    t = _dot(f1_ref[...], w)
    reps = D_FOURIER // V7X_LANES
    for j in range(nt):
        cols = slice(j * D_FOURIER, (j + 1) * D_FOURIER)
        lanes = slice(j * V7X_LANES, (j + 1) * V7X_LANES)
        c = jnp.concatenate([cos_ref[:, lanes]] * reps, axis=1)
        s = jnp.concatenate([sin_ref[:, lanes]] * reps, axis=1)
        a = t[:n1, cols]
        b = t[n1:, cols]
        tr_ref[0, :, cols] = (a * c + b * s).astype(BF16)
        ti_ref[0, :, cols] = (b * c - a * s).astype(BF16)


def _fft1_call(wre3, wim3, f1, cos_t, sin_t):
    bsz, n1, cols = wre3.shape
    tn = min(FFT_COL_TILE, cols)
    nt = tn // D_FOURIER
    blk = lambda b, j: (b, 0, j)
    sds = jax.ShapeDtypeStruct((bsz, n1, cols), BF16)
    return pl.pallas_call(
        functools.partial(_fft1_kernel, n1=n1, nt=nt),
        grid=(bsz, cols // tn),
        in_specs=[pl.BlockSpec((1, n1, tn), blk), pl.BlockSpec((1, n1, tn), blk),
                  pl.BlockSpec(f1.shape, lambda b, j: (0, 0)),
                  pl.BlockSpec((n1, nt * V7X_LANES), lambda b, j: (0, j)),
                  pl.BlockSpec((n1, nt * V7X_LANES), lambda b, j: (0, j))],
        out_specs=[pl.BlockSpec((1, n1, tn), blk), pl.BlockSpec((1, n1, tn), blk)],
        out_shape=[sds, sds],
        compiler_params=_params(("parallel", "parallel")),
        name="fft1",
    )(wre3, wim3, f1, cos_t, sin_t)


def _fft2_kernel(tr_ref, ti_ref, f2_ref, o_ref, *, kb):
    for k in range(kb):
        rhs = jnp.concatenate([tr_ref[0, k], ti_ref[0, k]], axis=0)
        o_ref[0, :, k * D_FOURIER:(k + 1) * D_FOURIER] = _dot(f2_ref[...], rhs).astype(BF16)


def _fft2_call(tr4, ti4, f2):
    bsz, n1, n2, dfo = tr4.shape
    kb = FFT2_K_TILE
    blk = lambda b, j: (b, j, 0, 0)
    return pl.pallas_call(
        functools.partial(_fft2_kernel, kb=kb),
        grid=(bsz, n1 // kb),
        in_specs=[pl.BlockSpec((1, kb, n2, dfo), blk), pl.BlockSpec((1, kb, n2, dfo), blk),
                  pl.BlockSpec(f2.shape, lambda b, j: (0, 0))],
        out_specs=pl.BlockSpec((1, n2, kb * dfo), lambda b, j: (b, 0, j)),
        out_shape=jax.ShapeDtypeStruct((bsz, n2, n1 * dfo), BF16),
        compiler_params=_params(("parallel", "parallel")),
        name="fft2",
    )(tr4, ti4, f2)


def _dft_tables(seq):
    n1 = FFT_N1
    n2 = seq // n1
    gd = FOURIER_GROUP_DIM
    total_scale = 1.0 / np.sqrt(float(seq) * gd)
    s_c = 2.0 ** -4
    s_1 = 2.0 ** -3
    s_2 = total_scale / (s_c * s_1)

    def cs(n):
        ang = 2.0 * np.pi * np.outer(np.arange(n), np.arange(n)) / n
        return np.cos(ang), np.sin(ang)

    cc, sc = cs(gd)
    fc = np.concatenate([cc, -sc], axis=1) * s_c
    c1, s1 = cs(n1)
    f1 = np.block([[c1, s1], [-s1, c1]]) * s_1
    c2, s2 = cs(n2)
    f2 = np.concatenate([c2, s2], axis=1) * s_2
    ang = 2.0 * np.pi * np.outer(np.arange(n1), np.arange(n2)) / seq
    cos_t = np.repeat(np.cos(ang), V7X_LANES, axis=1)
    sin_t = np.repeat(np.sin(ang), V7X_LANES, axis=1)
    as_bf16 = lambda m: jnp.asarray(m, F32).astype(BF16)
    return (as_bf16(fc), as_bf16(f1), as_bf16(f2),
            jnp.asarray(cos_t, F32), jnp.asarray(sin_t, F32))


def _mix_kernel(a_ref, q_ref, qprev_ref, qnext_ref, bg_ref, ga_ref, gb_ref, x_ref, mod_ref,
                wconv_ref, wf_ref, wco_ref, wo_ref, gmoe_ref, wrt_ref, br_ref,
                x1_ref, h2_ref, pt_ref, *, tm, tpb):
    i = pl.program_id(0)
    first = (i % tpb) == 0
    last = (i % tpb) == tpb - 1
    q = q_ref[...].astype(F32)
    nh = qprev_ref.shape[0]
    hp = jnp.where(first, 0.0, qprev_ref[...].astype(F32)[nh - 1:nh, :])
    hn = jnp.where(last, 0.0, qnext_ref[...].astype(F32)[0:1, :])
    rows = lax.broadcasted_iota(I32, (tm, 1), 0)
    q_prev = jnp.where(rows == 0, hp, pltpu.roll(q, 1, axis=0))
    q_next = jnp.where(rows == tm - 1, hn, pltpu.roll(q, tm - 1, axis=0))
    conv = q_prev * wconv_ref[0:1, :] + q * wconv_ref[1:2, :] + q_next * wconv_ref[2:3, :]
    y_b = _dot((bg_ref[...].astype(F32) * conv).astype(BF16), wco_ref[...])
    y_a = _dot(a_ref[...], wf_ref[...])
    z = ga_ref[...].astype(F32) * y_a + gb_ref[...].astype(F32) * y_b
    mix = _dot(z.astype(BF16), wo_ref[...])
    gate_m = mod_ref[0, 2:3, :]
    x1 = x_ref[...] + gate_m * mix
    x1_ref[...] = x1

    h2 = _rms_modulate(x1, gmoe_ref[...], mod_ref[0, 3:4, :], mod_ref[0, 4:5, :])
    h2_ref[...] = h2.astype(BF16)
    logits = lax.dot_general(wrt_ref[...], h2, (((1,), (1,)), ((), ())),
                             preferred_element_type=F32, precision=HIGHEST) + br_ref[...]
    m = jnp.max(logits, axis=0, keepdims=True)
    e = jnp.exp(logits - m)
    probs_t = e / jnp.sum(e, axis=0, keepdims=True)
    pt_ref[0] = probs_t


def _mix_call(a2, q, bg, ga, gb, x2, mod3, w_conv, wf, wco, wo, g_moe, wr_t, b_r, *, seq):
    t, d = x2.shape
    n_exp = wr_t.shape[0]
    tm = ROW_TILE
    tpb = seq // tm
    bsz = t // seq
    hb = V7X_SUBLANES_BF16
    row = lambda i: (i, 0)
    const = lambda i: (0, 0)
    return pl.pallas_call(
        functools.partial(_mix_kernel, tm=tm, tpb=tpb),
        grid=(t // tm,),
        in_specs=[pl.BlockSpec((tm, a2.shape[1]), row),
                  pl.BlockSpec((tm, q.shape[1]), row),
                  pl.BlockSpec((hb, q.shape[1]),
                               lambda i: (jnp.maximum(i * (tm // hb) - 1, 0), 0)),
                  pl.BlockSpec((hb, q.shape[1]),
                               lambda i: (jnp.minimum((i + 1) * (tm // hb), t // hb - 1), 0)),
                  pl.BlockSpec((tm, bg.shape[1]), row),
                  pl.BlockSpec((tm, d), row), pl.BlockSpec((tm, d), row),
                  pl.BlockSpec((tm, d), row),
                  pl.BlockSpec((1, N_MOD, d), lambda i: (i // tpb, 0, 0)),
                  pl.BlockSpec(w_conv.shape, const),
                  _resident(wf.shape, const), _resident(wco.shape, const),
                  _resident(wo.shape, const),
                  pl.BlockSpec((1, d), const),
                  pl.BlockSpec(wr_t.shape, const), pl.BlockSpec(b_r.shape, const)],
        out_specs=[pl.BlockSpec((tm, d), row), pl.BlockSpec((tm, d), row),
                   pl.BlockSpec((1, n_exp, tm), lambda i: (i // tpb, 0, i % tpb))],
        out_shape=[jax.ShapeDtypeStruct((t, d), F32), jax.ShapeDtypeStruct((t, d), BF16),
                   jax.ShapeDtypeStruct((bsz, n_exp, seq), F32)],
        compiler_params=_params(("parallel",)),
        name="mix",
    )(a2, q, q, q, bg, ga, gb, x2, mod3, w_conv, wf, wco, wo, g_moe, wr_t, b_r)


def _excl_cumsum(mask_f, upper, lower_strict):
    r, rows, lanes = mask_f.shape
    m2 = mask_f.reshape(r * rows, lanes).astype(BF16)
    incl = _dot(m2, upper)
    tot = jnp.broadcast_to(incl[:, lanes - 1:lanes], incl.shape).astype(BF16)
    tot3 = tot.reshape(r, rows, lanes)
    offs = [_dot(lower_strict, tot3[j]) for j in range(r)]
    off = jnp.stack(offs, axis=0)
    return incl.reshape(r, rows, lanes) - mask_f + off


def _select_kernel(p_ref, upper_ref, lower_ref, pos_ref, raw_ref, *, cap):
    p = p_ref[...]
    r = p.shape[0]

    def count(mask):
        c = jnp.sum(mask.astype(F32), axis=2, keepdims=True)
        return jnp.sum(c, axis=1, keepdims=True)

    prefix = jnp.zeros(p.shape, I32)
    for bit in range(29, -1, -1):
        cand = prefix | (1 << bit)
        keep = count(p >= pltpu.bitcast(cand, F32)) >= cap
        prefix = jnp.where(keep, cand, prefix)
    thr = pltpu.bitcast(prefix, F32)
    gt = p > thr
    eq = (p == thr).astype(F32)
    need = cap - count(gt)
    upper = upper_ref[...]
    lower = lower_ref[...]
    rank_eq = _excl_cumsum(eq, upper, lower)
    sel = jnp.where(gt, 1.0, jnp.where(rank_eq < need, eq, 0.0))
    raw = _excl_cumsum(sel, upper, lower)
    raw_i = raw.astype(I32)
    raw_ref[...] = raw_i
    pos_ref[...] = jnp.where(sel > 0.0, raw_i, -1)


def _select_call(p3, cap):
    r, rows, lanes = p3.shape
    upper = jnp.asarray(np.triu(np.ones((lanes, lanes))), BF16)
    lower = jnp.asarray(np.tril(np.ones((rows, rows)), -1), BF16)
    full = lambda shape: pl.BlockSpec(shape, lambda i: (0,) * len(shape))
    sds = jax.ShapeDtypeStruct(p3.shape, I32)
    return pl.pallas_call(
        functools.partial(_select_kernel, cap=cap),
        grid=(1,),
        in_specs=[full(p3.shape), full(upper.shape), full(lower.shape)],
        out_specs=[full(p3.shape), full(p3.shape)],
        out_shape=[sds, sds],
        compiler_params=_params(("arbitrary",)),
        name="select",
    )(p3, upper, lower)


def _moe_kernel(pfx_ref, h2_ref, pos_ref, prob_ref, wg_ref, wu_ref, wd_ref, ye_ref,
                xf_ref, val_ref, xe_ref, acc_ref, *, n_exp, nb, cap):
    b = pl.program_id(0)
    e = pl.program_id(1)
    f = pl.program_id(2)
    tb = TOKEN_BLOCK
    ch = GATHER_CHUNK
    halves = tb // V7X_LANES
    sub = V7X_SUBLANES_F32

    @pl.when(f == 0)
    def _gather():
        xf_ref[...] = jnp.zeros_like(xf_ref)
        val_ref[...] = jnp.zeros_like(val_ref)
        base = (b * n_exp + e) * (nb + 1)
        chunk_rows = lax.broadcasted_iota(I32, (ch, V7X_LANES), 0)

        def body(k, carry):
            p0 = pfx_ref[base + k]
            cnt = pfx_ref[base + k + 1] - p0
            start = (p0 // sub) * sub
            n_chunks = jnp.where(cnt > 0, (p0 - start + cnt + ch - 1) // ch, 0)
            tok0 = pl.multiple_of(k * tb, tb)

            def chunk(c, carry2):
                ws = pl.multiple_of(start + c * ch, sub)
                rows = chunk_rows + ws
                hits = [jnp.where(pos_ref[0, pl.ds(halves * k + j, 1), :] == rows, 1.0, 0.0)
                        for j in range(halves)]
                onehot = jnp.concatenate([h.astype(BF16) for h in hits], axis=1)
                xf_ref[pl.ds(ws, ch), :] += _dot(onehot, h2_ref[0, pl.ds(tok0, tb), :])
                v = hits[0] * prob_ref[0, pl.ds(halves * k, 1), :]
                for j in range(1, halves):
                    v = v + hits[j] * prob_ref[0, pl.ds(halves * k + j, 1), :]
                val_ref[pl.ds(ws, ch), :] += jnp.sum(v, axis=1, keepdims=True)
                return carry2

            lax.fori_loop(0, n_chunks, chunk, 0)
            return carry

        lax.fori_loop(0, nb, body, 0)
        xe_ref[...] = xf_ref[0:cap, :].astype(BF16)
        acc_ref[...] = jnp.zeros_like(acc_ref)

    xe = xe_ref[...]
    a = _dot(xe, wg_ref[0].astype(BF16))
    u = _dot(xe, wu_ref[0].astype(BF16))
    hm = (a * jax.nn.sigmoid(a) * u).astype(BF16)
    acc_ref[...] += _dot(hm, wd_ref[0].astype(BF16))

    @pl.when(f == pl.num_programs(2) - 1)
    def _():
        ye_ref[0, 0] = (acc_ref[...] * val_ref[0:cap, :]).astype(BF16)


def _moe_call(pfx, h2_3, pos3, prob3, w_gate, w_up, w_down, *, cap):
    bsz, seq, d = h2_3.shape
    n_exp, _, d_exp = w_gate.shape
    nb = seq // TOKEN_BLOCK
    fc = FFN_CHUNK
    grid_spec = pltpu.PrefetchScalarGridSpec(
        num_scalar_prefetch=1,
        grid=(bsz, n_exp, d_exp // fc),
        in_specs=[_resident((1, seq, d), lambda b, e, f, p: (b, 0, 0)),
                  pl.BlockSpec((1,) + pos3.shape[1:], lambda b, e, f, p: (b * n_exp + e, 0, 0)),
                  pl.BlockSpec((1,) + prob3.shape[1:], lambda b, e, f, p: (b * n_exp + e, 0, 0)),
                  pl.BlockSpec((1, d, fc), lambda b, e, f, p: (e, 0, f)),
                  pl.BlockSpec((1, d, fc), lambda b, e, f, p: (e, 0, f)),
                  pl.BlockSpec((1, fc, d), lambda b, e, f, p: (e, f, 0))],
        out_specs=pl.BlockSpec((1, 1, cap, d), lambda b, e, f, p: (b, e, 0, 0)),
        scratch_shapes=[pltpu.VMEM((cap + GATHER_CHUNK, d), F32),
                        pltpu.VMEM((cap + GATHER_CHUNK, 1), F32),
                        pltpu.VMEM((cap, d), BF16),
                        pltpu.VMEM((cap, d), F32)],
    )
    return pl.pallas_call(
        functools.partial(_moe_kernel, n_exp=n_exp, nb=nb, cap=cap),
        grid_spec=grid_spec,
        out_shape=jax.ShapeDtypeStruct((bsz, n_exp, cap, d), BF16),
        compiler_params=_params(("arbitrary", "arbitrary", "arbitrary")),
        name="moe",
    )(pfx, h2_3, pos3, prob3, w_gate, w_up, w_down)


def _combine_kernel(pfx_ref, x1_ref, ye_ref, post_ref, mod_ref, gfin_ref, o_ref,
                    ycat_ref, scat_ref, acc_ref, *, n_exp, nb, cap):
    b = pl.program_id(0)
    k = pl.program_id(1)
    tb = TOKEN_BLOCK
    win = COMBINE_WINDOW
    align = V7X_SUBLANES_BF16
    p0s, ends, wss = [], [], []
    fits = None
    for e in range(n_exp):
        base = (b * n_exp + e) * (nb + 1) + k
        p0 = pfx_ref[base]
        end = pfx_ref[base + 1]
        ws = pl.multiple_of(jnp.minimum((p0 // align) * align, cap - win), align)
        ok = end <= ws + win
        fits = ok if fits is None else jnp.logical_and(fits, ok)
        p0s.append(p0)
        ends.append(end)
        wss.append(ws)

    @pl.when(fits)
    def _():
        cols = lax.broadcasted_iota(I32, (tb, win), 1)
        for e in range(n_exp):
            ycat_ref[e * win:(e + 1) * win, :] = ye_ref[0, e, pl.ds(wss[e], win), :]
            hit = post_ref[:, e:e + 1] == cols + wss[e]
            scat_ref[:, e * win:(e + 1) * win] = jnp.where(hit, 1.0, 0.0).astype(BF16)
        acc_ref[...] = _dot(scat_ref[...], ycat_ref[...])

    @pl.when(jnp.logical_not(fits))
    def _():
        acc_ref[...] = jnp.zeros_like(acc_ref)
        cols = lax.broadcasted_iota(I32, (tb, tb), 1)
        for e in range(n_exp):
            ws1 = pl.multiple_of(jnp.minimum((p0s[e] // align) * align, cap - tb), align)
            ws2 = pl.multiple_of(jnp.minimum(ws1 + tb, cap - tb), align)
            pos_col = post_ref[:, e:e + 1]

            @pl.when(ends[e] > p0s[e])
            def _():
                onehot = jnp.where(pos_col == cols + ws1, 1.0, 0.0).astype(BF16)
                acc_ref[...] += _dot(onehot, ye_ref[0, e, pl.ds(ws1, tb), :])

            @pl.when(ends[e] > ws1 + tb)
            def _():
                c2 = cols + ws2
                hit = jnp.logical_and(pos_col == c2, c2 >= ws1 + tb)
                onehot = jnp.where(hit, 1.0, 0.0).astype(BF16)
                acc_ref[...] += _dot(onehot, ye_ref[0, e, pl.ds(ws2, tb), :])

    gate_f = mod_ref[0, 5:6, :]
    x2 = x1_ref[...] + gate_f * acc_ref[...]
    ms = jnp.mean(x2 * x2, axis=-1, keepdims=True)
    o_ref[...] = x2 * lax.rsqrt(ms + RMS_EPS) * gfin_ref[...]


def _combine_call(pfx, x1, ye, pos_t, mod3, g_final, *, seq):
    t, d = x1.shape
    bsz, n_exp, cap, _ = ye.shape
    tb = TOKEN_BLOCK
    nb = seq // tb
    row = lambda b, k, p: (b * nb + k, 0)
    grid_spec = pltpu.PrefetchScalarGridSpec(
        num_scalar_prefetch=1,
        grid=(bsz, nb),
        in_specs=[pl.BlockSpec((tb, d), row),
                  _resident((1, n_exp, cap, d), lambda b, k, p: (b, 0, 0, 0)),
                  pl.BlockSpec((tb, n_exp), row),
                  pl.BlockSpec((1, N_MOD, d), lambda b, k, p: (b, 0, 0)),
                  pl.BlockSpec((1, d), lambda b, k, p: (0, 0))],
        out_specs=pl.BlockSpec((tb, d), row),
        scratch_shapes=[pltpu.VMEM((n_exp * COMBINE_WINDOW, d), BF16),
                        pltpu.VMEM((tb, n_exp * COMBINE_WINDOW), BF16),
                        pltpu.VMEM((tb, d), F32)],
    )
    return pl.pallas_call(
        functools.partial(_combine_kernel, n_exp=n_exp, nb=nb, cap=cap),
        grid_spec=grid_spec,
        out_shape=jax.ShapeDtypeStruct((t, d), F32),
        compiler_params=_params(("arbitrary", "arbitrary")),
        name="combine",
    )(pfx, x1, ye, pos_t, mod3, g_final)


def _layer(x2, c_act_in, w_ada, b_ada, g_norm_mix, w_in, b_gate, w_fourier, w_conv, w_conv_out,
           w_o, g_norm_moe, w_router, b_router, w_gate_e, w_up_e, w_down_e, *, bsz, seq):
    t, d = x2.shape
    n_exp = w_router.shape[1]
    cap = EC_CAPACITY * seq // n_exp
    n1 = FFT_N1
    n2 = seq // n1
    fc, f1, f2, cos_t, sin_t = _dft_tables(seq)

    mod = _mod_call(c_act_in, w_ada, b_ada.reshape(1, -1))[:bsz]
    mod3 = mod.reshape(bsz, N_MOD, d)

    wre, wim, q, bg, ga, gb = _proj_call(
        x2, mod3, g_norm_mix.reshape(1, d), w_in.astype(BF16), b_gate.reshape(1, -1), fc, seq=seq)

    cols = n2 * D_FOURIER
    tr, ti = _fft1_call(wre.reshape(bsz, n1, cols), wim.reshape(bsz, n1, cols), f1, cos_t, sin_t)
    a3 = _fft2_call(tr.reshape(bsz, n1, n2, D_FOURIER), ti.reshape(bsz, n1, n2, D_FOURIER), f2)
    a2 = a3.reshape(t, D_FOURIER)

    x1, h2, probs_t = _mix_call(
        a2, q, bg, ga, gb, x2, mod3, w_conv, w_fourier.astype(BF16), w_conv_out.astype(BF16),
        w_o.astype(BF16), g_norm_moe.reshape(1, d), w_router.T, b_router.reshape(n_exp, 1),
        seq=seq)

    prob3 = probs_t.reshape(bsz * n_exp, seq // V7X_LANES, V7X_LANES)
    pos, raw = _select_call(prob3, cap)
    stride = TOKEN_BLOCK // V7X_LANES
    starts = raw[:, ::stride, 0]
    pfx = jnp.concatenate([starts, jnp.full((bsz * n_exp, 1), cap, I32)], axis=1).reshape(-1)
    pos_t = jnp.transpose(pos.reshape(bsz, n_exp, seq), (0, 2, 1)).reshape(t, n_exp)

    ye = _moe_call(pfx, h2.reshape(bsz, seq, d), pos, prob3, w_gate_e, w_up_e, w_down_e, cap=cap)
    return ye, pfx, x1, pos_t, mod3


def kernel(x, c, w_ada, b_ada, g_norm_mix, w_in, b_gate, w_fourier, w_conv, w_conv_out, w_o,
           g_norm_moe, w_router, b_router, w_gate_e, w_up_e, w_down_e, g_final):
    bsz, seq, d = x.shape
    assert w_ada.shape[0] == 1
    assert seq % (FFT_N1 * V7X_SUBLANES_F32) == 0 and seq % ROW_TILE == 0
    assert (EC_CAPACITY * seq // w_router.shape[2]) >= TOKEN_BLOCK
    c_pad = jnp.pad(c, ((0, V7X_SUBLANES_F32 - bsz), (0, 0)))
    x2 = x.reshape(bsz * seq, d)
    ye, pfx, x1, pos_t, mod3 = _layer(
        x2, c_pad, w_ada[0], b_ada[0], g_norm_mix[0], w_in[0], b_gate[0], w_fourier[0],
        w_conv[0], w_conv_out[0], w_o[0], g_norm_moe[0], w_router[0], b_router[0],
        w_gate_e[0], w_up_e[0], w_down_e[0], bsz=bsz, seq=seq)
    out = _combine_call(pfx, x1, ye, pos_t, mod3, g_final.reshape(1, d), seq=seq)
    return out.reshape(bsz, seq, d)
```

```python
import functools

import numpy as np
import jax
import jax.numpy as jnp
from jax import lax
from jax.experimental import pallas as pl
from jax.experimental.pallas import tpu as pltpu

F32 = jnp.float32
BF16 = jnp.bfloat16
I32 = jnp.int32
HIGHEST = lax.Precision.HIGHEST

FOURIER_GROUPS = 4
FOURIER_GROUP_DIM = 128
D_FOURIER = FOURIER_GROUPS * FOURIER_GROUP_DIM
N_MOD = 6
EC_CAPACITY = 2
RMS_EPS = 1e-6

V7X_LANES = 128
V7X_SUBLANES_F32 = 8
V7X_SUBLANES_BF16 = 16
V7X_VMEM_BYTES = 64 * 1024 * 1024
VMEM_LIMIT_BYTES = V7X_VMEM_BYTES - 6 * 1024 * 1024

ROW_TILE = 512
FFT_N1 = 128
FFT_N2_TILE = 16
FFT2_K_TILE = 16
TOKEN_BLOCK = 256
GATHER_CHUNK = 64
GATHER_UNROLL = 4
COMBINE_WINDOW = 128
FFN_CHUNK = 512


def _dot(a, b):
    return jnp.dot(a, b, preferred_element_type=F32)


def _params(semantics):
    return pltpu.CompilerParams(dimension_semantics=semantics,
                                vmem_limit_bytes=VMEM_LIMIT_BYTES)


def _pitch(rows):
    p = -(-rows // V7X_SUBLANES_F32)
    return (p if p % 2 else p + 1) * V7X_SUBLANES_F32


def _stage_group(stage_ref, group, value):
    rows = value.shape[0]
    row0 = group * _pitch(rows)
    for l in range(stage_ref.shape[0]):
        stage_ref[l, row0:row0 + rows, :] = value[:, l * V7X_LANES:(l + 1) * V7X_LANES]


def _row_of_each_group(stage_ref, row, groups, rows):
    return jnp.concatenate([stage_ref[l, pl.ds(row, groups, stride=_pitch(rows)), :]
                            for l in range(stage_ref.shape[0])], axis=1)


def _stage_scratch(groups, rows, width):
    return pltpu.VMEM((width // V7X_LANES, groups * _pitch(rows), V7X_LANES), F32)


def _resident(block_shape, index_map):
    return pl.BlockSpec(block_shape, index_map, pipeline_mode=pl.Buffered(1))


def _mod_kernel(c_ref, w_ref, b_ref, o_ref):
    c = c_ref[...]
    c_act = c * jax.nn.sigmoid(c)
    o_ref[...] = jnp.dot(c_act, w_ref[...], preferred_element_type=F32,
                         precision=HIGHEST) + b_ref[...]


def _mod_call(c_pad, w_ada, b_ada):
    rows, d = c_pad.shape
    n = w_ada.shape[1]
    tn = 1536
    return pl.pallas_call(
        _mod_kernel,
        grid=(n // tn,),
        in_specs=[pl.BlockSpec((rows, d), lambda j: (0, 0)),
                  pl.BlockSpec((d, tn), lambda j: (0, j)),
                  pl.BlockSpec((1, tn), lambda j: (0, j))],
        out_specs=pl.BlockSpec((rows, tn), lambda j: (0, j)),
        out_shape=jax.ShapeDtypeStruct((rows, n), F32),
        compiler_params=_params(("arbitrary",)),
        name="mod",
    )(c_pad, w_ada, b_ada)


def _rms_modulate(x, g, shift, scale):
    ms = jnp.mean(x * x, axis=-1, keepdims=True)
    y = x * lax.rsqrt(ms + RMS_EPS) * g
    return y * (1.0 + scale) + shift


def _proj_kernel(x_ref, mod_ref, g_ref, win_ref, bgate_ref, fc_ref,
                 wre_ref, wim_ref, q_ref, bg_ref, ga_ref, gb_ref, re_s, im_s, *, d, d_conv, n2):
    shift = mod_ref[0, 0:1, :]
    scale = mod_ref[0, 1:2, :]
    h = _rms_modulate(x_ref[...], g_ref[...], shift, scale).astype(BF16)

    o1 = D_FOURIER
    o2 = o1 + d_conv
    o3 = o2 + d_conv
    o4 = o3 + d_conv
    uf = _dot(h, win_ref[:, 0:o1]).astype(BF16)
    gd = FOURIER_GROUP_DIM
    ws = [_dot(uf[:, g * gd:(g + 1) * gd], fc_ref[...]) for g in range(FOURIER_GROUPS)]
    w_re = jnp.concatenate([w[:, :gd] for w in ws], axis=1)
    w_im = jnp.concatenate([w[:, gd:] for w in ws], axis=1)
    n1_rows = w_re.shape[0] // n2
    for r in range(n1_rows):
        _stage_group(re_s, r, w_re[r * n2:(r + 1) * n2, :])
        _stage_group(im_s, r, w_im[r * n2:(r + 1) * n2, :])
    for j in range(n2):
        cols = slice(j * D_FOURIER, (j + 1) * D_FOURIER)
        wre_ref[0, :, cols] = _row_of_each_group(re_s, j, n1_rows, n2)
        wim_ref[0, :, cols] = _row_of_each_group(im_s, j, n1_rows, n2)
    v = _dot(h, win_ref[:, o1:o2])
    cg = _dot(h, win_ref[:, o3:o4])
    q_ref[...] = (cg * v).astype(BF16)
    bg_ref[...] = _dot(h, win_ref[:, o2:o3]).astype(BF16)
    ga_ref[...] = jax.nn.sigmoid(_dot(h, win_ref[:, o4:o4 + d]) + bgate_ref[:, 0:d]).astype(BF16)
    gb_ref[...] = jax.nn.sigmoid(
        _dot(h, win_ref[:, o4 + d:o4 + 2 * d]) + bgate_ref[:, d:2 * d]).astype(BF16)


def _proj_call(x2, mod3, g_mix, w_in, b_gate, fc, *, seq):
    t, d = x2.shape
    k_in = w_in.shape[1]
    d_conv = (k_in - D_FOURIER - 2 * d) // 3
    tm = ROW_TILE
    tpb = seq // tm
    n2 = seq // FFT_N1
    n1_rows = tm // n2
    bsz = t // seq
    row = lambda i: (i, 0)
    const = lambda i: (0, 0)
    out_sds = lambda n: jax.ShapeDtypeStruct((t, n), BF16)
    dft_in = jax.ShapeDtypeStruct((bsz, FFT_N1, n2 * D_FOURIER), F32)
    dft_blk = pl.BlockSpec((1, n1_rows, n2 * D_FOURIER), lambda i: (i // tpb, i % tpb, 0))
    return pl.pallas_call(
        functools.partial(_proj_kernel, d=d, d_conv=d_conv, n2=n2),
        grid=(t // tm,),
        in_specs=[pl.BlockSpec((tm, d), row),
                  pl.BlockSpec((1, N_MOD, d), lambda i: (i // tpb, 0, 0)),
                  pl.BlockSpec((1, d), const),
                  _resident((d, k_in), const),
                  pl.BlockSpec((1, 2 * d), const),
                  pl.BlockSpec(fc.shape, const)],
        out_specs=[dft_blk, dft_blk,
                   pl.BlockSpec((tm, d_conv), row), pl.BlockSpec((tm, d_conv), row),
                   pl.BlockSpec((tm, d), row), pl.BlockSpec((tm, d), row)],
        out_shape=[dft_in, dft_in, out_sds(d_conv), out_sds(d_conv), out_sds(d), out_sds(d)],
        scratch_shapes=[_stage_scratch(n1_rows, n2, D_FOURIER)] * 2,
        compiler_params=_params(("parallel",)),
        name="proj",
    )(x2, mod3, g_mix, w_in, b_gate, fc)


def _fft1_kernel(wre_ref, wim_ref, f1_ref, cos_ref, sin_ref, tr_ref, ti_ref, re_s, im_s,
                 *, n1, nt):
    reps = D_FOURIER // V7X_LANES
    half = nt // 2
    for h in range(2):
        hcols = slice(h * half * D_FOURIER, (h + 1) * half * D_FOURIER)
        w = jnp.concatenate([wre_ref[0, :, hcols], wim_ref[0, :, hcols]], axis=0).astype(BF16)
        t = _dot(f1_ref[...], w)
        for jj in range(half):
            j = h * half + jj
            cols = slice(jj * D_FOURIER, (jj + 1) * D_FOURIER)
            lanes = slice(j * V7X_LANES, (j + 1) * V7X_LANES)
            c = jnp.concatenate([cos_ref[:, lanes]] * reps, axis=1)
            s = jnp.concatenate([sin_ref[:, lanes]] * reps, axis=1)
            a = t[:n1, cols]
            b = t[n1:, cols]
            _stage_group(re_s, j, a * c + b * s)
            _stage_group(im_s, j, b * c - a * s)
    for k in range(n1):
        tr_ref[0, k] = _row_of_each_group(re_s, k, nt, n1).astype(BF16)
        ti_ref[0, k] = _row_of_each_group(im_s, k, nt, n1).astype(BF16)


def _fft1_call(wre3, wim3, f1, cos_t, sin_t):
    bsz, n1, cols = wre3.shape
    n2 = cols // D_FOURIER
    nt = FFT_N2_TILE
    tn = nt * D_FOURIER
    blk = lambda b, j: (b, 0, j)
    oblk = lambda b, j: (b, 0, j, 0)
    sds = jax.ShapeDtypeStruct((bsz, n1, n2, D_FOURIER), BF16)
    return pl.pallas_call(
        functools.partial(_fft1_kernel, n1=n1, nt=nt),
        grid=(bsz, n2 // nt),
        in_specs=[pl.BlockSpec((1, n1, tn), blk), pl.BlockSpec((1, n1, tn), blk),
                  pl.BlockSpec(f1.shape, lambda b, j: (0, 0)),
                  pl.BlockSpec((n1, nt * V7X_LANES), lambda b, j: (0, j)),
                  pl.BlockSpec((n1, nt * V7X_LANES), lambda b, j: (0, j))],
        out_specs=[pl.BlockSpec((1, n1, nt, D_FOURIER), oblk),
                   pl.BlockSpec((1, n1, nt, D_FOURIER), oblk)],
        out_shape=[sds, sds],
        scratch_shapes=[_stage_scratch(nt, n1, D_FOURIER)] * 2,
        compiler_params=_params(("parallel", "parallel")),
        name="fft1",
    )(wre3, wim3, f1, cos_t, sin_t)


def _fft2_kernel(tr_ref, ti_ref, f2_ref, o_ref, res_s, *, kb, n2):
    for k in range(kb):
        rhs = jnp.concatenate([tr_ref[0, k], ti_ref[0, k]], axis=0)
        _stage_group(res_s, k, _dot(f2_ref[...], rhs))
    for k2 in range(n2):
        o_ref[0, k2] = _row_of_each_group(res_s, k2, kb, n2).astype(BF16)


def _fft2_call(tr4, ti4, f2):
    bsz, n1, n2, dfo = tr4.shape
    kb = FFT2_K_TILE
    blk = lambda b, j: (b, j, 0, 0)
    return pl.pallas_call(
        functools.partial(_fft2_kernel, kb=kb, n2=n2),
        grid=(bsz, n1 // kb),
        in_specs=[pl.BlockSpec((1, kb, n2, dfo), blk), pl.BlockSpec((1, kb, n2, dfo), blk),
                  pl.BlockSpec(f2.shape, lambda b, j: (0, 0))],
        out_specs=pl.BlockSpec((1, n2, kb, dfo), lambda b, j: (b, 0, j, 0)),
        out_shape=jax.ShapeDtypeStruct((bsz, n2, n1, dfo), BF16),
        scratch_shapes=[_stage_scratch(kb, n2, dfo)],
        compiler_params=_params(("parallel", "parallel")),
        name="fft2",
    )(tr4, ti4, f2)


def _dft_tables(seq):
    n1 = FFT_N1
    n2 = seq // n1
    gd = FOURIER_GROUP_DIM
    total_scale = 1.0 / np.sqrt(float(seq) * gd)
    s_c = 2.0 ** -4
    s_1 = 2.0 ** -3
    s_2 = total_scale / (s_c * s_1)

    def cs(n):
        ang = 2.0 * np.pi * np.outer(np.arange(n), np.arange(n)) / n
        return np.cos(ang), np.sin(ang)

    cc, sc = cs(gd)
    fc = np.concatenate([cc, -sc], axis=1) * s_c
    c1, s1 = cs(n1)
    f1 = np.block([[c1, s1], [-s1, c1]]) * s_1
    c2, s2 = cs(n2)
    f2 = np.concatenate([c2, s2], axis=1) * s_2
    ang = 2.0 * np.pi * np.outer(np.arange(n1), np.arange(n2)) / seq
    cos_t = np.repeat(np.cos(ang), V7X_LANES, axis=1)
    sin_t = np.repeat(np.sin(ang), V7X_LANES, axis=1)
    as_bf16 = lambda m: jnp.asarray(m, F32).astype(BF16)
    return (as_bf16(fc), as_bf16(f1), as_bf16(f2),
            jnp.asarray(cos_t, F32), jnp.asarray(sin_t, F32))


def _mix_kernel(a_ref, q_ref, qprev_ref, qnext_ref, bg_ref, ga_ref, gb_ref, x_ref, mod_ref,
                wconv_ref, wf_ref, wco_ref, wo_ref, gmoe_ref, wrt_ref, br_ref,
                x1_ref, h2_ref, pt_ref, *, tm, tpb):
    i = pl.program_id(0)
    first = (i % tpb) == 0
    last = (i % tpb) == tpb - 1
    q = q_ref[...].astype(F32)
    nh = qprev_ref.shape[0]
    hp = jnp.where(first, 0.0, qprev_ref[...].astype(F32)[nh - 1:nh, :])
    hn = jnp.where(last, 0.0, qnext_ref[...].astype(F32)[0:1, :])
    rows = lax.broadcasted_iota(I32, (tm, 1), 0)
    q_prev = jnp.where(rows == 0, hp, pltpu.roll(q, 1, axis=0))
    q_next = jnp.where(rows == tm - 1, hn, pltpu.roll(q, tm - 1, axis=0))
    conv = q_prev * wconv_ref[0:1, :] + q * wconv_ref[1:2, :] + q_next * wconv_ref[2:3, :]
    y_b = _dot((bg_ref[...].astype(F32) * conv).astype(BF16), wco_ref[...])
    y_a = _dot(a_ref[...], wf_ref[...])
    z = ga_ref[...].astype(F32) * y_a + gb_ref[...].astype(F32) * y_b
    mix = _dot(z.astype(BF16), wo_ref[...])
    gate_m = mod_ref[0, 2:3, :]
    x1 = x_ref[...] + gate_m * mix
    x1_ref[...] = x1

    h2 = _rms_modulate(x1, gmoe_ref[...], mod_ref[0, 3:4, :], mod_ref[0, 4:5, :])
    h2_ref[...] = h2.astype(BF16)
    logits = lax.dot_general(wrt_ref[...], h2, (((1,), (1,)), ((), ())),
                             preferred_element_type=F32, precision=HIGHEST) + br_ref[...]
    m = jnp.max(logits, axis=0, keepdims=True)
    e = jnp.exp(logits - m)
    probs_t = e / jnp.sum(e, axis=0, keepdims=True)
    pt_ref[0] = probs_t


def _mix_call(a2, q, bg, ga, gb, x2, mod3, w_conv, wf, wco, wo, g_moe, wr_t, b_r, *, seq):
    t, d = x2.shape
    n_exp = wr_t.shape[0]
    tm = ROW_TILE
    tpb = seq // tm
    bsz = t // seq
    hb = V7X_SUBLANES_BF16
    row = lambda i: (i, 0)
    const = lambda i: (0, 0)
    return pl.pallas_call(
        functools.partial(_mix_kernel, tm=tm, tpb=tpb),
        grid=(t // tm,),
        in_specs=[pl.BlockSpec((tm, a2.shape[1]), row),
                  pl.BlockSpec((tm, q.shape[1]), row),
                  pl.BlockSpec((hb, q.shape[1]),
                               lambda i: (jnp.maximum(i * (tm // hb) - 1, 0), 0)),
                  pl.BlockSpec((hb, q.shape[1]),
                               lambda i: (jnp.minimum((i + 1) * (tm // hb), t // hb - 1), 0)),
                  pl.BlockSpec((tm, bg.shape[1]), row),
                  pl.BlockSpec((tm, d), row), pl.BlockSpec((tm, d), row),
                  pl.BlockSpec((tm, d), row),
                  pl.BlockSpec((1, N_MOD, d), lambda i: (i // tpb, 0, 0)),
                  pl.BlockSpec(w_conv.shape, const),
                  _resident(wf.shape, const), _resident(wco.shape, const),
                  _resident(wo.shape, const),
                  pl.BlockSpec((1, d), const),
                  pl.BlockSpec(wr_t.shape, const), pl.BlockSpec(b_r.shape, const)],
        out_specs=[pl.BlockSpec((tm, d), row), pl.BlockSpec((tm, d), row),
                   pl.BlockSpec((1, n_exp, tm), lambda i: (i // tpb, 0, i % tpb))],
        out_shape=[jax.ShapeDtypeStruct((t, d), F32), jax.ShapeDtypeStruct((t, d), BF16),
                   jax.ShapeDtypeStruct((bsz, n_exp, seq), F32)],
        compiler_params=_params(("parallel",)),
        name="mix",
    )(a2, q, q, q, bg, ga, gb, x2, mod3, w_conv, wf, wco, wo, g_moe, wr_t, b_r)


def _excl_cumsum(mask_f, upper, lower_strict):
    r, rows, lanes = mask_f.shape
    m2 = mask_f.reshape(r * rows, lanes).astype(BF16)
    incl = _dot(m2, upper)
    tot = jnp.broadcast_to(incl[:, lanes - 1:lanes], incl.shape).astype(BF16)
    tot3 = tot.reshape(r, rows, lanes)
    offs = [_dot(lower_strict, tot3[j]) for j in range(r)]
    off = jnp.stack(offs, axis=0)
    return incl.reshape(r, rows, lanes) - mask_f + off


def _select_kernel(p_ref, upper_ref, lower_ref, pos_ref, raw_ref, *, cap):
    p = p_ref[...]
    r = p.shape[0]

    def count(mask):
        c = jnp.sum(mask.astype(F32), axis=2, keepdims=True)
        return jnp.sum(c, axis=1, keepdims=True)

    prefix = jnp.zeros(p.shape, I32)
    for bit in range(29, -1, -1):
        cand = prefix | (1 << bit)
        keep = count(p >= pltpu.bitcast(cand, F32)) >= cap
        prefix = jnp.where(keep, cand, prefix)
    thr = pltpu.bitcast(prefix, F32)
    gt = p > thr
    eq = (p == thr).astype(F32)
    need = cap - count(gt)
    upper = upper_ref[...]
    lower = lower_ref[...]
    rank_eq = _excl_cumsum(eq, upper, lower)
    sel = jnp.where(gt, 1.0, jnp.where(rank_eq < need, eq, 0.0))
    raw = _excl_cumsum(sel, upper, lower)
    raw_i = raw.astype(I32)
    raw_ref[...] = raw_i
    pos_ref[...] = jnp.where(sel > 0.0, raw_i, -1)


def _select_call(p3, cap):
    r, rows, lanes = p3.shape
    upper = jnp.asarray(np.triu(np.ones((lanes, lanes))), BF16)
    lower = jnp.asarray(np.tril(np.ones((rows, rows)), -1), BF16)
    full = lambda shape: pl.BlockSpec(shape, lambda i: (0,) * len(shape))
    sds = jax.ShapeDtypeStruct(p3.shape, I32)
    return pl.pallas_call(
        functools.partial(_select_kernel, cap=cap),
        grid=(1,),
        in_specs=[full(p3.shape), full(upper.shape), full(lower.shape)],
        out_specs=[full(p3.shape), full(p3.shape)],
        out_shape=[sds, sds],
        compiler_params=_params(("arbitrary",)),
        name="select",
    )(p3, upper, lower)


def _moe_kernel(pfx_ref, h2_ref, pos_ref, prob_ref, wg_ref, wu_ref, wd_ref, ye_ref,
                xf_ref, val_ref, xe_ref, acc_ref, *, n_exp, nb, cap):
    b = pl.program_id(0)
    e = pl.program_id(1)
    f = pl.program_id(2)
    tb = TOKEN_BLOCK
    ch = GATHER_CHUNK
    halves = tb // V7X_LANES
    sub = V7X_SUBLANES_F32

    @pl.when(f == 0)
    def _gather():
        xf_ref[...] = jnp.zeros_like(xf_ref)
        val_ref[...] = jnp.zeros_like(val_ref)
        base = (b * n_exp + e) * (nb + 1)
        chunk_rows = lax.broadcasted_iota(I32, (ch, V7X_LANES), 0)

        def add_chunk(k, ws):
            ws = pl.multiple_of(ws, sub)
            tok0 = pl.multiple_of(k * tb, tb)
            rows = chunk_rows + ws
            hits = [jnp.where(pos_ref[0, pl.ds(halves * k + j, 1), :] == rows, 1.0, 0.0)
                    for j in range(halves)]
            onehot = jnp.concatenate([h.astype(BF16) for h in hits], axis=1)
            xf_ref[pl.ds(ws, ch), :] += _dot(onehot, h2_ref[0, pl.ds(tok0, tb), :])
            v = hits[0] * prob_ref[0, pl.ds(halves * k, 1), :]
            for j in range(1, halves):
                v = v + hits[j] * prob_ref[0, pl.ds(halves * k + j, 1), :]
            val_ref[pl.ds(ws, ch), :] += jnp.sum(v, axis=1, keepdims=True)

        def group(g, carry):
            more = []
            for j in range(GATHER_UNROLL):
                k = g * GATHER_UNROLL + j
                p0 = pfx_ref[base + k]
                cnt = pfx_ref[base + k + 1] - p0
                start = (p0 // sub) * sub
                add_chunk(k, start)
                more.append((k, start, (p0 - start + cnt + ch - 1) // ch))
            for k, start, n_chunks in more:
                @pl.when(n_chunks > 1)
                def _():
                    def chunk(c, carry2):
                        add_chunk(k, start + c * ch)
                        return carry2
                    lax.fori_loop(1, n_chunks, chunk, 0)
            return carry

        lax.fori_loop(0, nb // GATHER_UNROLL, group, 0)
        xe_ref[...] = xf_ref[0:cap, :].astype(BF16)
        acc_ref[...] = jnp.zeros_like(acc_ref)

    xe = xe_ref[...]
    a = _dot(xe, wg_ref[0].astype(BF16))
    u = _dot(xe, wu_ref[0].astype(BF16))
    hm = (a * jax.nn.sigmoid(a) * u).astype(BF16)
    acc_ref[...] += _dot(hm, wd_ref[0].astype(BF16))

    @pl.when(f == pl.num_programs(2) - 1)
    def _():
        ye_ref[0, 0] = (acc_ref[...] * val_ref[0:cap, :]).astype(BF16)


def _moe_call(pfx, h2_3, pos3, prob3, w_gate, w_up, w_down, *, cap):
    bsz, seq, d = h2_3.shape
    n_exp, _, d_exp = w_gate.shape
    nb = seq // TOKEN_BLOCK
    fc = FFN_CHUNK
    grid_spec = pltpu.PrefetchScalarGridSpec(
        num_scalar_prefetch=1,
        grid=(bsz, n_exp, d_exp // fc),
        in_specs=[_resident((1, seq, d), lambda b, e, f, p: (b, 0, 0)),
                  pl.BlockSpec((1,) + pos3.shape[1:], lambda b, e, f, p: (b * n_exp + e, 0, 0)),
                  pl.BlockSpec((1,) + prob3.shape[1:], lambda b, e, f, p: (b * n_exp + e, 0, 0)),
                  pl.BlockSpec((1, d, fc), lambda b, e, f, p: (e, 0, f)),
                  pl.BlockSpec((1, d, fc), lambda b, e, f, p: (e, 0, f)),
                  pl.BlockSpec((1, fc, d), lambda b, e, f, p: (e, f, 0))],
        out_specs=pl.BlockSpec((1, 1, cap, d), lambda b, e, f, p: (b, e, 0, 0)),
        scratch_shapes=[pltpu.VMEM((cap + GATHER_CHUNK, d), F32),
                        pltpu.VMEM((cap + GATHER_CHUNK, 1), F32),
                        pltpu.VMEM((cap, d), BF16),
                        pltpu.VMEM((cap, d), F32)],
    )
    return pl.pallas_call(
        functools.partial(_moe_kernel, n_exp=n_exp, nb=nb, cap=cap),
        grid_spec=grid_spec,
        out_shape=jax.ShapeDtypeStruct((bsz, n_exp, cap, d), BF16),
        compiler_params=_params(("arbitrary", "arbitrary", "arbitrary")),
        name="moe",
    )(pfx, h2_3, pos3, prob3, w_gate, w_up, w_down)


def _combine_kernel(pfx_ref, x1_ref, ye_ref, post_ref, mod_ref, gfin_ref, o_ref,
                    ycat_ref, scat_ref, acc_ref, *, n_exp, nb, cap):
    b = pl.program_id(0)
    k = pl.program_id(1)
    tb = TOKEN_BLOCK
    win = COMBINE_WINDOW
    align = V7X_SUBLANES_BF16
    p0s, ends, wss = [], [], []
    fits = None
    for e in range(n_exp):
        base = (b * n_exp + e) * (nb + 1) + k
        p0 = pfx_ref[base]
        end = pfx_ref[base + 1]
        ws = pl.multiple_of(jnp.minimum((p0 // align) * align, cap - win), align)
        ok = end <= ws + win
        fits = ok if fits is None else jnp.logical_and(fits, ok)
        p0s.append(p0)
        ends.append(end)
        wss.append(ws)

    @pl.when(fits)
    def _():
        cols = lax.broadcasted_iota(I32, (tb, win), 1)
        for e in range(n_exp):
            ycat_ref[e * win:(e + 1) * win, :] = ye_ref[0, e, pl.ds(wss[e], win), :]
            hit = post_ref[:, e:e + 1] == cols + wss[e]
            scat_ref[:, e * win:(e + 1) * win] = jnp.where(hit, 1.0, 0.0).astype(BF16)
        acc_ref[...] = _dot(scat_ref[...], ycat_ref[...])

    @pl.when(jnp.logical_not(fits))
    def _():
        acc_ref[...] = jnp.zeros_like(acc_ref)
        cols = lax.broadcasted_iota(I32, (tb, tb), 1)
        for e in range(n_exp):
            ws1 = pl.multiple_of(jnp.minimum((p0s[e] // align) * align, cap - tb), align)
            ws2 = pl.multiple_of(jnp.minimum(ws1 + tb, cap - tb), align)
            pos_col = post_ref[:, e:e + 1]

            @pl.when(ends[e] > p0s[e])
            def _():
                onehot = jnp.where(pos_col == cols + ws1, 1.0, 0.0).astype(BF16)
                acc_ref[...] += _dot(onehot, ye_ref[0, e, pl.ds(ws1, tb), :])

            @pl.when(ends[e] > ws1 + tb)
            def _():
                c2 = cols + ws2
                hit = jnp.logical_and(pos_col == c2, c2 >= ws1 + tb)
                onehot = jnp.where(hit, 1.0, 0.0).astype(BF16)
                acc_ref[...] += _dot(onehot, ye_ref[0, e, pl.ds(ws2, tb), :])

    gate_f = mod_ref[0, 5:6, :]
    x2 = x1_ref[...] + gate_f * acc_ref[...]
    ms = jnp.mean(x2 * x2, axis=-1, keepdims=True)
    o_ref[...] = x2 * lax.rsqrt(ms + RMS_EPS) * gfin_ref[...]


def _combine_call(pfx, x1, ye, pos_t, mod3, g_final, *, seq):
    t, d = x1.shape
    bsz, n_exp, cap, _ = ye.shape
    tb = TOKEN_BLOCK
    nb = seq // tb
    row = lambda b, k, p: (b * nb + k, 0)
    grid_spec = pltpu.PrefetchScalarGridSpec(
        num_scalar_prefetch=1,
        grid=(bsz, nb),
        in_specs=[pl.BlockSpec((tb, d), row),
                  _resident((1, n_exp, cap, d), lambda b, k, p: (b, 0, 0, 0)),
                  pl.BlockSpec((tb, n_exp), row),
                  pl.BlockSpec((1, N_MOD, d), lambda b, k, p: (b, 0, 0)),
                  pl.BlockSpec((1, d), lambda b, k, p: (0, 0))],
        out_specs=pl.BlockSpec((tb, d), row),
        scratch_shapes=[pltpu.VMEM((n_exp * COMBINE_WINDOW, d), BF16),
                        pltpu.VMEM((tb, n_exp * COMBINE_WINDOW), BF16),
                        pltpu.VMEM((tb, d), F32)],
    )
    return pl.pallas_call(
        functools.partial(_combine_kernel, n_exp=n_exp, nb=nb, cap=cap),
        grid_spec=grid_spec,
        out_shape=jax.ShapeDtypeStruct((t, d), F32),
        compiler_params=_params(("arbitrary", "arbitrary")),
        name="combine",
    )(pfx, x1, ye, pos_t, mod3, g_final)


def _layer(x2, c_act_in, w_ada, b_ada, g_norm_mix, w_in, b_gate, w_fourier, w_conv, w_conv_out,
           w_o, g_norm_moe, w_router, b_router, w_gate_e, w_up_e, w_down_e, *, bsz, seq):
    t, d = x2.shape
    n_exp = w_router.shape[1]
    cap = EC_CAPACITY * seq // n_exp
    n1 = FFT_N1
    n2 = seq // n1
    fc, f1, f2, cos_t, sin_t = _dft_tables(seq)

    mod = _mod_call(c_act_in, w_ada, b_ada.reshape(1, -1))[:bsz]
    mod3 = mod.reshape(bsz, N_MOD, d)

    wre, wim, q, bg, ga, gb = _proj_call(
        x2, mod3, g_norm_mix.reshape(1, d), w_in.astype(BF16), b_gate.reshape(1, -1), fc, seq=seq)

    tr, ti = _fft1_call(wre, wim, f1, cos_t, sin_t)
    a2 = _fft2_call(tr, ti, f2).reshape(t, D_FOURIER)

    x1, h2, probs_t = _mix_call(
        a2, q, bg, ga, gb, x2, mod3, w_conv, w_fourier.astype(BF16), w_conv_out.astype(BF16),
        w_o.astype(BF16), g_norm_moe.reshape(1, d), w_router.T, b_router.reshape(n_exp, 1),
        seq=seq)

    prob3 = probs_t.reshape(bsz * n_exp, seq // V7X_LANES, V7X_LANES)
    pos, raw = _select_call(prob3, cap)
    stride = TOKEN_BLOCK // V7X_LANES
    starts = raw[:, ::stride, 0]
    pfx = jnp.concatenate([starts, jnp.full((bsz * n_exp, 1), cap, I32)], axis=1).reshape(-1)
    pos_t = jnp.transpose(pos.reshape(bsz, n_exp, seq), (0, 2, 1)).reshape(t, n_exp)

    ye = _moe_call(pfx, h2.reshape(bsz, seq, d), pos, prob3, w_gate_e, w_up_e, w_down_e, cap=cap)
    return ye, pfx, x1, pos_t, mod3


def kernel(x, c, w_ada, b_ada, g_norm_mix, w_in, b_gate, w_fourier, w_conv, w_conv_out, w_o,
           g_norm_moe, w_router, b_router, w_gate_e, w_up_e, w_down_e, g_final):
    bsz, seq, d = x.shape
    assert w_ada.shape[0] == 1
    assert seq % (FFT_N1 * V7X_SUBLANES_F32) == 0 and seq % ROW_TILE == 0
    assert (EC_CAPACITY * seq // w_router.shape[2]) >= TOKEN_BLOCK
    c_pad = jnp.pad(c, ((0, V7X_SUBLANES_F32 - bsz), (0, 0)))
    x2 = x.reshape(bsz * seq, d)
    ye, pfx, x1, pos_t, mod3 = _layer(
        x2, c_pad, w_ada[0], b_ada[0], g_norm_mix[0], w_in[0], b_gate[0], w_fourier[0],
        w_conv[0], w_conv_out[0], w_o[0], g_norm_moe[0], w_router[0], b_router[0],
        w_gate_e[0], w_up_e[0], w_down_e[0], bsz=bsz, seq=seq)
    out = _combine_call(pfx, x1, ye, pos_t, mod3, g_final.reshape(1, d), seq=seq)
    return out.reshape(bsz, seq, d)
```

```python
import functools

import numpy as np
import jax
import jax.numpy as jnp
from jax import lax
from jax.experimental import pallas as pl
from jax.experimental.pallas import tpu as pltpu

F32 = jnp.float32
BF16 = jnp.bfloat16
I32 = jnp.int32
HIGHEST = lax.Precision.HIGHEST

FOURIER_GROUPS = 4
FOURIER_GROUP_DIM = 128
D_FOURIER = FOURIER_GROUPS * FOURIER_GROUP_DIM
N_MOD = 6
EC_CAPACITY = 2
RMS_EPS = 1e-6

V7X_LANES = 128
V7X_SUBLANES_F32 = 8
V7X_SUBLANES_BF16 = 16
V7X_VMEM_BYTES = 64 * 1024 * 1024
VMEM_LIMIT_BYTES = V7X_VMEM_BYTES - 6 * 1024 * 1024

ROW_TILE = 512
FFT_N1 = 128
FFT_N2_TILE = 16
FFT2_K_TILE = 16
TOKEN_BLOCK = 256
GATHER_CHUNK = 64
GATHER_UNROLL = 8
COMBINE_WINDOW = 64
FFN_CHUNK = 512


def _dot(a, b):
    return jnp.dot(a, b, preferred_element_type=F32)


def _params(semantics):
    return pltpu.CompilerParams(dimension_semantics=semantics,
                                vmem_limit_bytes=VMEM_LIMIT_BYTES)


def _pitch(rows):
    p = -(-rows // V7X_SUBLANES_F32)
    return (p if p % 2 else p + 1) * V7X_SUBLANES_F32


def _stage_group(stage_ref, group, value):
    rows = value.shape[0]
    row0 = group * _pitch(rows)
    for l in range(stage_ref.shape[0]):
        stage_ref[l, row0:row0 + rows, :] = value[:, l * V7X_LANES:(l + 1) * V7X_LANES]


def _row_of_each_group(stage_ref, row, groups, rows):
    return jnp.concatenate([stage_ref[l, pl.ds(row, groups, stride=_pitch(rows)), :]
                            for l in range(stage_ref.shape[0])], axis=1)


def _stage_scratch(groups, rows, width):
    return pltpu.VMEM((width // V7X_LANES, groups * _pitch(rows), V7X_LANES), F32)


def _resident(block_shape, index_map):
    return pl.BlockSpec(block_shape, index_map, pipeline_mode=pl.Buffered(1))


def _mod_kernel(c_ref, w_ref, b_ref, o_ref):
    c = c_ref[...]
    c_act = c * jax.nn.sigmoid(c)
    o_ref[...] = jnp.dot(c_act, w_ref[...], preferred_element_type=F32,
                         precision=HIGHEST) + b_ref[...]


def _mod_call(c_pad, w_ada, b_ada):
    rows, d = c_pad.shape
    n = w_ada.shape[1]
    tn = 1536
    return pl.pallas_call(
        _mod_kernel,
        grid=(n // tn,),
        in_specs=[pl.BlockSpec((rows, d), lambda j: (0, 0)),
                  pl.BlockSpec((d, tn), lambda j: (0, j)),
                  pl.BlockSpec((1, tn), lambda j: (0, j))],
        out_specs=pl.BlockSpec((rows, tn), lambda j: (0, j)),
        out_shape=jax.ShapeDtypeStruct((rows, n), F32),
        compiler_params=_params(("arbitrary",)),
        name="mod",
    )(c_pad, w_ada, b_ada)


def _rms_modulate(x, g, shift, scale):
    ms = jnp.mean(x * x, axis=-1, keepdims=True)
    y = x * lax.rsqrt(ms + RMS_EPS) * g
    return y * (1.0 + scale) + shift


def _proj_kernel(x_ref, xprev_ref, xnext_ref, mod_ref, g_ref, win_ref, bgate_ref, fc_ref, wconv_ref,
                 wre_ref, wim_ref, cv_ref, ga_ref, gb_ref, re_s, im_s,
                 *, d, d_conv, n2, tm, tpb):
    i = pl.program_id(0)
    shift = mod_ref[0, 0:1, :]
    scale = mod_ref[0, 1:2, :]
    h = _rms_modulate(x_ref[...], g_ref[...], shift, scale).astype(BF16)
    x_halo = jnp.concatenate([xprev_ref[...], xnext_ref[...]], axis=0)
    h_halo = _rms_modulate(x_halo, g_ref[...], shift, scale).astype(BF16)
    h_ext = jnp.concatenate([h, h_halo], axis=0)

    o1 = D_FOURIER
    o2 = o1 + d_conv
    o3 = o2 + d_conv
    o4 = o3 + d_conv
    uf = _dot(h, win_ref[:, 0:o1]).astype(BF16)
    gd = FOURIER_GROUP_DIM
    ws = [_dot(uf[:, g * gd:(g + 1) * gd], fc_ref[...]) for g in range(FOURIER_GROUPS)]
    w_re = jnp.concatenate([w[:, :gd] for w in ws], axis=1)
    w_im = jnp.concatenate([w[:, gd:] for w in ws], axis=1)
    n1_rows = w_re.shape[0] // n2
    for r in range(n1_rows):
        _stage_group(re_s, r, w_re[r * n2:(r + 1) * n2, :])
        _stage_group(im_s, r, w_im[r * n2:(r + 1) * n2, :])
    for j in range(n2):
        cols = slice(j * D_FOURIER, (j + 1) * D_FOURIER)
        wre_ref[0, :, cols] = _row_of_each_group(re_s, j, n1_rows, n2)
        wim_ref[0, :, cols] = _row_of_each_group(im_s, j, n1_rows, n2)
    q_ext = _dot(h_ext, win_ref[:, o3:o4]) * _dot(h_ext, win_ref[:, o1:o2])
    q = q_ext[0:tm, :]
    nh = xprev_ref.shape[0]
    first = (i % tpb) == 0
    last = (i % tpb) == tpb - 1
    hp = jnp.where(first, 0.0, q_ext[tm + nh - 1:tm + nh, :])
    hn = jnp.where(last, 0.0, q_ext[tm + nh:tm + nh + 1, :])
    rows = lax.broadcasted_iota(I32, (tm, 1), 0)
    q_prev = jnp.where(rows == 0, hp, pltpu.roll(q, 1, axis=0))
    q_next = jnp.where(rows == tm - 1, hn, pltpu.roll(q, tm - 1, axis=0))
    conv = q_prev * wconv_ref[0:1, :] + q * wconv_ref[1:2, :] + q_next * wconv_ref[2:3, :]
    cv_ref[...] = (_dot(h, win_ref[:, o2:o3]) * conv).astype(BF16)
    ga_ref[...] = jax.nn.sigmoid(_dot(h, win_ref[:, o4:o4 + d]) + bgate_ref[:, 0:d]).astype(BF16)
    gb_ref[...] = jax.nn.sigmoid(
        _dot(h, win_ref[:, o4 + d:o4 + 2 * d]) + bgate_ref[:, d:2 * d]).astype(BF16)


def _proj_call(x2, mod3, g_mix, w_in, b_gate, fc, w_conv, *, seq):
    t, d = x2.shape
    k_in = w_in.shape[1]
    d_conv = (k_in - D_FOURIER - 2 * d) // 3
    tm = ROW_TILE
    tpb = seq // tm
    n2 = seq // FFT_N1
    n1_rows = tm // n2
    bsz = t // seq
    sub = V7X_SUBLANES_F32
    row = lambda i: (i, 0)
    const = lambda i: (0, 0)
    out_sds = lambda n: jax.ShapeDtypeStruct((t, n), BF16)
    dft_in = jax.ShapeDtypeStruct((bsz, FFT_N1, n2 * D_FOURIER), F32)
    dft_blk = pl.BlockSpec((1, n1_rows, n2 * D_FOURIER), lambda i: (i // tpb, i % tpb, 0))
    return pl.pallas_call(
        functools.partial(_proj_kernel, d=d, d_conv=d_conv, n2=n2, tm=tm, tpb=tpb),
        grid=(t // tm,),
        in_specs=[pl.BlockSpec((tm, d), row),
                  pl.BlockSpec((sub, d), lambda i: (jnp.maximum(i * (tm // sub) - 1, 0), 0)),
                  pl.BlockSpec((sub, d),
                               lambda i: (jnp.minimum((i + 1) * (tm // sub), t // sub - 1), 0)),
                  pl.BlockSpec((1, N_MOD, d), lambda i: (i // tpb, 0, 0)),
                  pl.BlockSpec((1, d), const),
                  _resident((d, k_in), const),
                  pl.BlockSpec((1, 2 * d), const),
                  pl.BlockSpec(fc.shape, const),
                  pl.BlockSpec(w_conv.shape, const)],
        out_specs=[dft_blk, dft_blk, pl.BlockSpec((tm, d_conv), row),
                   pl.BlockSpec((tm, d), row), pl.BlockSpec((tm, d), row)],
        out_shape=[dft_in, dft_in, out_sds(d_conv), out_sds(d), out_sds(d)],
        scratch_shapes=[_stage_scratch(n1_rows, n2, D_FOURIER)] * 2,
        compiler_params=_params(("parallel",)),
        name="proj",
    )(x2, x2, x2, mod3, g_mix, w_in, b_gate, fc, w_conv)


def _fft1_kernel(wre_ref, wim_ref, f1_ref, cos_ref, sin_ref, tr_ref, ti_ref, re_s, im_s,
                 *, n1, nt):
    reps = D_FOURIER // V7X_LANES
    half = nt // 2
    for h in range(2):
        hcols = slice(h * half * D_FOURIER, (h + 1) * half * D_FOURIER)
        w = jnp.concatenate([wre_ref[0, :, hcols], wim_ref[0, :, hcols]], axis=0).astype(BF16)
        t = _dot(f1_ref[...], w)
        for jj in range(half):
            j = h * half + jj
            cols = slice(jj * D_FOURIER, (jj + 1) * D_FOURIER)
            lanes = slice(j * V7X_LANES, (j + 1) * V7X_LANES)
            c = jnp.concatenate([cos_ref[:, lanes]] * reps, axis=1)
            s = jnp.concatenate([sin_ref[:, lanes]] * reps, axis=1)
            a = t[:n1, cols]
            b = t[n1:, cols]
            _stage_group(re_s, j, a * c + b * s)
            _stage_group(im_s, j, b * c - a * s)
    for k in range(n1):
        tr_ref[0, k] = _row_of_each_group(re_s, k, nt, n1).astype(BF16)
        ti_ref[0, k] = _row_of_each_group(im_s, k, nt, n1).astype(BF16)


def _fft1_call(wre3, wim3, f1, cos_t, sin_t):
    bsz, n1, cols = wre3.shape
    n2 = cols // D_FOURIER
    nt = FFT_N2_TILE
    tn = nt * D_FOURIER
    blk = lambda b, j: (b, 0, j)
    oblk = lambda b, j: (b, 0, j, 0)
    sds = jax.ShapeDtypeStruct((bsz, n1, n2, D_FOURIER), BF16)
    return pl.pallas_call(
        functools.partial(_fft1_kernel, n1=n1, nt=nt),
        grid=(bsz, n2 // nt),
        in_specs=[pl.BlockSpec((1, n1, tn), blk), pl.BlockSpec((1, n1, tn), blk),
                  pl.BlockSpec(f1.shape, lambda b, j: (0, 0)),
                  pl.BlockSpec((n1, nt * V7X_LANES), lambda b, j: (0, j)),
                  pl.BlockSpec((n1, nt * V7X_LANES), lambda b, j: (0, j))],
        out_specs=[pl.BlockSpec((1, n1, nt, D_FOURIER), oblk),
                   pl.BlockSpec((1, n1, nt, D_FOURIER), oblk)],
        out_shape=[sds, sds],
        scratch_shapes=[_stage_scratch(nt, n1, D_FOURIER)] * 2,
        compiler_params=_params(("parallel", "parallel")),
        name="fft1",
    )(wre3, wim3, f1, cos_t, sin_t)


def _fft2_kernel(tr_ref, ti_ref, f2_ref, o_ref, res_s, *, kb, n2):
    for k in range(kb):
        rhs = jnp.concatenate([tr_ref[0, k], ti_ref[0, k]], axis=0)
        _stage_group(res_s, k, _dot(f2_ref[...], rhs))
    for k2 in range(n2):
        o_ref[0, k2] = _row_of_each_group(res_s, k2, kb, n2).astype(BF16)


def _fft2_call(tr4, ti4, f2):
    bsz, n1, n2, dfo = tr4.shape
    kb = FFT2_K_TILE
    blk = lambda b, j: (b, j, 0, 0)
    return pl.pallas_call(
        functools.partial(_fft2_kernel, kb=kb, n2=n2),
        grid=(bsz, n1 // kb),
        in_specs=[pl.BlockSpec((1, kb, n2, dfo), blk), pl.BlockSpec((1, kb, n2, dfo), blk),
                  pl.BlockSpec(f2.shape, lambda b, j: (0, 0))],
        out_specs=pl.BlockSpec((1, n2, kb, dfo), lambda b, j: (b, 0, j, 0)),
        out_shape=jax.ShapeDtypeStruct((bsz, n2, n1, dfo), BF16),
        scratch_shapes=[_stage_scratch(kb, n2, dfo)],
        compiler_params=_params(("parallel", "parallel")),
        name="fft2",
    )(tr4, ti4, f2)


def _dft_tables(seq):
    n1 = FFT_N1
    n2 = seq // n1
    gd = FOURIER_GROUP_DIM
    total_scale = 1.0 / np.sqrt(float(seq) * gd)
    s_c = 2.0 ** -4
    s_1 = 2.0 ** -3
    s_2 = total_scale / (s_c * s_1)

    def cs(n):
        ang = 2.0 * np.pi * np.outer(np.arange(n), np.arange(n)) / n
        return np.cos(ang), np.sin(ang)

    cc, sc = cs(gd)
    fc = np.concatenate([cc, -sc], axis=1) * s_c
    c1, s1 = cs(n1)
    f1 = np.block([[c1, s1], [-s1, c1]]) * s_1
    c2, s2 = cs(n2)
    f2 = np.concatenate([c2, s2], axis=1) * s_2
    ang = 2.0 * np.pi * np.outer(np.arange(n1), np.arange(n2)) / seq
    cos_t = np.repeat(np.cos(ang), V7X_LANES, axis=1)
    sin_t = np.repeat(np.sin(ang), V7X_LANES, axis=1)
    as_bf16 = lambda m: jnp.asarray(m, F32).astype(BF16)
    return (as_bf16(fc), as_bf16(f1), as_bf16(f2),
            jnp.asarray(cos_t, F32), jnp.asarray(sin_t, F32))


def _mix_kernel(a_ref, cv_ref, ga_ref, gb_ref, x_ref, mod_ref,
                wf_ref, wco_ref, wo_ref, gmoe_ref, wrt_ref, br_ref,
                x1_ref, h2_ref, pt_ref):
    y_b = _dot(cv_ref[...], wco_ref[...])
    y_a = _dot(a_ref[...], wf_ref[...])
    z = ga_ref[...].astype(F32) * y_a + gb_ref[...].astype(F32) * y_b
    mix = _dot(z.astype(BF16), wo_ref[...])
    gate_m = mod_ref[0, 2:3, :]
    x1 = x_ref[...] + gate_m * mix
    x1_ref[...] = x1

    h2 = _rms_modulate(x1, gmoe_ref[...], mod_ref[0, 3:4, :], mod_ref[0, 4:5, :])
    h2_ref[...] = h2.astype(BF16)
    logits = lax.dot_general(wrt_ref[...], h2, (((1,), (1,)), ((), ())),
                             preferred_element_type=F32, precision=HIGHEST) + br_ref[...]
    m = jnp.max(logits, axis=0, keepdims=True)
    e = jnp.exp(logits - m)
    probs_t = e / jnp.sum(e, axis=0, keepdims=True)
    pt_ref[0] = probs_t


def _mix_call(a2, cv, ga, gb, x2, mod3, wf, wco, wo, g_moe, wr_t, b_r, *, seq):
    t, d = x2.shape
    n_exp = wr_t.shape[0]
    tm = ROW_TILE
    tpb = seq // tm
    bsz = t // seq
    row = lambda i: (i, 0)
    const = lambda i: (0, 0)
    return pl.pallas_call(
        _mix_kernel,
        grid=(t // tm,),
        in_specs=[pl.BlockSpec((tm, a2.shape[1]), row),
                  pl.BlockSpec((tm, cv.shape[1]), row),
                  pl.BlockSpec((tm, d), row), pl.BlockSpec((tm, d), row),
                  pl.BlockSpec((tm, d), row),
                  pl.BlockSpec((1, N_MOD, d), lambda i: (i // tpb, 0, 0)),
                  _resident(wf.shape, const), _resident(wco.shape, const),
                  _resident(wo.shape, const),
                  pl.BlockSpec((1, d), const),
                  pl.BlockSpec(wr_t.shape, const), pl.BlockSpec(b_r.shape, const)],
        out_specs=[pl.BlockSpec((tm, d), row), pl.BlockSpec((tm, d), row),
                   pl.BlockSpec((1, n_exp, tm), lambda i: (i // tpb, 0, i % tpb))],
        out_shape=[jax.ShapeDtypeStruct((t, d), F32), jax.ShapeDtypeStruct((t, d), BF16),
                   jax.ShapeDtypeStruct((bsz, n_exp, seq), F32)],
        compiler_params=_params(("parallel",)),
        name="mix",
    )(a2, cv, ga, gb, x2, mod3, wf, wco, wo, g_moe, wr_t, b_r)


def _excl_cumsum(mask_f, upper, lower_strict):
    r, rows, lanes = mask_f.shape
    m2 = mask_f.reshape(r * rows, lanes).astype(BF16)
    incl = _dot(m2, upper)
    tot = jnp.broadcast_to(incl[:, lanes - 1:lanes], incl.shape).astype(BF16)
    tot3 = tot.reshape(r, rows, lanes)
    offs = [_dot(lower_strict, tot3[j]) for j in range(r)]
    off = jnp.stack(offs, axis=0)
    return incl.reshape(r, rows, lanes) - mask_f + off


def _select_kernel(p_ref, upper_ref, lower_ref, pos_ref, raw_ref, *, cap):
    p = p_ref[...]
    r = p.shape[0]

    def count(mask):
        c = jnp.sum(mask.astype(F32), axis=2, keepdims=True)
        return jnp.sum(c, axis=1, keepdims=True)

    prefix = jnp.zeros(p.shape, I32)
    for bit in range(29, -1, -1):
        cand = prefix | (1 << bit)
        keep = count(p >= pltpu.bitcast(cand, F32)) >= cap
        prefix = jnp.where(keep, cand, prefix)
    thr = pltpu.bitcast(prefix, F32)
    gt = p > thr
    eq = (p == thr).astype(F32)
    need = cap - count(gt)
    upper = upper_ref[...]
    lower = lower_ref[...]
    rank_eq = _excl_cumsum(eq, upper, lower)
    sel = jnp.where(gt, 1.0, jnp.where(rank_eq < need, eq, 0.0))
    raw = _excl_cumsum(sel, upper, lower)
    raw_i = raw.astype(I32)
    raw_ref[...] = raw_i
    pos_ref[...] = jnp.where(sel > 0.0, raw_i, -1)


def _select_call(p3, cap):
    r, rows, lanes = p3.shape
    upper = jnp.asarray(np.triu(np.ones((lanes, lanes))), BF16)
    lower = jnp.asarray(np.tril(np.ones((rows, rows)), -1), BF16)
    full = lambda shape: pl.BlockSpec(shape, lambda i: (0,) * len(shape))
    sds = jax.ShapeDtypeStruct(p3.shape, I32)
    return pl.pallas_call(
        functools.partial(_select_kernel, cap=cap),
        grid=(1,),
        in_specs=[full(p3.shape), full(upper.shape), full(lower.shape)],
        out_specs=[full(p3.shape), full(p3.shape)],
        out_shape=[sds, sds],
        compiler_params=_params(("arbitrary",)),
        name="select",
    )(p3, upper, lower)


def _moe_kernel(pfx_ref, h2_ref, pos_ref, prob_ref, wg_ref, wu_ref, wd_ref, ye_ref,
                val_ref, xe_ref, acc_ref, *, n_exp, nb, cap):
    b = pl.program_id(0)
    e = pl.program_id(1)
    f = pl.program_id(2)
    tb = TOKEN_BLOCK
    ch = GATHER_CHUNK
    halves = tb // V7X_LANES
    sub = V7X_SUBLANES_BF16

    @pl.when(f == 0)
    def _gather():
        xe_ref[...] = jnp.zeros_like(xe_ref)
        val_ref[...] = jnp.zeros_like(val_ref)
        base = (b * n_exp + e) * (nb + 1)
        chunk_rows = lax.broadcasted_iota(I32, (ch, V7X_LANES), 0)

        def add_chunk(k, ws):
            ws = pl.multiple_of(ws, sub)
            tok0 = pl.multiple_of(k * tb, tb)
            rows = chunk_rows + ws
            hits = [jnp.where(pos_ref[0, pl.ds(halves * k + j, 1), :] == rows, 1.0, 0.0)
                    for j in range(halves)]
            onehot = jnp.concatenate([h.astype(BF16) for h in hits], axis=1)
            xe_ref[pl.ds(ws, ch), :] += _dot(onehot, h2_ref[0, pl.ds(tok0, tb), :]).astype(BF16)
            v = hits[0] * prob_ref[0, pl.ds(halves * k, 1), :]
            for j in range(1, halves):
                v = v + hits[j] * prob_ref[0, pl.ds(halves * k + j, 1), :]
            val_ref[pl.ds(ws, ch), :] += jnp.sum(v, axis=1, keepdims=True)

        def group(g, carry):
            more = []
            for j in range(GATHER_UNROLL):
                k = g * GATHER_UNROLL + j
                p0 = pfx_ref[base + k]
                cnt = pfx_ref[base + k + 1] - p0
                start = (p0 // sub) * sub
                add_chunk(k, start)
                more.append((k, start, (p0 - start + cnt + ch - 1) // ch))
            for k, start, n_chunks in more:
                @pl.when(n_chunks > 1)
                def _():
                    def chunk(c, carry2):
                        add_chunk(k, start + c * ch)
                        return carry2
                    lax.fori_loop(1, n_chunks, chunk, 0)
            return carry

        lax.fori_loop(0, nb // GATHER_UNROLL, group, 0)
        acc_ref[...] = jnp.zeros_like(acc_ref)

    xe = xe_ref[0:cap, :]
    a = _dot(xe, wg_ref[0].astype(BF16))
    u = _dot(xe, wu_ref[0].astype(BF16))
    hm = (a * jax.nn.sigmoid(a) * u).astype(BF16)
    acc_ref[...] += _dot(hm, wd_ref[0].astype(BF16))

    @pl.when(f == pl.num_programs(2) - 1)
    def _():
        ye_ref[0, 0] = (acc_ref[...] * val_ref[0:cap, :]).astype(BF16)


def _moe_call(pfx, h2_3, pos3, prob3, w_gate, w_up, w_down, *, cap):
    bsz, seq, d = h2_3.shape
    n_exp, _, d_exp = w_gate.shape
    nb = seq // TOKEN_BLOCK
    fc = FFN_CHUNK
    grid_spec = pltpu.PrefetchScalarGridSpec(
        num_scalar_prefetch=1,
        grid=(bsz, n_exp, d_exp // fc),
        in_specs=[_resident((1, seq, d), lambda b, e, f, p: (b, 0, 0)),
                  pl.BlockSpec((1,) + pos3.shape[1:], lambda b, e, f, p: (b * n_exp + e, 0, 0)),
                  pl.BlockSpec((1,) + prob3.shape[1:], lambda b, e, f, p: (b * n_exp + e, 0, 0)),
                  pl.BlockSpec((1, d, fc), lambda b, e, f, p: (e, 0, f)),
                  pl.BlockSpec((1, d, fc), lambda b, e, f, p: (e, 0, f)),
                  pl.BlockSpec((1, fc, d), lambda b, e, f, p: (e, f, 0))],
        out_specs=pl.BlockSpec((1, 1, cap, d), lambda b, e, f, p: (b, e, 0, 0)),
        scratch_shapes=[pltpu.VMEM((cap + GATHER_CHUNK, 1), F32),
                        pltpu.VMEM((cap + GATHER_CHUNK, d), BF16),
                        pltpu.VMEM((cap, d), F32)],
    )
    return pl.pallas_call(
        functools.partial(_moe_kernel, n_exp=n_exp, nb=nb, cap=cap),
        grid_spec=grid_spec,
        out_shape=jax.ShapeDtypeStruct((bsz, n_exp, cap, d), BF16),
        compiler_params=_params(("arbitrary", "arbitrary", "arbitrary")),
        name="moe",
    )(pfx, h2_3, pos3, prob3, w_gate, w_up, w_down)


def _combine_kernel(pfx_ref, x1_ref, ye_ref, post_ref, mod_ref, gfin_ref, o_ref,
                    ycat_ref, scat_ref, acc_ref, *, n_exp, nb, cap):
    b = pl.program_id(0)
    k = pl.program_id(1)
    tb = TOKEN_BLOCK
    win = COMBINE_WINDOW
    align = V7X_SUBLANES_BF16
    p0s, ends, wss = [], [], []
    fits = None
    for e in range(n_exp):
        base = (b * n_exp + e) * (nb + 1) + k
        p0 = pfx_ref[base]
        end = pfx_ref[base + 1]
        ws = pl.multiple_of(jnp.minimum((p0 // align) * align, cap - win), align)
        ok = end <= ws + win
        fits = ok if fits is None else jnp.logical_and(fits, ok)
        p0s.append(p0)
        ends.append(end)
        wss.append(ws)

    @pl.when(fits)
    def _():
        lane = lax.broadcasted_iota(I32, (1, V7X_LANES), 1)
        per = V7X_LANES // win
        for e0 in range(0, n_exp, per):
            target = lane + wss[e0]
            pos_col = post_ref[:, e0:e0 + 1]
            for j in range(1, per):
                mine = lane >= j * win
                target = jnp.where(mine, lane + (wss[e0 + j] - j * win), target)
                pos_col = jnp.where(mine, post_ref[:, e0 + j:e0 + j + 1], pos_col)
            g = e0 // per
            scat_ref[:, g * V7X_LANES:(g + 1) * V7X_LANES] = (
                jnp.where(pos_col == target, 1.0, 0.0).astype(BF16))
            for j in range(per):
                e = e0 + j
                ycat_ref[e * win:(e + 1) * win, :] = ye_ref[0, e, pl.ds(wss[e], win), :]
        acc_ref[...] = _dot(scat_ref[...], ycat_ref[...])

    @pl.when(jnp.logical_not(fits))
    def _():
        acc_ref[...] = jnp.zeros_like(acc_ref)
        cols = lax.broadcasted_iota(I32, (tb, tb), 1)
        for e in range(n_exp):
            ws1 = pl.multiple_of(jnp.minimum((p0s[e] // align) * align, cap - tb), align)
            ws2 = pl.multiple_of(jnp.minimum(ws1 + tb, cap - tb), align)
            pos_col = post_ref[:, e:e + 1]

            @pl.when(ends[e] > p0s[e])
            def _():
                onehot = jnp.where(pos_col == cols + ws1, 1.0, 0.0).astype(BF16)
                acc_ref[...] += _dot(onehot, ye_ref[0, e, pl.ds(ws1, tb), :])

            @pl.when(ends[e] > ws1 + tb)
            def _():
                c2 = cols + ws2
                hit = jnp.logical_and(pos_col == c2, c2 >= ws1 + tb)
                onehot = jnp.where(hit, 1.0, 0.0).astype(BF16)
                acc_ref[...] += _dot(onehot, ye_ref[0, e, pl.ds(ws2, tb), :])

    gate_f = mod_ref[0, 5:6, :]
    x2 = x1_ref[...] + gate_f * acc_ref[...]
    ms = jnp.mean(x2 * x2, axis=-1, keepdims=True)
    o_ref[...] = x2 * lax.rsqrt(ms + RMS_EPS) * gfin_ref[...]


def _combine_call(pfx, x1, ye, pos_t, mod3, g_final, *, seq):
    t, d = x1.shape
    bsz, n_exp, cap, _ = ye.shape
    tb = TOKEN_BLOCK
    nb = seq // tb
    row = lambda b, k, p: (b * nb + k, 0)
    grid_spec = pltpu.PrefetchScalarGridSpec(
        num_scalar_prefetch=1,
        grid=(bsz, nb),
        in_specs=[pl.BlockSpec((tb, d), row),
                  _resident((1, n_exp, cap, d), lambda b, k, p: (b, 0, 0, 0)),
                  pl.BlockSpec((tb, n_exp), row),
                  pl.BlockSpec((1, N_MOD, d), lambda b, k, p: (b, 0, 0)),
                  pl.BlockSpec((1, d), lambda b, k, p: (0, 0))],
        out_specs=pl.BlockSpec((tb, d), row),
        scratch_shapes=[pltpu.VMEM((n_exp * COMBINE_WINDOW, d), BF16),
                        pltpu.VMEM((tb, n_exp * COMBINE_WINDOW), BF16),
                        pltpu.VMEM((tb, d), F32)],
    )
    return pl.pallas_call(
        functools.partial(_combine_kernel, n_exp=n_exp, nb=nb, cap=cap),
        grid_spec=grid_spec,
        out_shape=jax.ShapeDtypeStruct((t, d), F32),
        compiler_params=_params(("arbitrary", "arbitrary")),
        name="combine",
    )(pfx, x1, ye, pos_t, mod3, g_final)


def _layer(x2, c_act_in, w_ada, b_ada, g_norm_mix, w_in, b_gate, w_fourier, w_conv, w_conv_out,
           w_o, g_norm_moe, w_router, b_router, w_gate_e, w_up_e, w_down_e, *, bsz, seq):
    t, d = x2.shape
    n_exp = w_router.shape[1]
    cap = EC_CAPACITY * seq // n_exp
    n1 = FFT_N1
    n2 = seq // n1
    fc, f1, f2, cos_t, sin_t = _dft_tables(seq)

    mod = _mod_call(c_act_in, w_ada, b_ada.reshape(1, -1))[:bsz]
    mod3 = mod.reshape(bsz, N_MOD, d)

    wre, wim, cv, ga, gb = _proj_call(
        x2, mod3, g_norm_mix.reshape(1, d), w_in.astype(BF16), b_gate.reshape(1, -1), fc, w_conv,
        seq=seq)

    tr, ti = _fft1_call(wre, wim, f1, cos_t, sin_t)
    a2 = _fft2_call(tr, ti, f2).reshape(t, D_FOURIER)

    x1, h2, probs_t = _mix_call(
        a2, cv, ga, gb, x2, mod3, w_fourier.astype(BF16), w_conv_out.astype(BF16),
        w_o.astype(BF16), g_norm_moe.reshape(1, d), w_router.T, b_router.reshape(n_exp, 1),
        seq=seq)

    prob3 = probs_t.reshape(bsz * n_exp, seq // V7X_LANES, V7X_LANES)
    pos, raw = _select_call(prob3, cap)
    stride = TOKEN_BLOCK // V7X_LANES
    starts = raw[:, ::stride, 0]
    pfx = jnp.concatenate([starts, jnp.full((bsz * n_exp, 1), cap, I32)], axis=1).reshape(-1)
    pos_t = jnp.transpose(pos.reshape(bsz, n_exp, seq), (0, 2, 1)).reshape(t, n_exp)

    ye = _moe_call(pfx, h2.reshape(bsz, seq, d), pos, prob3, w_gate_e, w_up_e, w_down_e, cap=cap)
    return ye, pfx, x1, pos_t, mod3


def kernel(x, c, w_ada, b_ada, g_norm_mix, w_in, b_gate, w_fourier, w_conv, w_conv_out, w_o,
           g_norm_moe, w_router, b_router, w_gate_e, w_up_e, w_down_e, g_final):
    bsz, seq, d = x.shape
    assert w_ada.shape[0] == 1
    assert seq % (FFT_N1 * V7X_SUBLANES_F32) == 0 and seq % ROW_TILE == 0
    assert (EC_CAPACITY * seq // w_router.shape[2]) >= TOKEN_BLOCK
    c_pad = jnp.pad(c, ((0, V7X_SUBLANES_F32 - bsz), (0, 0)))
    x2 = x.reshape(bsz * seq, d)
    ye, pfx, x1, pos_t, mod3 = _layer(
        x2, c_pad, w_ada[0], b_ada[0], g_norm_mix[0], w_in[0], b_gate[0], w_fourier[0],
        w_conv[0], w_conv_out[0], w_o[0], g_norm_moe[0], w_router[0], b_router[0],
        w_gate_e[0], w_up_e[0], w_down_e[0], bsz=bsz, seq=seq)
    out = _combine_call(pfx, x1, ye, pos_t, mod3, g_final.reshape(1, d), seq=seq)
    return out.reshape(bsz, seq, d)
```

```python
import functools

import numpy as np
import jax
import jax.numpy as jnp
from jax import lax
from jax.experimental import pallas as pl
from jax.experimental.pallas import tpu as pltpu

F32 = jnp.float32
BF16 = jnp.bfloat16
I32 = jnp.int32
HIGHEST = lax.Precision.HIGHEST

FOURIER_GROUPS = 4
FOURIER_GROUP_DIM = 128
D_FOURIER = FOURIER_GROUPS * FOURIER_GROUP_DIM
N_MOD = 6
EC_CAPACITY = 2
RMS_EPS = 1e-6

V7X_LANES = 128
V7X_SUBLANES_F32 = 8
V7X_SUBLANES_BF16 = 16
V7X_VMEM_BYTES = 64 * 1024 * 1024
VMEM_LIMIT_BYTES = V7X_VMEM_BYTES - 6 * 1024 * 1024

MOD_COL_TILE = 3072
MOD_ACC_COLS = 512
ROW_TILE = 1024
MIX_ROW_TILE = 1024
FFT_N1 = 128
FFT_N2_TILE = 16
FFT2_K_TILE = 16
TOKEN_BLOCK = 256
GATHER_CHUNK = 64
GATHER_UNROLL = 8
COMBINE_WINDOW = 64
COMBINE_BLOCKS = 2
FFN_CHUNK = 512


def _dot(a, b):
    return jnp.dot(a, b, preferred_element_type=F32)


def _params(semantics):
    return pltpu.CompilerParams(dimension_semantics=semantics,
                                vmem_limit_bytes=VMEM_LIMIT_BYTES)


def _pitch(rows):
    p = -(-rows // V7X_SUBLANES_F32)
    return (p if p % 2 else p + 1) * V7X_SUBLANES_F32


def _stage_group(stage_ref, group, value):
    rows = value.shape[0]
    row0 = group * _pitch(rows)
    for l in range(stage_ref.shape[0]):
        stage_ref[l, row0:row0 + rows, :] = value[:, l * V7X_LANES:(l + 1) * V7X_LANES]


def _row_of_each_group(stage_ref, row, groups, rows):
    return jnp.concatenate([stage_ref[l, pl.ds(row, groups, stride=_pitch(rows)), :]
                            for l in range(stage_ref.shape[0])], axis=1)


def _stage_scratch(groups, rows, width):
    return pltpu.VMEM((width // V7X_LANES, groups * _pitch(rows), V7X_LANES), F32)


def _resident(block_shape, index_map):
    return pl.BlockSpec(block_shape, index_map, pipeline_mode=pl.Buffered(1))


def _mod_kernel(ct_ref, w_ref, b_ref, o_ref, *, bsz):
    ct = ct_ref[...]
    ct = ct * jax.nn.sigmoid(ct)
    sub = V7X_SUBLANES_F32
    d, tn = w_ref.shape
    cw = MOD_ACC_COLS
    for c0 in range(0, tn, cw):
        accs = [jnp.zeros((sub, cw), F32) for _ in range(bsz)]
        for g in range(d // sub):
            w = w_ref[g * sub:(g + 1) * sub, c0:c0 + cw]
            for r in range(bsz):
                accs[r] = accs[r] + ct[g * sub:(g + 1) * sub, r:r + 1] * w
        for r in range(bsz):
            o_ref[r:r + 1, c0:c0 + cw] = (jnp.sum(accs[r], axis=0, keepdims=True)
                                          + b_ref[:, c0:c0 + cw])


def _mod_call(c_t, w_ada, b_ada):
    d, bsz = c_t.shape
    n = w_ada.shape[1]
    tn = MOD_COL_TILE
    return pl.pallas_call(
        functools.partial(_mod_kernel, bsz=bsz),
        grid=(n // tn,),
        in_specs=[pl.BlockSpec((d, bsz), lambda j: (0, 0)),
                  pl.BlockSpec((d, tn), lambda j: (0, j)),
                  pl.BlockSpec((1, tn), lambda j: (0, j))],
        out_specs=pl.BlockSpec((bsz, tn), lambda j: (0, j)),
        out_shape=jax.ShapeDtypeStruct((bsz, n), F32),
        compiler_params=_params(("arbitrary",)),
        name="mod",
    )(c_t, w_ada, b_ada)


def _rms_modulate(x, g, shift, scale):
    ms = jnp.mean(x * x, axis=-1, keepdims=True)
    y = x * lax.rsqrt(ms + RMS_EPS) * g
    return y * (1.0 + scale) + shift


def _proj_kernel(x_ref, xprev_ref, xnext_ref, mod_ref, g_ref, win_ref, bgate_ref, fc_ref, wconv_ref,
                 wre_ref, wim_ref, cv_ref, ga_ref, gb_ref, re_s, im_s,
                 *, d, d_conv, n2, tm, tpb):
    i = pl.program_id(0)
    shift = mod_ref[0, 0:1, :]
    scale = mod_ref[0, 1:2, :]
    h = _rms_modulate(x_ref[...], g_ref[...], shift, scale).astype(BF16)
    x_halo = jnp.concatenate([xprev_ref[...], xnext_ref[...]], axis=0)
    h_halo = _rms_modulate(x_halo, g_ref[...], shift, scale).astype(BF16)
    h_ext = jnp.concatenate([h, h_halo], axis=0)

    o1 = D_FOURIER
    o2 = o1 + d_conv
    o3 = o2 + d_conv
    o4 = o3 + d_conv
    uf = _dot(h, win_ref[:, 0:o1]).astype(BF16)
    gd = FOURIER_GROUP_DIM
    ws = [_dot(uf[:, g * gd:(g + 1) * gd], fc_ref[...]) for g in range(FOURIER_GROUPS)]
    w_re = jnp.concatenate([w[:, :gd] for w in ws], axis=1)
    w_im = jnp.concatenate([w[:, gd:] for w in ws], axis=1)
    n1_rows = w_re.shape[0] // n2
    for r in range(n1_rows):
        _stage_group(re_s, r, w_re[r * n2:(r + 1) * n2, :])
        _stage_group(im_s, r, w_im[r * n2:(r + 1) * n2, :])
    for j in range(n2):
        cols = slice(j * D_FOURIER, (j + 1) * D_FOURIER)
        wre_ref[0, :, cols] = _row_of_each_group(re_s, j, n1_rows, n2)
        wim_ref[0, :, cols] = _row_of_each_group(im_s, j, n1_rows, n2)
    q_ext = _dot(h_ext, win_ref[:, o3:o4]) * _dot(h_ext, win_ref[:, o1:o2])
    q = q_ext[0:tm, :]
    nh = xprev_ref.shape[0]
    first = (i % tpb) == 0
    last = (i % tpb) == tpb - 1
    hp = jnp.where(first, 0.0, q_ext[tm + nh - 1:tm + nh, :])
    hn = jnp.where(last, 0.0, q_ext[tm + nh:tm + nh + 1, :])
    rows = lax.broadcasted_iota(I32, (tm, 1), 0)
    q_prev = jnp.where(rows == 0, hp, pltpu.roll(q, 1, axis=0))
    q_next = jnp.where(rows == tm - 1, hn, pltpu.roll(q, tm - 1, axis=0))
    conv = q_prev * wconv_ref[0:1, :] + q * wconv_ref[1:2, :] + q_next * wconv_ref[2:3, :]
    cv_ref[...] = (_dot(h, win_ref[:, o2:o3]) * conv).astype(BF16)
    ga_ref[...] = jax.nn.sigmoid(_dot(h, win_ref[:, o4:o4 + d]) + bgate_ref[:, 0:d]).astype(BF16)
    gb_ref[...] = jax.nn.sigmoid(
        _dot(h, win_ref[:, o4 + d:o4 + 2 * d]) + bgate_ref[:, d:2 * d]).astype(BF16)


def _proj_call(x2, mod3, g_mix, w_in, b_gate, fc, w_conv, *, seq):
    t, d = x2.shape
    k_in = w_in.shape[1]
    d_conv = (k_in - D_FOURIER - 2 * d) // 3
    tm = ROW_TILE
    tpb = seq // tm
    n2 = seq // FFT_N1
    n1_rows = tm // n2
    bsz = t // seq
    sub = V7X_SUBLANES_F32
    row = lambda i: (i, 0)
    const = lambda i: (0, 0)
    out_sds = lambda n: jax.ShapeDtypeStruct((t, n), BF16)
    dft_in = jax.ShapeDtypeStruct((bsz, FFT_N1, n2 * D_FOURIER), F32)
    dft_blk = pl.BlockSpec((1, n1_rows, n2 * D_FOURIER), lambda i: (i // tpb, i % tpb, 0))
    return pl.pallas_call(
        functools.partial(_proj_kernel, d=d, d_conv=d_conv, n2=n2, tm=tm, tpb=tpb),
        grid=(t // tm,),
        in_specs=[pl.BlockSpec((tm, d), row),
                  pl.BlockSpec((sub, d), lambda i: (jnp.maximum(i * (tm // sub) - 1, 0), 0)),
                  pl.BlockSpec((sub, d),
                               lambda i: (jnp.minimum((i + 1) * (tm // sub), t // sub - 1), 0)),
                  pl.BlockSpec((1, N_MOD, d), lambda i: (i // tpb, 0, 0)),
                  pl.BlockSpec((1, d), const),
                  _resident((d, k_in), const),
                  pl.BlockSpec((1, 2 * d), const),
                  pl.BlockSpec(fc.shape, const),
                  pl.BlockSpec(w_conv.shape, const)],
        out_specs=[dft_blk, dft_blk, pl.BlockSpec((tm, d_conv), row),
                   pl.BlockSpec((tm, d), row), pl.BlockSpec((tm, d), row)],
        out_shape=[dft_in, dft_in, out_sds(d_conv), out_sds(d), out_sds(d)],
        scratch_shapes=[_stage_scratch(n1_rows, n2, D_FOURIER)] * 2,
        compiler_params=_params(("parallel",)),
        name="proj",
    )(x2, x2, x2, mod3, g_mix, w_in, b_gate, fc, w_conv)


def _fft1_kernel(wre_ref, wim_ref, f1_ref, cos_ref, sin_ref, tr_ref, ti_ref, re_s, im_s,
                 *, n1, nt):
    reps = D_FOURIER // V7X_LANES
    half = nt // 2
    for h in range(2):
        hcols = slice(h * half * D_FOURIER, (h + 1) * half * D_FOURIER)
        w = jnp.concatenate([wre_ref[0, :, hcols], wim_ref[0, :, hcols]], axis=0).astype(BF16)
        t = _dot(f1_ref[...], w)
        for jj in range(half):
            j = h * half + jj
            cols = slice(jj * D_FOURIER, (jj + 1) * D_FOURIER)
            lanes = slice(j * V7X_LANES, (j + 1) * V7X_LANES)
            c = jnp.concatenate([cos_ref[:, lanes]] * reps, axis=1)
            s = jnp.concatenate([sin_ref[:, lanes]] * reps, axis=1)
            a = t[:n1, cols]
            b = t[n1:, cols]
            _stage_group(re_s, j, a * c + b * s)
            _stage_group(im_s, j, b * c - a * s)
    for k in range(n1):
        tr_ref[0, k] = _row_of_each_group(re_s, k, nt, n1).astype(BF16)
        ti_ref[0, k] = _row_of_each_group(im_s, k, nt, n1).astype(BF16)


def _fft1_call(wre3, wim3, f1, cos_t, sin_t):
    bsz, n1, cols = wre3.shape
    n2 = cols // D_FOURIER
    nt = FFT_N2_TILE
    tn = nt * D_FOURIER
    blk = lambda b, j: (b, 0, j)
    oblk = lambda b, j: (b, 0, j, 0)
    sds = jax.ShapeDtypeStruct((bsz, n1, n2, D_FOURIER), BF16)
    return pl.pallas_call(
        functools.partial(_fft1_kernel, n1=n1, nt=nt),
        grid=(bsz, n2 // nt),
        in_specs=[pl.BlockSpec((1, n1, tn), blk), pl.BlockSpec((1, n1, tn), blk),
                  pl.BlockSpec(f1.shape, lambda b, j: (0, 0)),
                  pl.BlockSpec((n1, nt * V7X_LANES), lambda b, j: (0, j)),
                  pl.BlockSpec((n1, nt * V7X_LANES), lambda b, j: (0, j))],
        out_specs=[pl.BlockSpec((1, n1, nt, D_FOURIER), oblk),
                   pl.BlockSpec((1, n1, nt, D_FOURIER), oblk)],
        out_shape=[sds, sds],
        scratch_shapes=[_stage_scratch(nt, n1, D_FOURIER)] * 2,
        compiler_params=_params(("parallel", "parallel")),
        name="fft1",
    )(wre3, wim3, f1, cos_t, sin_t)


def _fft2_kernel(tr_ref, ti_ref, f2_ref, o_ref, res_s, *, kb, n2):
    for k in range(kb):
        rhs = jnp.concatenate([tr_ref[0, k], ti_ref[0, k]], axis=0)
        _stage_group(res_s, k, _dot(f2_ref[...], rhs))
    for k2 in range(n2):
        o_ref[0, k2] = _row_of_each_group(res_s, k2, kb, n2).astype(BF16)


def _fft2_call(tr4, ti4, f2):
    bsz, n1, n2, dfo = tr4.shape
    kb = FFT2_K_TILE
    blk = lambda b, j: (b, j, 0, 0)
    return pl.pallas_call(
        functools.partial(_fft2_kernel, kb=kb, n2=n2),
        grid=(bsz, n1 // kb),
        in_specs=[pl.BlockSpec((1, kb, n2, dfo), blk), pl.BlockSpec((1, kb, n2, dfo), blk),
                  pl.BlockSpec(f2.shape, lambda b, j: (0, 0))],
        out_specs=pl.BlockSpec((1, n2, kb, dfo), lambda b, j: (b, 0, j, 0)),
        out_shape=jax.ShapeDtypeStruct((bsz, n2, n1, dfo), BF16),
        scratch_shapes=[_stage_scratch(kb, n2, dfo)],
        compiler_params=_params(("parallel", "parallel")),
        name="fft2",
    )(tr4, ti4, f2)


def _dft_tables(seq):
    n1 = FFT_N1
    n2 = seq // n1
    gd = FOURIER_GROUP_DIM
    total_scale = 1.0 / np.sqrt(float(seq) * gd)
    s_c = 2.0 ** -4
    s_1 = 2.0 ** -3
    s_2 = total_scale / (s_c * s_1)

    def cs(n):
        ang = 2.0 * np.pi * np.outer(np.arange(n), np.arange(n)) / n
        return np.cos(ang), np.sin(ang)

    cc, sc = cs(gd)
    fc = np.concatenate([cc, -sc], axis=1) * s_c
    c1, s1 = cs(n1)
    f1 = np.block([[c1, s1], [-s1, c1]]) * s_1
    c2, s2 = cs(n2)
    f2 = np.concatenate([c2, s2], axis=1) * s_2
    ang = 2.0 * np.pi * np.outer(np.arange(n1), np.arange(n2)) / seq
    cos_t = np.repeat(np.cos(ang), V7X_LANES, axis=1)
    sin_t = np.repeat(np.sin(ang), V7X_LANES, axis=1)
    as_bf16 = lambda m: jnp.asarray(m, F32).astype(BF16)
    return (as_bf16(fc), as_bf16(f1), as_bf16(f2),
            jnp.asarray(cos_t, F32), jnp.asarray(sin_t, F32))


def _mix_kernel(a_ref, cv_ref, ga_ref, gb_ref, x_ref, mod_ref,
                wf_ref, wco_ref, wo_ref, gmoe_ref, wrt_ref, br_ref,
                x1_ref, h2_ref, pt_ref):
    y_b = _dot(cv_ref[...], wco_ref[...])
    y_a = _dot(a_ref[...], wf_ref[...])
    z = ga_ref[...].astype(F32) * y_a + gb_ref[...].astype(F32) * y_b
    mix = _dot(z.astype(BF16), wo_ref[...])
    gate_m = mod_ref[0, 2:3, :]
    x1 = x_ref[...] + gate_m * mix
    x1_ref[...] = x1

    h2 = _rms_modulate(x1, gmoe_ref[...], mod_ref[0, 3:4, :], mod_ref[0, 4:5, :])
    h2_ref[...] = h2.astype(BF16)
    logits = lax.dot_general(wrt_ref[...], h2, (((1,), (1,)), ((), ())),
                             preferred_element_type=F32, precision=HIGHEST) + br_ref[...]
    m = jnp.max(logits, axis=0, keepdims=True)
    e = jnp.exp(logits - m)
    probs_t = e / jnp.sum(e, axis=0, keepdims=True)
    pt_ref[0] = probs_t


def _mix_call(a2, cv, ga, gb, x2, mod3, wf, wco, wo, g_moe, wr_t, b_r, *, seq):
    t, d = x2.shape
    n_exp = wr_t.shape[0]
    tm = MIX_ROW_TILE
    tpb = seq // tm
    bsz = t // seq
    row = lambda i: (i, 0)
    const = lambda i: (0, 0)
    return pl.pallas_call(
        _mix_kernel,
        grid=(t // tm,),
        in_specs=[pl.BlockSpec((tm, a2.shape[1]), row),
                  pl.BlockSpec((tm, cv.shape[1]), row),
                  pl.BlockSpec((tm, d), row), pl.BlockSpec((tm, d), row),
                  pl.BlockSpec((tm, d), row),
                  pl.BlockSpec((1, N_MOD, d), lambda i: (i // tpb, 0, 0)),
                  _resident(wf.shape, const), _resident(wco.shape, const),
                  _resident(wo.shape, const),
                  pl.BlockSpec((1, d), const),
                  pl.BlockSpec(wr_t.shape, const), pl.BlockSpec(b_r.shape, const)],
        out_specs=[pl.BlockSpec((tm, d), row), pl.BlockSpec((tm, d), row),
                   pl.BlockSpec((1, n_exp, tm), lambda i: (i // tpb, 0, i % tpb))],
        out_shape=[jax.ShapeDtypeStruct((t, d), F32), jax.ShapeDtypeStruct((t, d), BF16),
                   jax.ShapeDtypeStruct((bsz, n_exp, seq), F32)],
        compiler_params=_params(("parallel",)),
        name="mix",
    )(a2, cv, ga, gb, x2, mod3, wf, wco, wo, g_moe, wr_t, b_r)


def _excl_cumsum(mask_f, upper, lower_strict):
    r, rows, lanes = mask_f.shape
    m2 = mask_f.reshape(r * rows, lanes).astype(BF16)
    incl = _dot(m2, upper)
    tot = jnp.broadcast_to(incl[:, lanes - 1:lanes], incl.shape).astype(BF16)
    tot3 = tot.reshape(r, rows, lanes)
    offs = [_dot(lower_strict, tot3[j]) for j in range(r)]
    off = jnp.stack(offs, axis=0)
    return incl.reshape(r, rows, lanes) - mask_f + off


def _select_kernel(p_ref, upper_ref, lower_ref, pos_ref, raw_ref, *, cap):
    p = p_ref[...]
    r = p.shape[0]

    def count(mask):
        c = jnp.sum(mask.astype(F32), axis=2, keepdims=True)
        return jnp.sum(c, axis=1, keepdims=True)

    prefix = jnp.zeros(p.shape, I32)
    for bit in range(29, -1, -1):
        cand = prefix | (1 << bit)
        keep = count(p >= pltpu.bitcast(cand, F32)) >= cap
        prefix = jnp.where(keep, cand, prefix)
    thr = pltpu.bitcast(prefix, F32)
    gt = p > thr
    eq = (p == thr).astype(F32)
    need = cap - count(gt)
    upper = upper_ref[...]
    lower = lower_ref[...]
    rank_eq = _excl_cumsum(eq, upper, lower)
    sel = jnp.where(gt, 1.0, jnp.where(rank_eq < need, eq, 0.0))
    raw = _excl_cumsum(sel, upper, lower)
    raw_i = raw.astype(I32)
    raw_ref[...] = raw_i
    pos_ref[...] = jnp.where(sel > 0.0, raw_i, -1)


def _select_call(p3, cap):
    r, rows, lanes = p3.shape
    upper = jnp.asarray(np.triu(np.ones((lanes, lanes))), BF16)
    lower = jnp.asarray(np.tril(np.ones((rows, rows)), -1), BF16)
    full = lambda shape: pl.BlockSpec(shape, lambda i: (0,) * len(shape))
    sds = jax.ShapeDtypeStruct(p3.shape, I32)
    return pl.pallas_call(
        functools.partial(_select_kernel, cap=cap),
        grid=(1,),
        in_specs=[full(p3.shape), full(upper.shape), full(lower.shape)],
        out_specs=[full(p3.shape), full(p3.shape)],
        out_shape=[sds, sds],
        compiler_params=_params(("arbitrary",)),
        name="select",
    )(p3, upper, lower)


def _gather_kernel(pfx_ref, h2_ref, pos_ref, prob_ref, xe_out_ref, val_out_ref,
                   val_ref, xe_ref, *, n_exp, nb, cap):
    b = pl.program_id(0)
    e = pl.program_id(1)
    tb = TOKEN_BLOCK
    ch = GATHER_CHUNK
    halves = tb // V7X_LANES
    sub = V7X_SUBLANES_BF16

    xe_ref[...] = jnp.zeros_like(xe_ref)
    val_ref[...] = jnp.zeros_like(val_ref)
    base = (b * n_exp + e) * (nb + 1)
    chunk_rows = lax.broadcasted_iota(I32, (ch, V7X_LANES), 0)

    def add_chunk(k, ws):
        ws = pl.multiple_of(ws, sub)
        tok0 = pl.multiple_of(k * tb, tb)
        rows = chunk_rows + ws
        hits = [jnp.where(pos_ref[0, pl.ds(halves * k + j, 1), :] == rows, 1.0, 0.0)
                for j in range(halves)]
        onehot = jnp.concatenate([h.astype(BF16) for h in hits], axis=1)
        xe_ref[pl.ds(ws, ch), :] += _dot(onehot, h2_ref[0, pl.ds(tok0, tb), :]).astype(BF16)
        v = hits[0] * prob_ref[0, pl.ds(halves * k, 1), :]
        for j in range(1, halves):
            v = v + hits[j] * prob_ref[0, pl.ds(halves * k + j, 1), :]
        val_ref[pl.ds(ws, ch), :] += jnp.sum(v, axis=1, keepdims=True)

    def group(g, carry):
        more = []
        for j in range(GATHER_UNROLL):
            k = g * GATHER_UNROLL + j
            p0 = pfx_ref[base + k]
            cnt = pfx_ref[base + k + 1] - p0
            start = (p0 // sub) * sub
            add_chunk(k, start)
            more.append((k, start, (p0 - start + cnt + ch - 1) // ch))
        for k, start, n_chunks in more:
            @pl.when(n_chunks > 1)
            def _():
                def chunk(c, carry2):
                    add_chunk(k, start + c * ch)
                    return carry2
                lax.fori_loop(1, n_chunks, chunk, 0)
        return carry

    lax.fori_loop(0, nb // GATHER_UNROLL, group, 0)
    xe_out_ref[0, 0] = xe_ref[0:cap, :]
    val_out_ref[0, 0] = val_ref[0:cap, :]


def _gather_call(pfx, h2_3, pos3, prob3, *, n_exp, cap):
    bsz, seq, d = h2_3.shape
    nb = seq // TOKEN_BLOCK
    grid_spec = pltpu.PrefetchScalarGridSpec(
        num_scalar_prefetch=1,
        grid=(bsz, n_exp),
        in_specs=[_resident((1, seq, d), lambda b, e, p: (b, 0, 0)),
                  pl.BlockSpec((1,) + pos3.shape[1:], lambda b, e, p: (b * n_exp + e, 0, 0)),
                  pl.BlockSpec((1,) + prob3.shape[1:], lambda b, e, p: (b * n_exp + e, 0, 0))],
        out_specs=[pl.BlockSpec((1, 1, cap, d), lambda b, e, p: (b, e, 0, 0)),
                   pl.BlockSpec((1, 1, cap, 1), lambda b, e, p: (b, e, 0, 0))],
        scratch_shapes=[pltpu.VMEM((cap + GATHER_CHUNK, 1), F32),
                        pltpu.VMEM((cap + GATHER_CHUNK, d), BF16)],
    )
    return pl.pallas_call(
        functools.partial(_gather_kernel, n_exp=n_exp, nb=nb, cap=cap),
        grid_spec=grid_spec,
        out_shape=[jax.ShapeDtypeStruct((bsz, n_exp, cap, d), BF16),
                   jax.ShapeDtypeStruct((bsz, n_exp, cap, 1), F32)],
        compiler_params=_params(("arbitrary", "arbitrary")),
        name="gather",
    )(pfx, h2_3, pos3, prob3)


def _ffn_kernel(xe_ref, val_ref, wg_ref, wu_ref, wd_ref, ye_ref, acc_ref):
    f = pl.program_id(1)
    bsz, _, cap, d = xe_ref.shape

    @pl.when(f == 0)
    def _():
        acc_ref[...] = jnp.zeros_like(acc_ref)

    xe = xe_ref[...].reshape(bsz * cap, d)
    a = _dot(xe, wg_ref[0].astype(BF16))
    u = _dot(xe, wu_ref[0].astype(BF16))
    hm = (a * jax.nn.sigmoid(a) * u).astype(BF16)
    acc_ref[...] += _dot(hm, wd_ref[0].astype(BF16))

    @pl.when(f == pl.num_programs(1) - 1)
    def _():
        y = acc_ref[...] * val_ref[...].reshape(bsz * cap, 1)
        ye_ref[...] = y.astype(BF16).reshape(bsz, 1, cap, d)


def _ffn_call(xe, val, w_gate, w_up, w_down):
    bsz, n_exp, cap, d = xe.shape
    d_exp = w_gate.shape[2]
    fc = FFN_CHUNK
    return pl.pallas_call(
        _ffn_kernel,
        grid=(n_exp, d_exp // fc),
        in_specs=[pl.BlockSpec((bsz, 1, cap, d), lambda e, f: (0, e, 0, 0)),
                  pl.BlockSpec((bsz, 1, cap, 1), lambda e, f: (0, e, 0, 0)),
                  pl.BlockSpec((1, d, fc), lambda e, f: (e, 0, f)),
                  pl.BlockSpec((1, d, fc), lambda e, f: (e, 0, f)),
                  pl.BlockSpec((1, fc, d), lambda e, f: (e, f, 0))],
        out_specs=pl.BlockSpec((bsz, 1, cap, d), lambda e, f: (0, e, 0, 0)),
        out_shape=jax.ShapeDtypeStruct((bsz, n_exp, cap, d), BF16),
        scratch_shapes=[pltpu.VMEM((bsz * cap, d), F32)],
        compiler_params=_params(("arbitrary", "arbitrary")),
        name="ffn",
    )(xe, val, w_gate, w_up, w_down)


def _combine_kernel(pfx_ref, x1_ref, ye_ref, post_ref, mod_ref, gfin_ref, o_ref,
                    ycat_ref, scat_ref, acc_ref, *, n_exp, nb, cap):
    b = pl.program_id(0)
    tb = TOKEN_BLOCK
    win = COMBINE_WINDOW
    align = V7X_SUBLANES_BF16
    gate_f = mod_ref[0, 5:6, :]
    lane = lax.broadcasted_iota(I32, (1, V7X_LANES), 1)
    per = V7X_LANES // win

    def finish(rows, moe):
        x2 = x1_ref[rows, :] + gate_f * moe
        ms = jnp.mean(x2 * x2, axis=-1, keepdims=True)
        o_ref[rows, :] = x2 * lax.rsqrt(ms + RMS_EPS) * gfin_ref[...]

    redo = []
    for s in range(COMBINE_BLOCKS):
        k = pl.program_id(1) * COMBINE_BLOCKS + s
        rows = slice(s * tb, (s + 1) * tb)
        p0s, ends, wss = [], [], []
        fits = None
        for e in range(n_exp):
            base = (b * n_exp + e) * (nb + 1) + k
            p0 = pfx_ref[base]
            end = pfx_ref[base + 1]
            ws = pl.multiple_of(jnp.minimum((p0 // align) * align, cap - win), align)
            ok = end <= ws + win
            fits = ok if fits is None else jnp.logical_and(fits, ok)
            p0s.append(p0)
            ends.append(end)
            wss.append(ws)
        for e0 in range(0, n_exp, per):
            target = lane + wss[e0]
            pos_col = post_ref[rows, e0:e0 + 1]
            for j in range(1, per):
                mine = lane >= j * win
                target = jnp.where(mine, lane + (wss[e0 + j] - j * win), target)
                pos_col = jnp.where(mine, post_ref[rows, e0 + j:e0 + j + 1], pos_col)
            g = e0 // per
            scat_ref[s, :, g * V7X_LANES:(g + 1) * V7X_LANES] = (
                jnp.where(pos_col == target, 1.0, 0.0).astype(BF16))
            for j in range(per):
                e = e0 + j
                ycat_ref[s, e * win:(e + 1) * win, :] = ye_ref[0, e, pl.ds(wss[e], win), :]
        finish(rows, _dot(scat_ref[s], ycat_ref[s]))
        redo.append((rows, jnp.logical_not(fits), p0s, ends))

    for rows, misfit, p0s, ends in redo:
        @pl.when(misfit)
        def _():
            acc_ref[...] = jnp.zeros_like(acc_ref)
            cols = lax.broadcasted_iota(I32, (tb, tb), 1)
            for e in range(n_exp):
                ws1 = pl.multiple_of(jnp.minimum((p0s[e] // align) * align, cap - tb), align)
                ws2 = pl.multiple_of(jnp.minimum(ws1 + tb, cap - tb), align)
                pos_col = post_ref[rows, e:e + 1]

                @pl.when(ends[e] > p0s[e])
                def _():
                    onehot = jnp.where(pos_col == cols + ws1, 1.0, 0.0).astype(BF16)
                    acc_ref[...] += _dot(onehot, ye_ref[0, e, pl.ds(ws1, tb), :])

                @pl.when(ends[e] > ws1 + tb)
                def _():
                    c2 = cols + ws2
                    hit = jnp.logical_and(pos_col == c2, c2 >= ws1 + tb)
                    onehot = jnp.where(hit, 1.0, 0.0).astype(BF16)
                    acc_ref[...] += _dot(onehot, ye_ref[0, e, pl.ds(ws2, tb), :])
            finish(rows, acc_ref[...])


def _combine_call(pfx, x1, ye, pos_t, mod3, g_final, *, seq):
    t, d = x1.shape
    bsz, n_exp, cap, _ = ye.shape
    tb = TOKEN_BLOCK
    nb = seq // tb
    step_rows = COMBINE_BLOCKS * tb
    steps = nb // COMBINE_BLOCKS
    row = lambda b, k, p: (b * steps + k, 0)
    grid_spec = pltpu.PrefetchScalarGridSpec(
        num_scalar_prefetch=1,
        grid=(bsz, steps),
        in_specs=[pl.BlockSpec((step_rows, d), row),
                  _resident((1, n_exp, cap, d), lambda b, k, p: (b, 0, 0, 0)),
                  pl.BlockSpec((step_rows, n_exp), row),
                  pl.BlockSpec((1, N_MOD, d), lambda b, k, p: (b, 0, 0)),
                  pl.BlockSpec((1, d), lambda b, k, p: (0, 0))],
        out_specs=pl.BlockSpec((step_rows, d), row),
        scratch_shapes=[pltpu.VMEM((COMBINE_BLOCKS, n_exp * COMBINE_WINDOW, d), BF16),
                        pltpu.VMEM((COMBINE_BLOCKS, tb, n_exp * COMBINE_WINDOW), BF16),
                        pltpu.VMEM((tb, d), F32)],
    )
    return pl.pallas_call(
        functools.partial(_combine_kernel, n_exp=n_exp, nb=nb, cap=cap),
        grid_spec=grid_spec,
        out_shape=jax.ShapeDtypeStruct((t, d), F32),
        compiler_params=_params(("arbitrary", "arbitrary")),
        name="combine",
    )(pfx, x1, ye, pos_t, mod3, g_final)


def _layer(x2, c_act_in, w_ada, b_ada, g_norm_mix, w_in, b_gate, w_fourier, w_conv, w_conv_out,
           w_o, g_norm_moe, w_router, b_router, w_gate_e, w_up_e, w_down_e, *, bsz, seq):
    t, d = x2.shape
    n_exp = w_router.shape[1]
    cap = EC_CAPACITY * seq // n_exp
    n1 = FFT_N1
    n2 = seq // n1
    fc, f1, f2, cos_t, sin_t = _dft_tables(seq)

    mod = _mod_call(c_act_in, w_ada, b_ada.reshape(1, -1))
    mod3 = mod.reshape(bsz, N_MOD, d)

    wre, wim, cv, ga, gb = _proj_call(
        x2, mod3, g_norm_mix.reshape(1, d), w_in.astype(BF16), b_gate.reshape(1, -1), fc, w_conv,
        seq=seq)

    tr, ti = _fft1_call(wre, wim, f1, cos_t, sin_t)
    a2 = _fft2_call(tr, ti, f2).reshape(t, D_FOURIER)

    x1, h2, probs_t = _mix_call(
        a2, cv, ga, gb, x2, mod3, w_fourier.astype(BF16), w_conv_out.astype(BF16),
        w_o.astype(BF16), g_norm_moe.reshape(1, d), w_router.T, b_router.reshape(n_exp, 1),
        seq=seq)

    prob3 = probs_t.reshape(bsz * n_exp, seq // V7X_LANES, V7X_LANES)
    pos, raw = _select_call(prob3, cap)
    stride = TOKEN_BLOCK // V7X_LANES
    starts = raw[:, ::stride, 0]
    pfx = jnp.concatenate([starts, jnp.full((bsz * n_exp, 1), cap, I32)], axis=1).reshape(-1)
    pos_t = jnp.transpose(pos.reshape(bsz, n_exp, seq), (0, 2, 1)).reshape(t, n_exp)

    xe, val = _gather_call(pfx, h2.reshape(bsz, seq, d), pos, prob3, n_exp=n_exp, cap=cap)
    ye = _ffn_call(xe, val, w_gate_e, w_up_e, w_down_e)
    return ye, pfx, x1, pos_t, mod3


def kernel(x, c, w_ada, b_ada, g_norm_mix, w_in, b_gate, w_fourier, w_conv, w_conv_out, w_o,
           g_norm_moe, w_router, b_router, w_gate_e, w_up_e, w_down_e, g_final):
    bsz, seq, d = x.shape
    assert w_ada.shape[0] == 1
    assert seq % (FFT_N1 * V7X_SUBLANES_F32) == 0 and seq % ROW_TILE == 0
    assert (EC_CAPACITY * seq // w_router.shape[2]) >= TOKEN_BLOCK
    c_pad = c.T
    x2 = x.reshape(bsz * seq, d)
    ye, pfx, x1, pos_t, mod3 = _layer(
        x2, c_pad, w_ada[0], b_ada[0], g_norm_mix[0], w_in[0], b_gate[0], w_fourier[0],
        w_conv[0], w_conv_out[0], w_o[0], g_norm_moe[0], w_router[0], b_router[0],
        w_gate_e[0], w_up_e[0], w_down_e[0], bsz=bsz, seq=seq)
    out = _combine_call(pfx, x1, ye, pos_t, mod3, g_final.reshape(1, d), seq=seq)
    return out.reshape(bsz, seq, d)
```

```python
import functools

import numpy as np
import jax
import jax.numpy as jnp
from jax import lax
from jax.experimental import pallas as pl
from jax.experimental.pallas import tpu as pltpu

F32 = jnp.float32
BF16 = jnp.bfloat16
I32 = jnp.int32
HIGHEST = lax.Precision.HIGHEST

FOURIER_GROUPS = 4
FOURIER_GROUP_DIM = 128
D_FOURIER = FOURIER_GROUPS * FOURIER_GROUP_DIM
N_MOD = 6
EC_CAPACITY = 2
RMS_EPS = 1e-6

V7X_LANES = 128
V7X_SUBLANES_F32 = 8
V7X_SUBLANES_BF16 = 16
V7X_VMEM_BYTES = 64 * 1024 * 1024
VMEM_LIMIT_BYTES = V7X_VMEM_BYTES - 6 * 1024 * 1024

MOD_COL_TILE = 3072
MOD_ACC_COLS = 512
ROW_TILE = 1024
MIX_ROW_TILE = 1024
FFT_N1 = 128
FFT_N2_TILE = 16
FFT2_K_TILE = 16
TOKEN_BLOCK = 256
GATHER_CHUNK = 64
GATHER_BLOCKS = 4
COMBINE_WINDOW = 64
COMBINE_BLOCKS = 2
FFN_CHUNK = 512


def _dot(a, b):
    return jnp.dot(a, b, preferred_element_type=F32)


def _params(semantics):
    return pltpu.CompilerParams(dimension_semantics=semantics,
                                vmem_limit_bytes=VMEM_LIMIT_BYTES)


def _pitch(rows):
    p = -(-rows // V7X_SUBLANES_F32)
    return (p if p % 2 else p + 1) * V7X_SUBLANES_F32


def _stage_group(stage_ref, group, value):
    rows = value.shape[0]
    row0 = group * _pitch(rows)
    for l in range(stage_ref.shape[0]):
        stage_ref[l, row0:row0 + rows, :] = value[:, l * V7X_LANES:(l + 1) * V7X_LANES]


def _row_of_each_group(stage_ref, row, groups, rows):
    return jnp.concatenate([stage_ref[l, pl.ds(row, groups, stride=_pitch(rows)), :]
                            for l in range(stage_ref.shape[0])], axis=1)


def _stage_scratch(groups, rows, width):
    return pltpu.VMEM((width // V7X_LANES, groups * _pitch(rows), V7X_LANES), F32)


def _resident(block_shape, index_map):
    return pl.BlockSpec(block_shape, index_map, pipeline_mode=pl.Buffered(1))


def _mod_kernel(ct_ref, w_ref, b_ref, o_ref, *, bsz):
    ct = ct_ref[...]
    ct = ct * jax.nn.sigmoid(ct)
    sub = V7X_SUBLANES_F32
    d, tn = w_ref.shape
    cw = MOD_ACC_COLS
    for c0 in range(0, tn, cw):
        accs = [jnp.zeros((sub, cw), F32) for _ in range(bsz)]
        for g in range(d // sub):
            w = w_ref[g * sub:(g + 1) * sub, c0:c0 + cw]
            for r in range(bsz):
                accs[r] = accs[r] + ct[g * sub:(g + 1) * sub, r:r + 1] * w
        for r in range(bsz):
            o_ref[r:r + 1, c0:c0 + cw] = (jnp.sum(accs[r], axis=0, keepdims=True)
                                          + b_ref[:, c0:c0 + cw])


def _mod_call(c_t, w_ada, b_ada):
    d, bsz = c_t.shape
    n = w_ada.shape[1]
    tn = MOD_COL_TILE
    return pl.pallas_call(
        functools.partial(_mod_kernel, bsz=bsz),
        grid=(n // tn,),
        in_specs=[pl.BlockSpec((d, bsz), lambda j: (0, 0)),
                  pl.BlockSpec((d, tn), lambda j: (0, j)),
                  pl.BlockSpec((1, tn), lambda j: (0, j))],
        out_specs=pl.BlockSpec((bsz, tn), lambda j: (0, j)),
        out_shape=jax.ShapeDtypeStruct((bsz, n), F32),
        compiler_params=_params(("arbitrary",)),
        name="mod",
    )(c_t, w_ada, b_ada)


def _rms_modulate(x, g, shift, scale):
    ms = jnp.mean(x * x, axis=-1, keepdims=True)
    y = x * lax.rsqrt(ms + RMS_EPS) * g
    return y * (1.0 + scale) + shift


def _proj_kernel(x_ref, xprev_ref, xnext_ref, mod_ref, g_ref, win_ref, bgate_ref, fc_ref, wconv_ref,
                 wre_ref, wim_ref, cv_ref, ga_ref, gb_ref, re_s, im_s,
                 *, d, d_conv, n2, tm, tpb):
    i = pl.program_id(0)
    shift = mod_ref[0, 0:1, :]
    scale = mod_ref[0, 1:2, :]
    h = _rms_modulate(x_ref[...], g_ref[...], shift, scale).astype(BF16)
    x_halo = jnp.concatenate([xprev_ref[...], xnext_ref[...]], axis=0)
    h_halo = _rms_modulate(x_halo, g_ref[...], shift, scale).astype(BF16)
    h_ext = jnp.concatenate([h, h_halo], axis=0)

    o1 = D_FOURIER
    o2 = o1 + d_conv
    o3 = o2 + d_conv
    o4 = o3 + d_conv
    uf = _dot(h, win_ref[:, 0:o1]).astype(BF16)
    gd = FOURIER_GROUP_DIM
    ws = [_dot(uf[:, g * gd:(g + 1) * gd], fc_ref[...]) for g in range(FOURIER_GROUPS)]
    w_re = jnp.concatenate([w[:, :gd] for w in ws], axis=1)
    w_im = jnp.concatenate([w[:, gd:] for w in ws], axis=1)
    n1_rows = w_re.shape[0] // n2
    for r in range(n1_rows):
        _stage_group(re_s, r, w_re[r * n2:(r + 1) * n2, :])
        _stage_group(im_s, r, w_im[r * n2:(r + 1) * n2, :])
    for j in range(n2):
        cols = slice(j * D_FOURIER, (j + 1) * D_FOURIER)
        wre_ref[0, :, cols] = _row_of_each_group(re_s, j, n1_rows, n2)
        wim_ref[0, :, cols] = _row_of_each_group(im_s, j, n1_rows, n2)
    q_ext = _dot(h_ext, win_ref[:, o3:o4]) * _dot(h_ext, win_ref[:, o1:o2])
    q = q_ext[0:tm, :]
    nh = xprev_ref.shape[0]
    first = (i % tpb) == 0
    last = (i % tpb) == tpb - 1
    hp = jnp.where(first, 0.0, q_ext[tm + nh - 1:tm + nh, :])
    hn = jnp.where(last, 0.0, q_ext[tm + nh:tm + nh + 1, :])
    rows = lax.broadcasted_iota(I32, (tm, 1), 0)
    q_prev = jnp.where(rows == 0, hp, pltpu.roll(q, 1, axis=0))
    q_next = jnp.where(rows == tm - 1, hn, pltpu.roll(q, tm - 1, axis=0))
    conv = q_prev * wconv_ref[0:1, :] + q * wconv_ref[1:2, :] + q_next * wconv_ref[2:3, :]
    cv_ref[...] = (_dot(h, win_ref[:, o2:o3]) * conv).astype(BF16)
    ga_ref[...] = jax.nn.sigmoid(_dot(h, win_ref[:, o4:o4 + d]) + bgate_ref[:, 0:d]).astype(BF16)
    gb_ref[...] = jax.nn.sigmoid(
        _dot(h, win_ref[:, o4 + d:o4 + 2 * d]) + bgate_ref[:, d:2 * d]).astype(BF16)


def _proj_call(x2, mod3, g_mix, w_in, b_gate, fc, w_conv, *, seq):
    t, d = x2.shape
    k_in = w_in.shape[1]
    d_conv = (k_in - D_FOURIER - 2 * d) // 3
    tm = ROW_TILE
    tpb = seq // tm
    n2 = seq // FFT_N1
    n1_rows = tm // n2
    bsz = t // seq
    sub = V7X_SUBLANES_F32
    row = lambda i: (i, 0)
    const = lambda i: (0, 0)
    out_sds = lambda n: jax.ShapeDtypeStruct((t, n), BF16)
    dft_in = jax.ShapeDtypeStruct((bsz, FFT_N1, n2 * D_FOURIER), F32)
    dft_blk = pl.BlockSpec((1, n1_rows, n2 * D_FOURIER), lambda i: (i // tpb, i % tpb, 0))
    return pl.pallas_call(
        functools.partial(_proj_kernel, d=d, d_conv=d_conv, n2=n2, tm=tm, tpb=tpb),
        grid=(t // tm,),
        in_specs=[pl.BlockSpec((tm, d), row),
                  pl.BlockSpec((sub, d), lambda i: (jnp.maximum(i * (tm // sub) - 1, 0), 0)),
                  pl.BlockSpec((sub, d),
                               lambda i: (jnp.minimum((i + 1) * (tm // sub), t // sub - 1), 0)),
                  pl.BlockSpec((1, N_MOD, d), lambda i: (i // tpb, 0, 0)),
                  pl.BlockSpec((1, d), const),
                  _resident((d, k_in), const),
                  pl.BlockSpec((1, 2 * d), const),
                  pl.BlockSpec(fc.shape, const),
                  pl.BlockSpec(w_conv.shape, const)],
        out_specs=[dft_blk, dft_blk, pl.BlockSpec((tm, d_conv), row),
                   pl.BlockSpec((tm, d), row), pl.BlockSpec((tm, d), row)],
        out_shape=[dft_in, dft_in, out_sds(d_conv), out_sds(d), out_sds(d)],
        scratch_shapes=[_stage_scratch(n1_rows, n2, D_FOURIER)] * 2,
        compiler_params=_params(("parallel",)),
        name="proj",
    )(x2, x2, x2, mod3, g_mix, w_in, b_gate, fc, w_conv)


def _fft1_kernel(wre_ref, wim_ref, f1_ref, cos_ref, sin_ref, tr_ref, ti_ref, re_s, im_s,
                 *, n1, nt):
    reps = D_FOURIER // V7X_LANES
    half = nt // 2
    for h in range(2):
        hcols = slice(h * half * D_FOURIER, (h + 1) * half * D_FOURIER)
        w = jnp.concatenate([wre_ref[0, :, hcols], wim_ref[0, :, hcols]], axis=0).astype(BF16)
        t = _dot(f1_ref[...], w)
        for jj in range(half):
            j = h * half + jj
            cols = slice(jj * D_FOURIER, (jj + 1) * D_FOURIER)
            lanes = slice(j * V7X_LANES, (j + 1) * V7X_LANES)
            c = jnp.concatenate([cos_ref[:, lanes]] * reps, axis=1)
            s = jnp.concatenate([sin_ref[:, lanes]] * reps, axis=1)
            a = t[:n1, cols]
            b = t[n1:, cols]
            _stage_group(re_s, j, a * c + b * s)
            _stage_group(im_s, j, b * c - a * s)
    for k in range(n1):
        tr_ref[0, k] = _row_of_each_group(re_s, k, nt, n1).astype(BF16)
        ti_ref[0, k] = _row_of_each_group(im_s, k, nt, n1).astype(BF16)


def _fft1_call(wre3, wim3, f1, cos_t, sin_t):
    bsz, n1, cols = wre3.shape
    n2 = cols // D_FOURIER
    nt = FFT_N2_TILE
    tn = nt * D_FOURIER
    blk = lambda b, j: (b, 0, j)
    oblk = lambda b, j: (b, 0, j, 0)
    sds = jax.ShapeDtypeStruct((bsz, n1, n2, D_FOURIER), BF16)
    return pl.pallas_call(
        functools.partial(_fft1_kernel, n1=n1, nt=nt),
        grid=(bsz, n2 // nt),
        in_specs=[pl.BlockSpec((1, n1, tn), blk), pl.BlockSpec((1, n1, tn), blk),
                  pl.BlockSpec(f1.shape, lambda b, j: (0, 0)),
                  pl.BlockSpec((n1, nt * V7X_LANES), lambda b, j: (0, j)),
                  pl.BlockSpec((n1, nt * V7X_LANES), lambda b, j: (0, j))],
        out_specs=[pl.BlockSpec((1, n1, nt, D_FOURIER), oblk),
                   pl.BlockSpec((1, n1, nt, D_FOURIER), oblk)],
        out_shape=[sds, sds],
        scratch_shapes=[_stage_scratch(nt, n1, D_FOURIER)] * 2,
        compiler_params=_params(("parallel", "parallel")),
        name="fft1",
    )(wre3, wim3, f1, cos_t, sin_t)


def _fft2_kernel(tr_ref, ti_ref, f2_ref, o_ref, res_s, *, kb, n2):
    for k in range(kb):
        rhs = jnp.concatenate([tr_ref[0, k], ti_ref[0, k]], axis=0)
        _stage_group(res_s, k, _dot(f2_ref[...], rhs))
    for k2 in range(n2):
        o_ref[0, k2] = _row_of_each_group(res_s, k2, kb, n2).astype(BF16)


def _fft2_call(tr4, ti4, f2):
    bsz, n1, n2, dfo = tr4.shape
    kb = FFT2_K_TILE
    blk = lambda b, j: (b, j, 0, 0)
    return pl.pallas_call(
        functools.partial(_fft2_kernel, kb=kb, n2=n2),
        grid=(bsz, n1 // kb),
        in_specs=[pl.BlockSpec((1, kb, n2, dfo), blk), pl.BlockSpec((1, kb, n2, dfo), blk),
                  pl.BlockSpec(f2.shape, lambda b, j: (0, 0))],
        out_specs=pl.BlockSpec((1, n2, kb, dfo), lambda b, j: (b, 0, j, 0)),
        out_shape=jax.ShapeDtypeStruct((bsz, n2, n1, dfo), BF16),
        scratch_shapes=[_stage_scratch(kb, n2, dfo)],
        compiler_params=_params(("parallel", "parallel")),
        name="fft2",
    )(tr4, ti4, f2)


def _dft_tables(seq):
    n1 = FFT_N1
    n2 = seq // n1
    gd = FOURIER_GROUP_DIM
    total_scale = 1.0 / np.sqrt(float(seq) * gd)
    s_c = 2.0 ** -4
    s_1 = 2.0 ** -3
    s_2 = total_scale / (s_c * s_1)

    def cs(n):
        ang = 2.0 * np.pi * np.outer(np.arange(n), np.arange(n)) / n
        return np.cos(ang), np.sin(ang)

    cc, sc = cs(gd)
    fc = np.concatenate([cc, -sc], axis=1) * s_c
    c1, s1 = cs(n1)
    f1 = np.block([[c1, s1], [-s1, c1]]) * s_1
    c2, s2 = cs(n2)
    f2 = np.concatenate([c2, s2], axis=1) * s_2
    ang = 2.0 * np.pi * np.outer(np.arange(n1), np.arange(n2)) / seq
    cos_t = np.repeat(np.cos(ang), V7X_LANES, axis=1)
    sin_t = np.repeat(np.sin(ang), V7X_LANES, axis=1)
    as_bf16 = lambda m: jnp.asarray(m, F32).astype(BF16)
    return (as_bf16(fc), as_bf16(f1), as_bf16(f2),
            jnp.asarray(cos_t, F32), jnp.asarray(sin_t, F32))


def _mix_kernel(a_ref, cv_ref, ga_ref, gb_ref, x_ref, mod_ref,
                wf_ref, wco_ref, wo_ref, gmoe_ref, wrt_ref, br_ref,
                x1_ref, h2_ref, pt_ref):
    y_b = _dot(cv_ref[...], wco_ref[...])
    y_a = _dot(a_ref[...], wf_ref[...])
    z = ga_ref[...].astype(F32) * y_a + gb_ref[...].astype(F32) * y_b
    mix = _dot(z.astype(BF16), wo_ref[...])
    gate_m = mod_ref[0, 2:3, :]
    x1 = x_ref[...] + gate_m * mix
    x1_ref[...] = x1

    h2 = _rms_modulate(x1, gmoe_ref[...], mod_ref[0, 3:4, :], mod_ref[0, 4:5, :])
    h2_ref[...] = h2.astype(BF16)
    logits = lax.dot_general(wrt_ref[...], h2, (((1,), (1,)), ((), ())),
                             preferred_element_type=F32, precision=HIGHEST) + br_ref[...]
    m = jnp.max(logits, axis=0, keepdims=True)
    e = jnp.exp(logits - m)
    probs_t = e / jnp.sum(e, axis=0, keepdims=True)
    pt_ref[0] = probs_t


def _mix_call(a2, cv, ga, gb, x2, mod3, wf, wco, wo, g_moe, wr_t, b_r, *, seq):
    t, d = x2.shape
    n_exp = wr_t.shape[0]
    tm = MIX_ROW_TILE
    tpb = seq // tm
    bsz = t // seq
    row = lambda i: (i, 0)
    const = lambda i: (0, 0)
    return pl.pallas_call(
        _mix_kernel,
        grid=(t // tm,),
        in_specs=[pl.BlockSpec((tm, a2.shape[1]), row),
                  pl.BlockSpec((tm, cv.shape[1]), row),
                  pl.BlockSpec((tm, d), row), pl.BlockSpec((tm, d), row),
                  pl.BlockSpec((tm, d), row),
                  pl.BlockSpec((1, N_MOD, d), lambda i: (i // tpb, 0, 0)),
                  _resident(wf.shape, const), _resident(wco.shape, const),
                  _resident(wo.shape, const),
                  pl.BlockSpec((1, d), const),
                  pl.BlockSpec(wr_t.shape, const), pl.BlockSpec(b_r.shape, const)],
        out_specs=[pl.BlockSpec((tm, d), row), pl.BlockSpec((tm, d), row),
                   pl.BlockSpec((1, n_exp, tm), lambda i: (i // tpb, 0, i % tpb))],
        out_shape=[jax.ShapeDtypeStruct((t, d), F32), jax.ShapeDtypeStruct((t, d), BF16),
                   jax.ShapeDtypeStruct((bsz, n_exp, seq), F32)],
        compiler_params=_params(("parallel",)),
        name="mix",
    )(a2, cv, ga, gb, x2, mod3, wf, wco, wo, g_moe, wr_t, b_r)


def _excl_cumsum(mask_f, upper, lower_strict):
    r, rows, lanes = mask_f.shape
    m2 = mask_f.reshape(r * rows, lanes).astype(BF16)
    incl = _dot(m2, upper)
    tot = jnp.broadcast_to(incl[:, lanes - 1:lanes], incl.shape).astype(BF16)
    tot3 = tot.reshape(r, rows, lanes)
    offs = [_dot(lower_strict, tot3[j]) for j in range(r)]
    off = jnp.stack(offs, axis=0)
    return incl.reshape(r, rows, lanes) - mask_f + off


def _select_kernel(p_ref, upper_ref, lower_ref, pos_ref, raw_ref, *, cap):
    p = p_ref[...]
    r = p.shape[0]

    def count(mask):
        c = jnp.sum(mask.astype(F32), axis=2, keepdims=True)
        return jnp.sum(c, axis=1, keepdims=True)

    prefix = jnp.zeros(p.shape, I32)
    for bit in range(29, -1, -1):
        cand = prefix | (1 << bit)
        keep = count(p >= pltpu.bitcast(cand, F32)) >= cap
        prefix = jnp.where(keep, cand, prefix)
    thr = pltpu.bitcast(prefix, F32)
    gt = p > thr
    eq = (p == thr).astype(F32)
    need = cap - count(gt)
    upper = upper_ref[...]
    lower = lower_ref[...]
    rank_eq = _excl_cumsum(eq, upper, lower)
    sel = jnp.where(gt, 1.0, jnp.where(rank_eq < need, eq, 0.0))
    raw = _excl_cumsum(sel, upper, lower)
    raw_i = raw.astype(I32)
    raw_ref[...] = raw_i
    pos_ref[...] = jnp.where(sel > 0.0, raw_i, -1)


def _select_call(p3, cap):
    r, rows, lanes = p3.shape
    upper = jnp.asarray(np.triu(np.ones((lanes, lanes))), BF16)
    lower = jnp.asarray(np.tril(np.ones((rows, rows)), -1), BF16)
    full = lambda shape: pl.BlockSpec(shape, lambda i: (0,) * len(shape))
    sds = jax.ShapeDtypeStruct(p3.shape, I32)
    return pl.pallas_call(
        functools.partial(_select_kernel, cap=cap),
        grid=(1,),
        in_specs=[full(p3.shape), full(upper.shape), full(lower.shape)],
        out_specs=[full(p3.shape), full(p3.shape)],
        out_shape=[sds, sds],
        compiler_params=_params(("arbitrary",)),
        name="select",
    )(p3, upper, lower)


def _gather_kernel(pfx_ref, h2_ref, pos_ref, prob_ref, xe_ref, val_ref, *, n_exp, nb, cap):
    b = pl.program_id(0)
    g = pl.program_id(1)
    tb = TOKEN_BLOCK
    ch = GATHER_CHUNK
    halves = tb // V7X_LANES
    sub = V7X_SUBLANES_BF16

    @pl.when(g == 0)
    def _():
        xe_ref[...] = jnp.zeros_like(xe_ref)
        val_ref[...] = jnp.zeros_like(val_ref)

    chunk_rows = lax.broadcasted_iota(I32, (ch, V7X_LANES), 0)
    expert_lane = lax.broadcasted_iota(I32, (1, n_exp), 1)

    def window(e, k):
        base = (b * n_exp + e) * (nb + 1) + k
        p0 = pfx_ref[base]
        start = pl.multiple_of((p0 // sub) * sub, sub)
        return start, pfx_ref[base + 1] - start

    def one_hot(pos_rows, prob_rows, ws):
        rows = chunk_rows + ws
        hits = [jnp.where(p == rows, 1.0, 0.0) for p in pos_rows]
        onehot = jnp.concatenate([h.astype(BF16) for h in hits], axis=1)
        v = hits[0] * prob_rows[0]
        for h, p in zip(hits[1:], prob_rows[1:]):
            v = v + h * p
        return onehot, jnp.sum(v, axis=1, keepdims=True)

    for j in range(GATHER_BLOCKS):
        k = g * GATHER_BLOCKS + j
        hk = h2_ref[0, j * tb:(j + 1) * tb, :]
        tok_rows = [slice(halves * j + h, halves * j + h + 1) for h in range(halves)]
        starts, pieces = [], []
        overflow = None
        for e in range(n_exp):
            ws, need = window(e, k)
            onehot, v = one_hot([pos_ref[e, r, :] for r in tok_rows],
                                [prob_ref[e, r, :] for r in tok_rows], ws)
            val_ref[0, pl.ds(ws, ch), e:e + 1] += v
            starts.append(ws)
            pieces.append(onehot)
            over = need > ch
            overflow = over if overflow is None else jnp.logical_or(overflow, over)
        gathered = _dot(jnp.concatenate(pieces, axis=0), hk)
        for e in range(n_exp):
            xe_ref[0, e, pl.ds(starts[e], ch), :] += gathered[e * ch:(e + 1) * ch, :].astype(BF16)

        @pl.when(overflow)
        def _():
            def per_expert(e, carry):
                ws, need = window(e, k)

                def chunk(c, carry2):
                    wsc = pl.multiple_of(ws + c * ch, sub)
                    onehot, v = one_hot([pos_ref[pl.ds(e, 1), r, :][0] for r in tok_rows],
                                        [prob_ref[pl.ds(e, 1), r, :][0] for r in tok_rows], wsc)
                    xe_ref[0, e, pl.ds(wsc, ch), :] += _dot(onehot, hk).astype(BF16)
                    val_ref[0, pl.ds(wsc, ch), :] += jnp.where(expert_lane == e, v, 0.0)
                    return carry2

                lax.fori_loop(1, (need + ch - 1) // ch, chunk, 0)
                return carry

            lax.fori_loop(0, n_exp, per_expert, 0)


def _gather_call(pfx, h2_3, pos3, prob3, *, n_exp, cap):
    bsz, seq, d = h2_3.shape
    nb = seq // TOKEN_BLOCK
    gb = GATHER_BLOCKS
    rows_per_step = gb * TOKEN_BLOCK // V7X_LANES
    slots = cap + GATHER_CHUNK
    grid_spec = pltpu.PrefetchScalarGridSpec(
        num_scalar_prefetch=1,
        grid=(bsz, nb // gb),
        in_specs=[pl.BlockSpec((1, gb * TOKEN_BLOCK, d), lambda b, g, p: (b, g, 0)),
                  pl.BlockSpec((n_exp, rows_per_step, V7X_LANES), lambda b, g, p: (b, g, 0)),
                  pl.BlockSpec((n_exp, rows_per_step, V7X_LANES), lambda b, g, p: (b, g, 0))],
        out_specs=[_resident((1, n_exp, slots, d), lambda b, g, p: (b, 0, 0, 0)),
                   pl.BlockSpec((1, slots, n_exp), lambda b, g, p: (b, 0, 0))],
    )
    return pl.pallas_call(
        functools.partial(_gather_kernel, n_exp=n_exp, nb=nb, cap=cap),
        grid_spec=grid_spec,
        out_shape=[jax.ShapeDtypeStruct((bsz, n_exp, slots, d), BF16),
                   jax.ShapeDtypeStruct((bsz, slots, n_exp), F32)],
        compiler_params=_params(("arbitrary", "arbitrary")),
        name="gather",
    )(pfx, h2_3, pos3, prob3)


def _ffn_kernel(xe_ref, val_ref, wg_ref, wu_ref, wd_ref, ye_ref, acc_ref):
    e = pl.program_id(0)
    f = pl.program_id(1)
    bsz, _, cap, d = xe_ref.shape

    @pl.when(f == 0)
    def _():
        acc_ref[...] = jnp.zeros_like(acc_ref)

    xe = xe_ref[...].reshape(bsz * cap, d)
    a = _dot(xe, wg_ref[0].astype(BF16))
    u = _dot(xe, wu_ref[0].astype(BF16))
    hm = (a * jax.nn.sigmoid(a) * u).astype(BF16)
    acc_ref[...] += _dot(hm, wd_ref[0].astype(BF16))

    @pl.when(f == pl.num_programs(1) - 1)
    def _():
        vals = val_ref[...].reshape(bsz * cap, val_ref.shape[2])
        mine = lax.broadcasted_iota(I32, (1, vals.shape[1]), 1) == e
        val = jnp.sum(jnp.where(mine, vals, 0.0), axis=1, keepdims=True)
        ye_ref[...] = (acc_ref[...] * val).astype(BF16).reshape(bsz, 1, cap, d)


def _ffn_call(xe, val, w_gate, w_up, w_down, *, cap):
    bsz, n_exp, _, d = xe.shape
    d_exp = w_gate.shape[2]
    fc = FFN_CHUNK
    return pl.pallas_call(
        _ffn_kernel,
        grid=(n_exp, d_exp // fc),
        in_specs=[pl.BlockSpec((bsz, 1, cap, d), lambda e, f: (0, e, 0, 0)),
                  pl.BlockSpec((bsz, cap, n_exp), lambda e, f: (0, 0, 0)),
                  pl.BlockSpec((1, d, fc), lambda e, f: (e, 0, f)),
                  pl.BlockSpec((1, d, fc), lambda e, f: (e, 0, f)),
                  pl.BlockSpec((1, fc, d), lambda e, f: (e, f, 0))],
        out_specs=pl.BlockSpec((bsz, 1, cap, d), lambda e, f: (0, e, 0, 0)),
        out_shape=jax.ShapeDtypeStruct((bsz, n_exp, cap, d), BF16),
        scratch_shapes=[pltpu.VMEM((bsz * cap, d), F32)],
        compiler_params=_params(("arbitrary", "arbitrary")),
        name="ffn",
    )(xe, val, w_gate, w_up, w_down)


def _combine_kernel(pfx_ref, x1_ref, ye_ref, post_ref, mod_ref, gfin_ref, o_ref,
                    ycat_ref, scat_ref, acc_ref, *, n_exp, nb, cap):
    b = pl.program_id(0)
    tb = TOKEN_BLOCK
    win = COMBINE_WINDOW
    align = V7X_SUBLANES_BF16
    gate_f = mod_ref[0, 5:6, :]
    lane = lax.broadcasted_iota(I32, (1, V7X_LANES), 1)
    per = V7X_LANES // win

    def finish(rows, moe):
        x2 = x1_ref[rows, :] + gate_f * moe
        ms = jnp.mean(x2 * x2, axis=-1, keepdims=True)
        o_ref[rows, :] = x2 * lax.rsqrt(ms + RMS_EPS) * gfin_ref[...]

    redo = []
    for s in range(COMBINE_BLOCKS):
        k = pl.program_id(1) * COMBINE_BLOCKS + s
        rows = slice(s * tb, (s + 1) * tb)
        p0s, ends, wss = [], [], []
        fits = None
        for e in range(n_exp):
            base = (b * n_exp + e) * (nb + 1) + k
            p0 = pfx_ref[base]
            end = pfx_ref[base + 1]
            ws = pl.multiple_of(jnp.minimum((p0 // align) * align, cap - win), align)
            ok = end <= ws + win
            fits = ok if fits is None else jnp.logical_and(fits, ok)
            p0s.append(p0)
            ends.append(end)
            wss.append(ws)
        for e0 in range(0, n_exp, per):
            target = lane + wss[e0]
            pos_col = post_ref[rows, e0:e0 + 1]
            for j in range(1, per):
                mine = lane >= j * win
                target = jnp.where(mine, lane + (wss[e0 + j] - j * win), target)
                pos_col = jnp.where(mine, post_ref[rows, e0 + j:e0 + j + 1], pos_col)
            g = e0 // per
            scat_ref[s, :, g * V7X_LANES:(g + 1) * V7X_LANES] = (
                jnp.where(pos_col == target, 1.0, 0.0).astype(BF16))
            for j in range(per):
                e = e0 + j
                ycat_ref[s, e * win:(e + 1) * win, :] = ye_ref[0, e, pl.ds(wss[e], win), :]
        finish(rows, _dot(scat_ref[s], ycat_ref[s]))
        redo.append((rows, jnp.logical_not(fits), p0s, ends))

    for rows, misfit, p0s, ends in redo:
        @pl.when(misfit)
        def _():
            acc_ref[...] = jnp.zeros_like(acc_ref)
            cols = lax.broadcasted_iota(I32, (tb, tb), 1)
            for e in range(n_exp):
                ws1 = pl.multiple_of(jnp.minimum((p0s[e] // align) * align, cap - tb), align)
                ws2 = pl.multiple_of(jnp.minimum(ws1 + tb, cap - tb), align)
                pos_col = post_ref[rows, e:e + 1]

                @pl.when(ends[e] > p0s[e])
                def _():
                    onehot = jnp.where(pos_col == cols + ws1, 1.0, 0.0).astype(BF16)
                    acc_ref[...] += _dot(onehot, ye_ref[0, e, pl.ds(ws1, tb), :])

                @pl.when(ends[e] > ws1 + tb)
                def _():
                    c2 = cols + ws2
                    hit = jnp.logical_and(pos_col == c2, c2 >= ws1 + tb)
                    onehot = jnp.where(hit, 1.0, 0.0).astype(BF16)
                    acc_ref[...] += _dot(onehot, ye_ref[0, e, pl.ds(ws2, tb), :])
            finish(rows, acc_ref[...])


def _combine_call(pfx, x1, ye, pos_t, mod3, g_final, *, seq):
    t, d = x1.shape
    bsz, n_exp, cap, _ = ye.shape
    tb = TOKEN_BLOCK
    nb = seq // tb
    step_rows = COMBINE_BLOCKS * tb
    steps = nb // COMBINE_BLOCKS
    row = lambda b, k, p: (b * steps + k, 0)
    grid_spec = pltpu.PrefetchScalarGridSpec(
        num_scalar_prefetch=1,
        grid=(bsz, steps),
        in_specs=[pl.BlockSpec((step_rows, d), row),
                  _resident((1, n_exp, cap, d), lambda b, k, p: (b, 0, 0, 0)),
                  pl.BlockSpec((step_rows, n_exp), row),
                  pl.BlockSpec((1, N_MOD, d), lambda b, k, p: (b, 0, 0)),
                  pl.BlockSpec((1, d), lambda b, k, p: (0, 0))],
        out_specs=pl.BlockSpec((step_rows, d), row),
        scratch_shapes=[pltpu.VMEM((COMBINE_BLOCKS, n_exp * COMBINE_WINDOW, d), BF16),
                        pltpu.VMEM((COMBINE_BLOCKS, tb, n_exp * COMBINE_WINDOW), BF16),
                        pltpu.VMEM((tb, d), F32)],
    )
    return pl.pallas_call(
        functools.partial(_combine_kernel, n_exp=n_exp, nb=nb, cap=cap),
        grid_spec=grid_spec,
        out_shape=jax.ShapeDtypeStruct((t, d), F32),
        compiler_params=_params(("arbitrary", "arbitrary")),
        name="combine",
    )(pfx, x1, ye, pos_t, mod3, g_final)


def _layer(x2, c_act_in, w_ada, b_ada, g_norm_mix, w_in, b_gate, w_fourier, w_conv, w_conv_out,
           w_o, g_norm_moe, w_router, b_router, w_gate_e, w_up_e, w_down_e, *, bsz, seq):
    t, d = x2.shape
    n_exp = w_router.shape[1]
    cap = EC_CAPACITY * seq // n_exp
    n1 = FFT_N1
    n2 = seq // n1
    fc, f1, f2, cos_t, sin_t = _dft_tables(seq)

    mod = _mod_call(c_act_in, w_ada, b_ada.reshape(1, -1))
    mod3 = mod.reshape(bsz, N_MOD, d)

    wre, wim, cv, ga, gb = _proj_call(
        x2, mod3, g_norm_mix.reshape(1, d), w_in.astype(BF16), b_gate.reshape(1, -1), fc, w_conv,
        seq=seq)

    tr, ti = _fft1_call(wre, wim, f1, cos_t, sin_t)
    a2 = _fft2_call(tr, ti, f2).reshape(t, D_FOURIER)

    x1, h2, probs_t = _mix_call(
        a2, cv, ga, gb, x2, mod3, w_fourier.astype(BF16), w_conv_out.astype(BF16),
        w_o.astype(BF16), g_norm_moe.reshape(1, d), w_router.T, b_router.reshape(n_exp, 1),
        seq=seq)

    prob3 = probs_t.reshape(bsz * n_exp, seq // V7X_LANES, V7X_LANES)
    pos, raw = _select_call(prob3, cap)
    stride = TOKEN_BLOCK // V7X_LANES
    starts = raw[:, ::stride, 0]
    pfx = jnp.concatenate([starts, jnp.full((bsz * n_exp, 1), cap, I32)], axis=1).reshape(-1)
    pos_t = jnp.transpose(pos.reshape(bsz, n_exp, seq), (0, 2, 1)).reshape(t, n_exp)

    xe, val = _gather_call(pfx, h2.reshape(bsz, seq, d), pos, prob3, n_exp=n_exp, cap=cap)
    ye = _ffn_call(xe, val, w_gate_e, w_up_e, w_down_e, cap=cap)
    return ye, pfx, x1, pos_t, mod3


def kernel(x, c, w_ada, b_ada, g_norm_mix, w_in, b_gate, w_fourier, w_conv, w_conv_out, w_o,
           g_norm_moe, w_router, b_router, w_gate_e, w_up_e, w_down_e, g_final):
    bsz, seq, d = x.shape
    assert w_ada.shape[0] == 1
    assert seq % (FFT_N1 * V7X_SUBLANES_F32) == 0 and seq % ROW_TILE == 0
    assert (EC_CAPACITY * seq // w_router.shape[2]) >= TOKEN_BLOCK
    c_pad = c.T
    x2 = x.reshape(bsz * seq, d)
    ye, pfx, x1, pos_t, mod3 = _layer(
        x2, c_pad, w_ada[0], b_ada[0], g_norm_mix[0], w_in[0], b_gate[0], w_fourier[0],
        w_conv[0], w_conv_out[0], w_o[0], g_norm_moe[0], w_router[0], b_router[0],
        w_gate_e[0], w_up_e[0], w_down_e[0], bsz=bsz, seq=seq)
    out = _combine_call(pfx, x1, ye, pos_t, mod3, g_final.reshape(1, d), seq=seq)
    return out.reshape(bsz, seq, d)
```

```python
import functools

import numpy as np
import jax
import jax.numpy as jnp
from jax import lax
from jax.experimental import pallas as pl
from jax.experimental.pallas import tpu as pltpu

F32 = jnp.float32
BF16 = jnp.bfloat16
I32 = jnp.int32
HIGHEST = lax.Precision.HIGHEST

FOURIER_GROUPS = 4
FOURIER_GROUP_DIM = 128
D_FOURIER = FOURIER_GROUPS * FOURIER_GROUP_DIM
N_MOD = 6
EC_CAPACITY = 2
RMS_EPS = 1e-6

V7X_LANES = 128
V7X_SUBLANES_F32 = 8
V7X_SUBLANES_BF16 = 16
V7X_VMEM_BYTES = 64 * 1024 * 1024
VMEM_LIMIT_BYTES = V7X_VMEM_BYTES - 6 * 1024 * 1024

MOD_COL_TILE = 1536
MOD_ACC_COLS = 512
ROW_TILE = 1024
MIX_ROW_TILE = 1024
FFT_N1 = 128
FFT_N2_TILE = 16
FFT2_K_TILE = 16
TOKEN_BLOCK = 256
GATHER_CHUNK = 64
GATHER_BLOCKS = 4
COMBINE_WINDOW = 64
COMBINE_BLOCKS = 2
FFN_CHUNK = 512


def _dot(a, b):
    return jnp.dot(a, b, preferred_element_type=F32)


def _params(semantics):
    return pltpu.CompilerParams(dimension_semantics=semantics,
                                vmem_limit_bytes=VMEM_LIMIT_BYTES)


def _pitch(rows):
    p = -(-rows // V7X_SUBLANES_F32)
    return (p if p % 2 else p + 1) * V7X_SUBLANES_F32


def _stage_group(stage_ref, group, value):
    rows = value.shape[0]
    row0 = group * _pitch(rows)
    for l in range(stage_ref.shape[0]):
        stage_ref[l, row0:row0 + rows, :] = value[:, l * V7X_LANES:(l + 1) * V7X_LANES]


def _row_of_each_group(stage_ref, row, groups, rows):
    return jnp.concatenate([stage_ref[l, pl.ds(row, groups, stride=_pitch(rows)), :]
                            for l in range(stage_ref.shape[0])], axis=1)


def _stage_scratch(groups, rows, width):
    return pltpu.VMEM((width // V7X_LANES, groups * _pitch(rows), V7X_LANES), F32)


def _resident(block_shape, index_map):
    return pl.BlockSpec(block_shape, index_map, pipeline_mode=pl.Buffered(1))


def _mod_kernel(ct_ref, w_ref, b_ref, o_ref, *, bsz):
    ct = ct_ref[...]
    ct = ct * jax.nn.sigmoid(ct)
    sub = V7X_SUBLANES_F32
    d, tn = w_ref.shape
    cw = MOD_ACC_COLS
    for c0 in range(0, tn, cw):
        accs = [jnp.zeros((sub, cw), F32) for _ in range(bsz)]
        for g in range(d // sub):
            w = w_ref[g * sub:(g + 1) * sub, c0:c0 + cw]
            for r in range(bsz):
                accs[r] = accs[r] + ct[g * sub:(g + 1) * sub, r:r + 1] * w
        for r in range(bsz):
            o_ref[r:r + 1, c0:c0 + cw] = (jnp.sum(accs[r], axis=0, keepdims=True)
                                          + b_ref[:, c0:c0 + cw])


def _mod_call(c_t, w_ada, b_ada):
    d, bsz = c_t.shape
    n = w_ada.shape[1]
    tn = MOD_COL_TILE
    return pl.pallas_call(
        functools.partial(_mod_kernel, bsz=bsz),
        grid=(n // tn,),
        in_specs=[pl.BlockSpec((d, bsz), lambda j: (0, 0)),
                  pl.BlockSpec((d, tn), lambda j: (0, j)),
                  pl.BlockSpec((1, tn), lambda j: (0, j))],
        out_specs=pl.BlockSpec((bsz, tn), lambda j: (0, j)),
        out_shape=jax.ShapeDtypeStruct((bsz, n), F32),
        compiler_params=_params(("arbitrary",)),
        name="mod",
    )(c_t, w_ada, b_ada)


def _rms_modulate(x, g, shift, scale):
    ms = jnp.mean(x * x, axis=-1, keepdims=True)
    y = x * lax.rsqrt(ms + RMS_EPS) * g
    return y * (1.0 + scale) + shift


def _proj_kernel(x_ref, xprev_ref, xnext_ref, mod_ref, g_ref, win_ref, bgate_ref, fc_ref, wconv_ref,
                 wre_ref, wim_ref, cv_ref, ga_ref, gb_ref, re_s, im_s,
                 *, d, d_conv, n2, tm, tpb):
    i = pl.program_id(0)
    shift = mod_ref[0, 0:1, :]
    scale = mod_ref[0, 1:2, :]
    h = _rms_modulate(x_ref[...], g_ref[...], shift, scale).astype(BF16)
    x_halo = jnp.concatenate([xprev_ref[...], xnext_ref[...]], axis=0)
    h_halo = _rms_modulate(x_halo, g_ref[...], shift, scale).astype(BF16)
    h_ext = jnp.concatenate([h, h_halo], axis=0)

    o1 = D_FOURIER
    o2 = o1 + d_conv
    o3 = o2 + d_conv
    o4 = o3 + d_conv
    uf = _dot(h, win_ref[:, 0:o1]).astype(BF16)
    gd = FOURIER_GROUP_DIM
    ws = [_dot(uf[:, g * gd:(g + 1) * gd], fc_ref[...]) for g in range(FOURIER_GROUPS)]
    w_re = jnp.concatenate([w[:, :gd] for w in ws], axis=1)
    w_im = jnp.concatenate([w[:, gd:] for w in ws], axis=1)
    n1_rows = w_re.shape[0] // n2
    for r in range(n1_rows):
        _stage_group(re_s, r, w_re[r * n2:(r + 1) * n2, :])
        _stage_group(im_s, r, w_im[r * n2:(r + 1) * n2, :])
    for j in range(n2):
        cols = slice(j * D_FOURIER, (j + 1) * D_FOURIER)
        wre_ref[0, :, cols] = _row_of_each_group(re_s, j, n1_rows, n2)
        wim_ref[0, :, cols] = _row_of_each_group(im_s, j, n1_rows, n2)
    q_ext = _dot(h_ext, win_ref[:, o3:o4]) * _dot(h_ext, win_ref[:, o1:o2])
    q = q_ext[0:tm, :]
    nh = xprev_ref.shape[0]
    first = (i % tpb) == 0
    last = (i % tpb) == tpb - 1
    hp = jnp.where(first, 0.0, q_ext[tm + nh - 1:tm + nh, :])
    hn = jnp.where(last, 0.0, q_ext[tm + nh:tm + nh + 1, :])
    rows = lax.broadcasted_iota(I32, (tm, 1), 0)
    q_prev = jnp.where(rows == 0, hp, pltpu.roll(q, 1, axis=0))
    q_next = jnp.where(rows == tm - 1, hn, pltpu.roll(q, tm - 1, axis=0))
    conv = q_prev * wconv_ref[0:1, :] + q * wconv_ref[1:2, :] + q_next * wconv_ref[2:3, :]
    cv_ref[...] = (_dot(h, win_ref[:, o2:o3]) * conv).astype(BF16)
    ga_ref[...] = jax.nn.sigmoid(_dot(h, win_ref[:, o4:o4 + d]) + bgate_ref[:, 0:d]).astype(BF16)
    gb_ref[...] = jax.nn.sigmoid(
        _dot(h, win_ref[:, o4 + d:o4 + 2 * d]) + bgate_ref[:, d:2 * d]).astype(BF16)


def _proj_call(x2, mod3, g_mix, w_in, b_gate, fc, w_conv, *, seq):
    t, d = x2.shape
    k_in = w_in.shape[1]
    d_conv = (k_in - D_FOURIER - 2 * d) // 3
    tm = ROW_TILE
    tpb = seq // tm
    n2 = seq // FFT_N1
    n1_rows = tm // n2
    bsz = t // seq
    sub = V7X_SUBLANES_F32
    row = lambda i: (i, 0)
    const = lambda i: (0, 0)
    out_sds = lambda n: jax.ShapeDtypeStruct((t, n), BF16)
    dft_in = jax.ShapeDtypeStruct((bsz, FFT_N1, n2 * D_FOURIER), F32)
    dft_blk = pl.BlockSpec((1, n1_rows, n2 * D_FOURIER), lambda i: (i // tpb, i % tpb, 0))
    return pl.pallas_call(
        functools.partial(_proj_kernel, d=d, d_conv=d_conv, n2=n2, tm=tm, tpb=tpb),
        grid=(t // tm,),
        in_specs=[pl.BlockSpec((tm, d), row),
                  pl.BlockSpec((sub, d), lambda i: (jnp.maximum(i * (tm // sub) - 1, 0), 0)),
                  pl.BlockSpec((sub, d),
                               lambda i: (jnp.minimum((i + 1) * (tm // sub), t // sub - 1), 0)),
                  pl.BlockSpec((1, N_MOD, d), lambda i: (i // tpb, 0, 0)),
                  pl.BlockSpec((1, d), const),
                  _resident((d, k_in), const),
                  pl.BlockSpec((1, 2 * d), const),
                  pl.BlockSpec(fc.shape, const),
                  pl.BlockSpec(w_conv.shape, const)],
        out_specs=[dft_blk, dft_blk, pl.BlockSpec((tm, d_conv), row),
                   pl.BlockSpec((tm, d), row), pl.BlockSpec((tm, d), row)],
        out_shape=[dft_in, dft_in, out_sds(d_conv), out_sds(d), out_sds(d)],
        scratch_shapes=[_stage_scratch(n1_rows, n2, D_FOURIER)] * 2,
        compiler_params=_params(("parallel",)),
        name="proj",
    )(x2, x2, x2, mod3, g_mix, w_in, b_gate, fc, w_conv)


def _fft1_kernel(wre_ref, wim_ref, f1_ref, cos_ref, sin_ref, tr_ref, ti_ref, re_s, im_s,
                 *, n1, nt):
    reps = D_FOURIER // V7X_LANES
    half = nt // 2
    for h in range(2):
        hcols = slice(h * half * D_FOURIER, (h + 1) * half * D_FOURIER)
        w = jnp.concatenate([wre_ref[0, :, hcols], wim_ref[0, :, hcols]], axis=0).astype(BF16)
        t = _dot(f1_ref[...], w)
        for jj in range(half):
            j = h * half + jj
            cols = slice(jj * D_FOURIER, (jj + 1) * D_FOURIER)
            lanes = slice(j * V7X_LANES, (j + 1) * V7X_LANES)
            c = jnp.concatenate([cos_ref[:, lanes]] * reps, axis=1)
            s = jnp.concatenate([sin_ref[:, lanes]] * reps, axis=1)
            a = t[:n1, cols]
            b = t[n1:, cols]
            _stage_group(re_s, j, a * c + b * s)
            _stage_group(im_s, j, b * c - a * s)
    for k in range(n1):
        tr_ref[0, k] = _row_of_each_group(re_s, k, nt, n1).astype(BF16)
        ti_ref[0, k] = _row_of_each_group(im_s, k, nt, n1).astype(BF16)


def _fft1_call(wre3, wim3, f1, cos_t, sin_t):
    bsz, n1, cols = wre3.shape
    n2 = cols // D_FOURIER
    nt = FFT_N2_TILE
    tn = nt * D_FOURIER
    blk = lambda b, j: (b, 0, j)
    oblk = lambda b, j: (b, 0, j, 0)
    sds = jax.ShapeDtypeStruct((bsz, n1, n2, D_FOURIER), BF16)
    return pl.pallas_call(
        functools.partial(_fft1_kernel, n1=n1, nt=nt),
        grid=(bsz, n2 // nt),
        in_specs=[pl.BlockSpec((1, n1, tn), blk), pl.BlockSpec((1, n1, tn), blk),
                  pl.BlockSpec(f1.shape, lambda b, j: (0, 0)),
                  pl.BlockSpec((n1, nt * V7X_LANES), lambda b, j: (0, j)),
                  pl.BlockSpec((n1, nt * V7X_LANES), lambda b, j: (0, j))],
        out_specs=[pl.BlockSpec((1, n1, nt, D_FOURIER), oblk),
                   pl.BlockSpec((1, n1, nt, D_FOURIER), oblk)],
        out_shape=[sds, sds],
        scratch_shapes=[_stage_scratch(nt, n1, D_FOURIER)] * 2,
        compiler_params=_params(("parallel", "parallel")),
        name="fft1",
    )(wre3, wim3, f1, cos_t, sin_t)


def _fft2_kernel(tr_ref, ti_ref, f2_ref, o_ref, res_s, *, kb, n2):
    for k in range(kb):
        rhs = jnp.concatenate([tr_ref[0, k], ti_ref[0, k]], axis=0)
        _stage_group(res_s, k, _dot(f2_ref[...], rhs))
    for k2 in range(n2):
        o_ref[0, k2] = _row_of_each_group(res_s, k2, kb, n2).astype(BF16)


def _fft2_call(tr4, ti4, f2):
    bsz, n1, n2, dfo = tr4.shape
    kb = FFT2_K_TILE
    blk = lambda b, j: (b, j, 0, 0)
    return pl.pallas_call(
        functools.partial(_fft2_kernel, kb=kb, n2=n2),
        grid=(bsz, n1 // kb),
        in_specs=[pl.BlockSpec((1, kb, n2, dfo), blk), pl.BlockSpec((1, kb, n2, dfo), blk),
                  pl.BlockSpec(f2.shape, lambda b, j: (0, 0))],
        out_specs=pl.BlockSpec((1, n2, kb, dfo), lambda b, j: (b, 0, j, 0)),
        out_shape=jax.ShapeDtypeStruct((bsz, n2, n1, dfo), BF16),
        scratch_shapes=[_stage_scratch(kb, n2, dfo)],
        compiler_params=_params(("parallel", "parallel")),
        name="fft2",
    )(tr4, ti4, f2)


def _dft_tables(seq):
    n1 = FFT_N1
    n2 = seq // n1
    gd = FOURIER_GROUP_DIM
    total_scale = 1.0 / np.sqrt(float(seq) * gd)
    s_c = 2.0 ** -4
    s_1 = 2.0 ** -3
    s_2 = total_scale / (s_c * s_1)

    def cs(n):
        ang = 2.0 * np.pi * np.outer(np.arange(n), np.arange(n)) / n
        return np.cos(ang), np.sin(ang)

    cc, sc = cs(gd)
    fc = np.concatenate([cc, -sc], axis=1) * s_c
    c1, s1 = cs(n1)
    f1 = np.block([[c1, s1], [-s1, c1]]) * s_1
    c2, s2 = cs(n2)
    f2 = np.concatenate([c2, s2], axis=1) * s_2
    ang = 2.0 * np.pi * np.outer(np.arange(n1), np.arange(n2)) / seq
    cos_t = np.repeat(np.cos(ang), V7X_LANES, axis=1)
    sin_t = np.repeat(np.sin(ang), V7X_LANES, axis=1)
    as_bf16 = lambda m: jnp.asarray(m, F32).astype(BF16)
    return (as_bf16(fc), as_bf16(f1), as_bf16(f2),
            jnp.asarray(cos_t, F32), jnp.asarray(sin_t, F32))


def _mix_kernel(a_ref, cv_ref, ga_ref, gb_ref, x_ref, mod_ref,
                wf_ref, wco_ref, wo_ref, gmoe_ref, wrt_ref, br_ref,
                x1_ref, h2_ref, pt_ref):
    y_b = _dot(cv_ref[...], wco_ref[...])
    y_a = _dot(a_ref[...], wf_ref[...])
    z = ga_ref[...].astype(F32) * y_a + gb_ref[...].astype(F32) * y_b
    mix = _dot(z.astype(BF16), wo_ref[...])
    gate_m = mod_ref[0, 2:3, :]
    x1 = x_ref[...] + gate_m * mix
    x1_ref[...] = x1

    h2 = _rms_modulate(x1, gmoe_ref[...], mod_ref[0, 3:4, :], mod_ref[0, 4:5, :])
    h2_hi = h2.astype(BF16)
    h2_ref[...] = h2_hi
    h2_lo = (h2 - h2_hi.astype(F32)).astype(BF16)
    w_r = wrt_ref[...]
    w_hi = w_r.astype(BF16)
    w_lo = (w_r - w_hi.astype(F32)).astype(BF16)
    n_exp = w_r.shape[0]
    rows_t = lambda a, b: lax.dot_general(a, b, (((1,), (1,)), ((), ())),
                                          preferred_element_type=F32)
    both = rows_t(jnp.concatenate([w_hi, w_lo], axis=0), h2_hi)
    logits = both[:n_exp] + both[n_exp:] + rows_t(w_hi, h2_lo) + br_ref[...]
    m = jnp.max(logits, axis=0, keepdims=True)
    e = jnp.exp(logits - m)
    probs_t = e / jnp.sum(e, axis=0, keepdims=True)
    pt_ref[0] = probs_t


def _mix_call(a2, cv, ga, gb, x2, mod3, wf, wco, wo, g_moe, wr_t, b_r, *, seq):
    t, d = x2.shape
    n_exp = wr_t.shape[0]
    tm = MIX_ROW_TILE
    tpb = seq // tm
    bsz = t // seq
    row = lambda i: (i, 0)
    const = lambda i: (0, 0)
    return pl.pallas_call(
        _mix_kernel,
        grid=(t // tm,),
        in_specs=[pl.BlockSpec((tm, a2.shape[1]), row),
                  pl.BlockSpec((tm, cv.shape[1]), row),
                  pl.BlockSpec((tm, d), row), pl.BlockSpec((tm, d), row),
                  pl.BlockSpec((tm, d), row),
                  pl.BlockSpec((1, N_MOD, d), lambda i: (i // tpb, 0, 0)),
                  _resident(wf.shape, const), _resident(wco.shape, const),
                  _resident(wo.shape, const),
                  pl.BlockSpec((1, d), const),
                  pl.BlockSpec(wr_t.shape, const), pl.BlockSpec(b_r.shape, const)],
        out_specs=[pl.BlockSpec((tm, d), row), pl.BlockSpec((tm, d), row),
                   pl.BlockSpec((1, n_exp, tm), lambda i: (i // tpb, 0, i % tpb))],
        out_shape=[jax.ShapeDtypeStruct((t, d), F32), jax.ShapeDtypeStruct((t, d), BF16),
                   jax.ShapeDtypeStruct((bsz, n_exp, seq), F32)],
        compiler_params=_params(("parallel",)),
        name="mix",
    )(a2, cv, ga, gb, x2, mod3, wf, wco, wo, g_moe, wr_t, b_r)


def _excl_cumsum(mask_f, upper, lower_strict):
    r, rows, lanes = mask_f.shape
    m2 = mask_f.reshape(r * rows, lanes).astype(BF16)
    incl = _dot(m2, upper)
    tot = jnp.broadcast_to(incl[:, lanes - 1:lanes], incl.shape).astype(BF16)
    tot3 = tot.reshape(r, rows, lanes)
    offs = [_dot(lower_strict, tot3[j]) for j in range(r)]
    off = jnp.stack(offs, axis=0)
    return incl.reshape(r, rows, lanes) - mask_f + off


def _select_kernel(p_ref, upper_ref, lower_ref, pos_ref, raw_ref, *, cap):
    p = p_ref[...]
    r = p.shape[0]

    def count(mask):
        c = jnp.sum(mask.astype(F32), axis=2, keepdims=True)
        return jnp.sum(c, axis=1, keepdims=True)

    prefix = jnp.zeros(p.shape, I32)
    for bit in range(29, -1, -1):
        cand = prefix | (1 << bit)
        keep = count(p >= pltpu.bitcast(cand, F32)) >= cap
        prefix = jnp.where(keep, cand, prefix)
    thr = pltpu.bitcast(prefix, F32)
    gt = p > thr
    eq = (p == thr).astype(F32)
    need = cap - count(gt)
    upper = upper_ref[...]
    lower = lower_ref[...]
    rank_eq = _excl_cumsum(eq, upper, lower)
    sel = jnp.where(gt, 1.0, jnp.where(rank_eq < need, eq, 0.0))
    raw = _excl_cumsum(sel, upper, lower)
    raw_i = raw.astype(I32)
    raw_ref[...] = raw_i
    pos_ref[...] = jnp.where(sel > 0.0, raw_i, -1)


def _select_call(p3, cap):
    r, rows, lanes = p3.shape
    upper = jnp.asarray(np.triu(np.ones((lanes, lanes))), BF16)
    lower = jnp.asarray(np.tril(np.ones((rows, rows)), -1), BF16)
    full = lambda shape: pl.BlockSpec(shape, lambda i: (0,) * len(shape))
    sds = jax.ShapeDtypeStruct(p3.shape, I32)
    return pl.pallas_call(
        functools.partial(_select_kernel, cap=cap),
        grid=(1,),
        in_specs=[full(p3.shape), full(upper.shape), full(lower.shape)],
        out_specs=[full(p3.shape), full(p3.shape)],
        out_shape=[sds, sds],
        compiler_params=_params(("arbitrary",)),
        name="select",
    )(p3, upper, lower)


def _gather_kernel(pfx_ref, h2_ref, pos_ref, prob_ref, xe_ref, val_ref, *, n_exp, nb, cap):
    b = pl.program_id(0)
    g = pl.program_id(1)
    tb = TOKEN_BLOCK
    ch = GATHER_CHUNK
    halves = tb // V7X_LANES
    sub = V7X_SUBLANES_BF16

    @pl.when(g == 0)
    def _():
        xe_ref[...] = jnp.zeros_like(xe_ref)
        val_ref[...] = jnp.zeros_like(val_ref)

    chunk_rows = lax.broadcasted_iota(I32, (ch, V7X_LANES), 0)
    expert_lane = lax.broadcasted_iota(I32, (1, n_exp), 1)

    def window(e, k):
        base = (b * n_exp + e) * (nb + 1) + k
        p0 = pfx_ref[base]
        start = pl.multiple_of((p0 // sub) * sub, sub)
        return start, pfx_ref[base + 1] - start

    def one_hot(pos_rows, prob_rows, ws):
        rows = chunk_rows + ws
        hits = [jnp.where(p == rows, 1.0, 0.0) for p in pos_rows]
        onehot = jnp.concatenate([h.astype(BF16) for h in hits], axis=1)
        v = hits[0] * prob_rows[0]
        for h, p in zip(hits[1:], prob_rows[1:]):
            v = v + h * p
        return onehot, jnp.sum(v, axis=1, keepdims=True)

    for j in range(GATHER_BLOCKS):
        k = g * GATHER_BLOCKS + j
        hk = h2_ref[0, j * tb:(j + 1) * tb, :]
        tok_rows = [slice(halves * j + h, halves * j + h + 1) for h in range(halves)]
        starts, pieces = [], []
        overflow = None
        for e in range(n_exp):
            ws, need = window(e, k)
            onehot, v = one_hot([pos_ref[e, r, :] for r in tok_rows],
                                [prob_ref[e, r, :] for r in tok_rows], ws)
            val_ref[0, pl.ds(ws, ch), e:e + 1] += v
            starts.append(ws)
            pieces.append(onehot)
            over = need > ch
            overflow = over if overflow is None else jnp.logical_or(overflow, over)
        gathered = _dot(jnp.concatenate(pieces, axis=0), hk)
        for e in range(n_exp):
            xe_ref[0, e, pl.ds(starts[e], ch), :] += gathered[e * ch:(e + 1) * ch, :].astype(BF16)

        @pl.when(overflow)
        def _():
            def per_expert(e, carry):
                ws, need = window(e, k)

                def chunk(c, carry2):
                    wsc = pl.multiple_of(ws + c * ch, sub)
                    onehot, v = one_hot([pos_ref[pl.ds(e, 1), r, :][0] for r in tok_rows],
                                        [prob_ref[pl.ds(e, 1), r, :][0] for r in tok_rows], wsc)
                    xe_ref[0, e, pl.ds(wsc, ch), :] += _dot(onehot, hk).astype(BF16)
                    val_ref[0, pl.ds(wsc, ch), :] += jnp.where(expert_lane == e, v, 0.0)
                    return carry2

                lax.fori_loop(1, (need + ch - 1) // ch, chunk, 0)
                return carry

            lax.fori_loop(0, n_exp, per_expert, 0)


def _gather_call(pfx, h2_3, pos3, prob3, *, n_exp, cap):
    bsz, seq, d = h2_3.shape
    nb = seq // TOKEN_BLOCK
    gb = GATHER_BLOCKS
    rows_per_step = gb * TOKEN_BLOCK // V7X_LANES
    slots = cap + GATHER_CHUNK
    grid_spec = pltpu.PrefetchScalarGridSpec(
        num_scalar_prefetch=1,
        grid=(bsz, nb // gb),
        in_specs=[pl.BlockSpec((1, gb * TOKEN_BLOCK, d), lambda b, g, p: (b, g, 0)),
                  pl.BlockSpec((n_exp, rows_per_step, V7X_LANES), lambda b, g, p: (b, g, 0)),
                  pl.BlockSpec((n_exp, rows_per_step, V7X_LANES), lambda b, g, p: (b, g, 0))],
        out_specs=[_resident((1, n_exp, slots, d), lambda b, g, p: (b, 0, 0, 0)),
                   pl.BlockSpec((1, slots, n_exp), lambda b, g, p: (b, 0, 0))],
    )
    return pl.pallas_call(
        functools.partial(_gather_kernel, n_exp=n_exp, nb=nb, cap=cap),
        grid_spec=grid_spec,
        out_shape=[jax.ShapeDtypeStruct((bsz, n_exp, slots, d), BF16),
                   jax.ShapeDtypeStruct((bsz, slots, n_exp), F32)],
        compiler_params=_params(("arbitrary", "arbitrary")),
        name="gather",
    )(pfx, h2_3, pos3, prob3)


def _ffn_kernel(xe_ref, val_ref, wg_ref, wu_ref, wd_ref, ye_ref, acc_ref):
    e = pl.program_id(0)
    f = pl.program_id(1)
    bsz, _, cap, d = xe_ref.shape

    @pl.when(f == 0)
    def _():
        acc_ref[...] = jnp.zeros_like(acc_ref)

    xe = xe_ref[...].reshape(bsz * cap, d)
    a = _dot(xe, wg_ref[0].astype(BF16))
    u = _dot(xe, wu_ref[0].astype(BF16))
    hm = (a * jax.nn.sigmoid(a) * u).astype(BF16)
    acc_ref[...] += _dot(hm, wd_ref[0].astype(BF16))

    @pl.when(f == pl.num_programs(1) - 1)
    def _():
        vals = val_ref[...].reshape(bsz * cap, val_ref.shape[2])
        mine = lax.broadcasted_iota(I32, (1, vals.shape[1]), 1) == e
        val = jnp.sum(jnp.where(mine, vals, 0.0), axis=1, keepdims=True)
        ye_ref[...] = (acc_ref[...] * val).astype(BF16).reshape(bsz, 1, cap, d)


def _ffn_call(xe, val, w_gate, w_up, w_down, *, cap):
    bsz, n_exp, _, d = xe.shape
    d_exp = w_gate.shape[2]
    fc = FFN_CHUNK
    return pl.pallas_call(
        _ffn_kernel,
        grid=(n_exp, d_exp // fc),
        in_specs=[pl.BlockSpec((bsz, 1, cap, d), lambda e, f: (0, e, 0, 0)),
                  pl.BlockSpec((bsz, cap, n_exp), lambda e, f: (0, 0, 0)),
                  pl.BlockSpec((1, d, fc), lambda e, f: (e, 0, f)),
                  pl.BlockSpec((1, d, fc), lambda e, f: (e, 0, f)),
                  pl.BlockSpec((1, fc, d), lambda e, f: (e, f, 0))],
        out_specs=pl.BlockSpec((bsz, 1, cap, d), lambda e, f: (0, e, 0, 0)),
        out_shape=jax.ShapeDtypeStruct((bsz, n_exp, cap, d), BF16),
        scratch_shapes=[pltpu.VMEM((bsz * cap, d), F32)],
        compiler_params=_params(("arbitrary", "arbitrary")),
        name="ffn",
    )(xe, val, w_gate, w_up, w_down)


def _combine_kernel(pfx_ref, x1_ref, ye_ref, post_ref, mod_ref, gfin_ref, o_ref,
                    ycat_ref, scat_ref, acc_ref, *, n_exp, nb, cap):
    b = pl.program_id(0)
    tb = TOKEN_BLOCK
    win = COMBINE_WINDOW
    align = V7X_SUBLANES_BF16
    gate_f = mod_ref[0, 5:6, :]
    lane = lax.broadcasted_iota(I32, (1, V7X_LANES), 1)
    per = V7X_LANES // win

    def finish(rows, moe):
        x2 = x1_ref[rows, :] + gate_f * moe
        ms = jnp.mean(x2 * x2, axis=-1, keepdims=True)
        o_ref[rows, :] = x2 * lax.rsqrt(ms + RMS_EPS) * gfin_ref[...]

    redo = []
    for s in range(COMBINE_BLOCKS):
        k = pl.program_id(1) * COMBINE_BLOCKS + s
        rows = slice(s * tb, (s + 1) * tb)
        p0s, ends, wss = [], [], []
        fits = None
        for e in range(n_exp):
            base = (b * n_exp + e) * (nb + 1) + k
            p0 = pfx_ref[base]
            end = pfx_ref[base + 1]
            ws = pl.multiple_of(jnp.minimum((p0 // align) * align, cap - win), align)
            ok = end <= ws + win
            fits = ok if fits is None else jnp.logical_and(fits, ok)
            p0s.append(p0)
            ends.append(end)
            wss.append(ws)
        for e0 in range(0, n_exp, per):
            target = lane + wss[e0]
            pos_col = post_ref[rows, e0:e0 + 1]
            for j in range(1, per):
                mine = lane >= j * win
                target = jnp.where(mine, lane + (wss[e0 + j] - j * win), target)
                pos_col = jnp.where(mine, post_ref[rows, e0 + j:e0 + j + 1], pos_col)
            g = e0 // per
            scat_ref[s, :, g * V7X_LANES:(g + 1) * V7X_LANES] = (
                jnp.where(pos_col == target, 1.0, 0.0).astype(BF16))
            for j in range(per):
                e = e0 + j
                ycat_ref[s, e * win:(e + 1) * win, :] = ye_ref[0, e, pl.ds(wss[e], win), :]
        finish(rows, _dot(scat_ref[s], ycat_ref[s]))
        redo.append((rows, jnp.logical_not(fits), p0s, ends))

    for rows, misfit, p0s, ends in redo:
        @pl.when(misfit)
        def _():
            acc_ref[...] = jnp.zeros_like(acc_ref)
            cols = lax.broadcasted_iota(I32, (tb, tb), 1)
            for e in range(n_exp):
                ws1 = pl.multiple_of(jnp.minimum((p0s[e] // align) * align, cap - tb), align)
                ws2 = pl.multiple_of(jnp.minimum(ws1 + tb, cap - tb), align)
                pos_col = post_ref[rows, e:e + 1]

                @pl.when(ends[e] > p0s[e])
                def _():
                    onehot = jnp.where(pos_col == cols + ws1, 1.0, 0.0).astype(BF16)
                    acc_ref[...] += _dot(onehot, ye_ref[0, e, pl.ds(ws1, tb), :])

                @pl.when(ends[e] > ws1 + tb)
                def _():
                    c2 = cols + ws2
                    hit = jnp.logical_and(pos_col == c2, c2 >= ws1 + tb)
                    onehot = jnp.where(hit, 1.0, 0.0).astype(BF16)
                    acc_ref[...] += _dot(onehot, ye_ref[0, e, pl.ds(ws2, tb), :])
            finish(rows, acc_ref[...])


def _combine_call(pfx, x1, ye, pos_t, mod3, g_final, *, seq):
    t, d = x1.shape
    bsz, n_exp, cap, _ = ye.shape
    tb = TOKEN_BLOCK
    nb = seq // tb
    step_rows = COMBINE_BLOCKS * tb
    steps = nb // COMBINE_BLOCKS
    row = lambda b, k, p: (b * steps + k, 0)
    grid_spec = pltpu.PrefetchScalarGridSpec(
        num_scalar_prefetch=1,
        grid=(bsz, steps),
        in_specs=[pl.BlockSpec((step_rows, d), row),
                  _resident((1, n_exp, cap, d), lambda b, k, p: (b, 0, 0, 0)),
                  pl.BlockSpec((step_rows, n_exp), row),
                  pl.BlockSpec((1, N_MOD, d), lambda b, k, p: (b, 0, 0)),
                  pl.BlockSpec((1, d), lambda b, k, p: (0, 0))],
        out_specs=pl.BlockSpec((step_rows, d), row),
        scratch_shapes=[pltpu.VMEM((COMBINE_BLOCKS, n_exp * COMBINE_WINDOW, d), BF16),
                        pltpu.VMEM((COMBINE_BLOCKS, tb, n_exp * COMBINE_WINDOW), BF16),
                        pltpu.VMEM((tb, d), F32)],
    )
    return pl.pallas_call(
        functools.partial(_combine_kernel, n_exp=n_exp, nb=nb, cap=cap),
        grid_spec=grid_spec,
        out_shape=jax.ShapeDtypeStruct((t, d), F32),
        compiler_params=_params(("arbitrary", "arbitrary")),
        name="combine",
    )(pfx, x1, ye, pos_t, mod3, g_final)


def _layer(x2, c_act_in, w_ada, b_ada, g_norm_mix, w_in, b_gate, w_fourier, w_conv, w_conv_out,
           w_o, g_norm_moe, w_router, b_router, w_gate_e, w_up_e, w_down_e, *, bsz, seq):
    t, d = x2.shape
    n_exp = w_router.shape[1]
    cap = EC_CAPACITY * seq // n_exp
    n1 = FFT_N1
    n2 = seq // n1
    fc, f1, f2, cos_t, sin_t = _dft_tables(seq)

    mod = _mod_call(c_act_in, w_ada, b_ada.reshape(1, -1))
    mod3 = mod.reshape(bsz, N_MOD, d)

    wre, wim, cv, ga, gb = _proj_call(
        x2, mod3, g_norm_mix.reshape(1, d), w_in.astype(BF16), b_gate.reshape(1, -1), fc, w_conv,
        seq=seq)

    tr, ti = _fft1_call(wre, wim, f1, cos_t, sin_t)
    a2 = _fft2_call(tr, ti, f2).reshape(t, D_FOURIER)

    x1, h2, probs_t = _mix_call(
        a2, cv, ga, gb, x2, mod3, w_fourier.astype(BF16), w_conv_out.astype(BF16),
        w_o.astype(BF16), g_norm_moe.reshape(1, d), w_router.T, b_router.reshape(n_exp, 1),
        seq=seq)

    prob3 = probs_t.reshape(bsz * n_exp, seq // V7X_LANES, V7X_LANES)
    pos, raw = _select_call(prob3, cap)
    stride = TOKEN_BLOCK // V7X_LANES
    starts = raw[:, ::stride, 0]
    pfx = jnp.concatenate([starts, jnp.full((bsz * n_exp, 1), cap, I32)], axis=1).reshape(-1)
    pos_t = jnp.transpose(pos.reshape(bsz, n_exp, seq), (0, 2, 1)).reshape(t, n_exp)

    xe, val = _gather_call(pfx, h2.reshape(bsz, seq, d), pos, prob3, n_exp=n_exp, cap=cap)
    ye = _ffn_call(xe, val, w_gate_e, w_up_e, w_down_e, cap=cap)
    return ye, pfx, x1, pos_t, mod3


def kernel(x, c, w_ada, b_ada, g_norm_mix, w_in, b_gate, w_fourier, w_conv, w_conv_out, w_o,
           g_norm_moe, w_router, b_router, w_gate_e, w_up_e, w_down_e, g_final):
    bsz, seq, d = x.shape
    assert w_ada.shape[0] == 1
    assert seq % (FFT_N1 * V7X_SUBLANES_F32) == 0 and seq % ROW_TILE == 0
    assert (EC_CAPACITY * seq // w_router.shape[2]) >= TOKEN_BLOCK
    c_pad = c.T
    x2 = x.reshape(bsz * seq, d)
    ye, pfx, x1, pos_t, mod3 = _layer(
        x2, c_pad, w_ada[0], b_ada[0], g_norm_mix[0], w_in[0], b_gate[0], w_fourier[0],
        w_conv[0], w_conv_out[0], w_o[0], g_norm_moe[0], w_router[0], b_router[0],
        w_gate_e[0], w_up_e[0], w_down_e[0], bsz=bsz, seq=seq)
    out = _combine_call(pfx, x1, ye, pos_t, mod3, g_final.reshape(1, d), seq=seq)
    return out.reshape(bsz, seq, d)
```

```python
import functools

import numpy as np
import jax
import jax.numpy as jnp
from jax import lax
from jax.experimental import pallas as pl
from jax.experimental.pallas import tpu as pltpu

F32 = jnp.float32
BF16 = jnp.bfloat16
I32 = jnp.int32
HIGHEST = lax.Precision.HIGHEST

FOURIER_GROUPS = 4
FOURIER_GROUP_DIM = 128
D_FOURIER = FOURIER_GROUPS * FOURIER_GROUP_DIM
N_MOD = 6
EC_CAPACITY = 2
RMS_EPS = 1e-6

V7X_LANES = 128
V7X_SUBLANES_F32 = 8
V7X_SUBLANES_BF16 = 16
V7X_VMEM_BYTES = 64 * 1024 * 1024
VMEM_LIMIT_BYTES = V7X_VMEM_BYTES - 6 * 1024 * 1024

MOD_COL_TILE = 1536
MOD_ACC_COLS = 512
ROW_TILE = 1024
MIX_ROW_TILE = 1024
FFT_N1 = 128
FFT_N2_TILE = 16
FFT2_K_TILE = 16
TOKEN_BLOCK = 256
GATHER_CHUNK = 64
GATHER_BLOCKS = 4
COMBINE_WINDOW = 64
COMBINE_BLOCKS = 2
FFN_CHUNK = 512


def _dot(a, b):
    return jnp.dot(a, b, preferred_element_type=F32)


def _params(semantics):
    return pltpu.CompilerParams(dimension_semantics=semantics,
                                vmem_limit_bytes=VMEM_LIMIT_BYTES)


def _pitch(rows):
    p = -(-rows // V7X_SUBLANES_F32)
    return (p if p % 2 else p + 1) * V7X_SUBLANES_F32


def _stage_group(stage_ref, group, value):
    rows = value.shape[0]
    row0 = group * _pitch(rows)
    for l in range(stage_ref.shape[0]):
        stage_ref[l, row0:row0 + rows, :] = value[:, l * V7X_LANES:(l + 1) * V7X_LANES]


def _row_of_each_group(stage_ref, row, groups, rows):
    return jnp.concatenate([stage_ref[l, pl.ds(row, groups, stride=_pitch(rows)), :]
                            for l in range(stage_ref.shape[0])], axis=1)


def _stage_scratch(groups, rows, width):
    return pltpu.VMEM((width // V7X_LANES, groups * _pitch(rows), V7X_LANES), F32)


def _resident(block_shape, index_map):
    return pl.BlockSpec(block_shape, index_map, pipeline_mode=pl.Buffered(1))


def _mod_kernel(ct_ref, w_ref, b_ref, o_ref, *, bsz):
    ct = ct_ref[...]
    ct = ct * jax.nn.sigmoid(ct)
    sub = V7X_SUBLANES_F32
    d, tn = w_ref.shape
    cw = MOD_ACC_COLS
    for c0 in range(0, tn, cw):
        accs = [jnp.zeros((sub, cw), F32) for _ in range(bsz)]
        for g in range(d // sub):
            w = w_ref[g * sub:(g + 1) * sub, c0:c0 + cw]
            for r in range(bsz):
                accs[r] = accs[r] + ct[g * sub:(g + 1) * sub, r:r + 1] * w
        for r in range(bsz):
            o_ref[r:r + 1, c0:c0 + cw] = (jnp.sum(accs[r], axis=0, keepdims=True)
                                          + b_ref[:, c0:c0 + cw])


def _mod_call(c_t, w_ada, b_ada):
    d, bsz = c_t.shape
    n = w_ada.shape[1]
    tn = MOD_COL_TILE
    return pl.pallas_call(
        functools.partial(_mod_kernel, bsz=bsz),
        grid=(n // tn,),
        in_specs=[pl.BlockSpec((d, bsz), lambda j: (0, 0)),
                  pl.BlockSpec((d, tn), lambda j: (0, j)),
                  pl.BlockSpec((1, tn), lambda j: (0, j))],
        out_specs=pl.BlockSpec((bsz, tn), lambda j: (0, j)),
        out_shape=jax.ShapeDtypeStruct((bsz, n), F32),
        compiler_params=_params(("arbitrary",)),
        name="mod",
    )(c_t, w_ada, b_ada)


def _rms_modulate(x, g, shift, scale):
    ms = jnp.mean(x * x, axis=-1, keepdims=True)
    y = x * lax.rsqrt(ms + RMS_EPS) * g
    return y * (1.0 + scale) + shift


def _proj_kernel(x_ref, xprev_ref, xnext_ref, mod_ref, g_ref, win_ref, bgate_ref, fc_ref, wconv_ref,
                 wre_ref, wim_ref, cv_ref, ga_ref, gb_ref, re_s, im_s,
                 *, d, d_conv, n2, tm, tpb):
    i = pl.program_id(0)
    shift = mod_ref[0, 0:1, :]
    scale = mod_ref[0, 1:2, :]
    h = _rms_modulate(x_ref[...], g_ref[...], shift, scale).astype(BF16)
    x_halo = jnp.concatenate([xprev_ref[...], xnext_ref[...]], axis=0)
    h_halo = _rms_modulate(x_halo, g_ref[...], shift, scale).astype(BF16)
    h_ext = jnp.concatenate([h, h_halo], axis=0)

    o1 = D_FOURIER
    o2 = o1 + d_conv
    o3 = o2 + d_conv
    o4 = o3 + d_conv
    uf = _dot(h, win_ref[:, 0:o1]).astype(BF16)
    gd = FOURIER_GROUP_DIM
    ws = [_dot(uf[:, g * gd:(g + 1) * gd], fc_ref[...]) for g in range(FOURIER_GROUPS)]
    w_re = jnp.concatenate([w[:, :gd] for w in ws], axis=1)
    w_im = jnp.concatenate([w[:, gd:] for w in ws], axis=1)
    n1_rows = w_re.shape[0] // n2
    for r in range(n1_rows):
        _stage_group(re_s, r, w_re[r * n2:(r + 1) * n2, :])
        _stage_group(im_s, r, w_im[r * n2:(r + 1) * n2, :])
    for j in range(n2):
        cols = slice(j * D_FOURIER, (j + 1) * D_FOURIER)
        wre_ref[0, :, cols] = _row_of_each_group(re_s, j, n1_rows, n2).astype(BF16)
        wim_ref[0, :, cols] = _row_of_each_group(im_s, j, n1_rows, n2).astype(BF16)
    q_ext = _dot(h_ext, win_ref[:, o3:o4]) * _dot(h_ext, win_ref[:, o1:o2])
    q = q_ext[0:tm, :]
    nh = xprev_ref.shape[0]
    first = (i % tpb) == 0
    last = (i % tpb) == tpb - 1
    hp = jnp.where(first, 0.0, q_ext[tm + nh - 1:tm + nh, :])
    hn = jnp.where(last, 0.0, q_ext[tm + nh:tm + nh + 1, :])
    rows = lax.broadcasted_iota(I32, (tm, 1), 0)
    q_prev = jnp.where(rows == 0, hp, pltpu.roll(q, 1, axis=0))
    q_next = jnp.where(rows == tm - 1, hn, pltpu.roll(q, tm - 1, axis=0))
    conv = q_prev * wconv_ref[0:1, :] + q * wconv_ref[1:2, :] + q_next * wconv_ref[2:3, :]
    cv_ref[...] = (_dot(h, win_ref[:, o2:o3]) * conv).astype(BF16)
    ga_ref[...] = jax.nn.sigmoid(_dot(h, win_ref[:, o4:o4 + d]) + bgate_ref[:, 0:d]).astype(BF16)
    gb_ref[...] = jax.nn.sigmoid(
        _dot(h, win_ref[:, o4 + d:o4 + 2 * d]) + bgate_ref[:, d:2 * d]).astype(BF16)


def _proj_call(x2, mod3, g_mix, w_in, b_gate, fc, w_conv, *, seq):
    t, d = x2.shape
    k_in = w_in.shape[1]
    d_conv = (k_in - D_FOURIER - 2 * d) // 3
    tm = ROW_TILE
    tpb = seq // tm
    n2 = seq // FFT_N1
    n1_rows = tm // n2
    bsz = t // seq
    sub = V7X_SUBLANES_F32
    row = lambda i: (i, 0)
    const = lambda i: (0, 0)
    out_sds = lambda n: jax.ShapeDtypeStruct((t, n), BF16)
    assert n1_rows % V7X_SUBLANES_BF16 == 0
    dft_in = jax.ShapeDtypeStruct((bsz, FFT_N1, n2 * D_FOURIER), BF16)
    dft_blk = pl.BlockSpec((1, n1_rows, n2 * D_FOURIER), lambda i: (i // tpb, i % tpb, 0))
    return pl.pallas_call(
        functools.partial(_proj_kernel, d=d, d_conv=d_conv, n2=n2, tm=tm, tpb=tpb),
        grid=(t // tm,),
        in_specs=[pl.BlockSpec((tm, d), row),
                  pl.BlockSpec((sub, d), lambda i: (jnp.maximum(i * (tm // sub) - 1, 0), 0)),
                  pl.BlockSpec((sub, d),
                               lambda i: (jnp.minimum((i + 1) * (tm // sub), t // sub - 1), 0)),
                  pl.BlockSpec((1, N_MOD, d), lambda i: (i // tpb, 0, 0)),
                  pl.BlockSpec((1, d), const),
                  _resident((d, k_in), const),
                  pl.BlockSpec((1, 2 * d), const),
                  pl.BlockSpec(fc.shape, const),
                  pl.BlockSpec(w_conv.shape, const)],
        out_specs=[dft_blk, dft_blk, pl.BlockSpec((tm, d_conv), row),
                   pl.BlockSpec((tm, d), row), pl.BlockSpec((tm, d), row)],
        out_shape=[dft_in, dft_in, out_sds(d_conv), out_sds(d), out_sds(d)],
        scratch_shapes=[_stage_scratch(n1_rows, n2, D_FOURIER)] * 2,
        compiler_params=_params(("parallel",)),
        name="proj",
    )(x2, x2, x2, mod3, g_mix, w_in, b_gate, fc, w_conv)


def _fft1_kernel(wre_ref, wim_ref, f1_ref, cos_ref, sin_ref, tr_ref, ti_ref, re_s, im_s,
                 *, n1, nt):
    reps = D_FOURIER // V7X_LANES
    half = nt // 2
    for h in range(2):
        hcols = slice(h * half * D_FOURIER, (h + 1) * half * D_FOURIER)
        w = jnp.concatenate([wre_ref[0, :, hcols], wim_ref[0, :, hcols]], axis=0)
        t = _dot(f1_ref[...], w)
        for jj in range(half):
            j = h * half + jj
            cols = slice(jj * D_FOURIER, (jj + 1) * D_FOURIER)
            lanes = slice(j * V7X_LANES, (j + 1) * V7X_LANES)
            c = jnp.concatenate([cos_ref[:, lanes]] * reps, axis=1)
            s = jnp.concatenate([sin_ref[:, lanes]] * reps, axis=1)
            a = t[:n1, cols]
            b = t[n1:, cols]
            _stage_group(re_s, j, a * c + b * s)
            _stage_group(im_s, j, b * c - a * s)
    for k in range(n1):
        tr_ref[0, k] = _row_of_each_group(re_s, k, nt, n1).astype(BF16)
        ti_ref[0, k] = _row_of_each_group(im_s, k, nt, n1).astype(BF16)


def _fft1_call(wre3, wim3, f1, cos_t, sin_t):
    bsz, n1, cols = wre3.shape
    n2 = cols // D_FOURIER
    nt = FFT_N2_TILE
    tn = nt * D_FOURIER
    blk = lambda b, j: (b, 0, j)
    oblk = lambda b, j: (b, 0, j, 0)
    sds = jax.ShapeDtypeStruct((bsz, n1, n2, D_FOURIER), BF16)
    return pl.pallas_call(
        functools.partial(_fft1_kernel, n1=n1, nt=nt),
        grid=(bsz, n2 // nt),
        in_specs=[pl.BlockSpec((1, n1, tn), blk), pl.BlockSpec((1, n1, tn), blk),
                  pl.BlockSpec(f1.shape, lambda b, j: (0, 0)),
                  pl.BlockSpec((n1, nt * V7X_LANES), lambda b, j: (0, j)),
                  pl.BlockSpec((n1, nt * V7X_LANES), lambda b, j: (0, j))],
        out_specs=[pl.BlockSpec((1, n1, nt, D_FOURIER), oblk),
                   pl.BlockSpec((1, n1, nt, D_FOURIER), oblk)],
        out_shape=[sds, sds],
        scratch_shapes=[_stage_scratch(nt, n1, D_FOURIER)] * 2,
        compiler_params=_params(("parallel", "parallel")),
        name="fft1",
    )(wre3, wim3, f1, cos_t, sin_t)


def _fft2_kernel(tr_ref, ti_ref, f2_ref, o_ref, res_s, *, kb, n2):
    for k in range(kb):
        rhs = jnp.concatenate([tr_ref[0, k], ti_ref[0, k]], axis=0)
        _stage_group(res_s, k, _dot(f2_ref[...], rhs))
    for k2 in range(n2):
        o_ref[0, k2] = _row_of_each_group(res_s, k2, kb, n2).astype(BF16)


def _fft2_call(tr4, ti4, f2):
    bsz, n1, n2, dfo = tr4.shape
    kb = FFT2_K_TILE
    blk = lambda b, j: (b, j, 0, 0)
    return pl.pallas_call(
        functools.partial(_fft2_kernel, kb=kb, n2=n2),
        grid=(bsz, n1 // kb),
        in_specs=[pl.BlockSpec((1, kb, n2, dfo), blk), pl.BlockSpec((1, kb, n2, dfo), blk),
                  pl.BlockSpec(f2.shape, lambda b, j: (0, 0))],
        out_specs=pl.BlockSpec((1, n2, kb, dfo), lambda b, j: (b, 0, j, 0)),
        out_shape=jax.ShapeDtypeStruct((bsz, n2, n1, dfo), BF16),
        scratch_shapes=[_stage_scratch(kb, n2, dfo)],
        compiler_params=_params(("parallel", "parallel")),
        name="fft2",
    )(tr4, ti4, f2)


def _dft_tables(seq):
    n1 = FFT_N1
    n2 = seq // n1
    gd = FOURIER_GROUP_DIM
    total_scale = 1.0 / np.sqrt(float(seq) * gd)
    s_c = 2.0 ** -4
    s_1 = 2.0 ** -3
    s_2 = total_scale / (s_c * s_1)

    def cs(n):
        ang = 2.0 * np.pi * np.outer(np.arange(n), np.arange(n)) / n
        return np.cos(ang), np.sin(ang)

    cc, sc = cs(gd)
    fc = np.concatenate([cc, -sc], axis=1) * s_c
    c1, s1 = cs(n1)
    f1 = np.block([[c1, s1], [-s1, c1]]) * s_1
    c2, s2 = cs(n2)
    f2 = np.concatenate([c2, s2], axis=1) * s_2
    ang = 2.0 * np.pi * np.outer(np.arange(n1), np.arange(n2)) / seq
    cos_t = np.repeat(np.cos(ang), V7X_LANES, axis=1)
    sin_t = np.repeat(np.sin(ang), V7X_LANES, axis=1)
    as_bf16 = lambda m: jnp.asarray(m, F32).astype(BF16)
    return (as_bf16(fc), as_bf16(f1), as_bf16(f2),
            jnp.asarray(cos_t, F32), jnp.asarray(sin_t, F32))


def _mix_kernel(a_ref, cv_ref, ga_ref, gb_ref, x_ref, mod_ref,
                wf_ref, wco_ref, wo_ref, gmoe_ref, wrt_ref, br_ref,
                x1_ref, h2_ref, pt_ref):
    y_b = _dot(cv_ref[...], wco_ref[...])
    y_a = _dot(a_ref[...], wf_ref[...])
    z = ga_ref[...].astype(F32) * y_a + gb_ref[...].astype(F32) * y_b
    mix = _dot(z.astype(BF16), wo_ref[...])
    gate_m = mod_ref[0, 2:3, :]
    x1 = x_ref[...] + gate_m * mix
    x1_ref[...] = x1

    h2 = _rms_modulate(x1, gmoe_ref[...], mod_ref[0, 3:4, :], mod_ref[0, 4:5, :])
    h2_hi = h2.astype(BF16)
    h2_ref[...] = h2_hi
    h2_lo = (h2 - h2_hi.astype(F32)).astype(BF16)
    w_r = wrt_ref[...]
    w_hi = w_r.astype(BF16)
    w_lo = (w_r - w_hi.astype(F32)).astype(BF16)
    n_exp = w_r.shape[0]
    rows_t = lambda a, b: lax.dot_general(a, b, (((1,), (1,)), ((), ())),
                                          preferred_element_type=F32)
    both = rows_t(jnp.concatenate([w_hi, w_lo], axis=0), h2_hi)
    logits = both[:n_exp] + both[n_exp:] + rows_t(w_hi, h2_lo) + br_ref[...]
    m = jnp.max(logits, axis=0, keepdims=True)
    e = jnp.exp(logits - m)
    probs_t = e / jnp.sum(e, axis=0, keepdims=True)
    pt_ref[0] = probs_t


def _mix_call(a2, cv, ga, gb, x2, mod3, wf, wco, wo, g_moe, wr_t, b_r, *, seq):
    t, d = x2.shape
    n_exp = wr_t.shape[0]
    tm = MIX_ROW_TILE
    tpb = seq // tm
    bsz = t // seq
    row = lambda i: (i, 0)
    const = lambda i: (0, 0)
    return pl.pallas_call(
        _mix_kernel,
        grid=(t // tm,),
        in_specs=[pl.BlockSpec((tm, a2.shape[1]), row),
                  pl.BlockSpec((tm, cv.shape[1]), row),
                  pl.BlockSpec((tm, d), row), pl.BlockSpec((tm, d), row),
                  pl.BlockSpec((tm, d), row),
                  pl.BlockSpec((1, N_MOD, d), lambda i: (i // tpb, 0, 0)),
                  _resident(wf.shape, const), _resident(wco.shape, const),
                  _resident(wo.shape, const),
                  pl.BlockSpec((1, d), const),
                  pl.BlockSpec(wr_t.shape, const), pl.BlockSpec(b_r.shape, const)],
        out_specs=[pl.BlockSpec((tm, d), row), pl.BlockSpec((tm, d), row),
                   pl.BlockSpec((1, n_exp, tm), lambda i: (i // tpb, 0, i % tpb))],
        out_shape=[jax.ShapeDtypeStruct((t, d), F32), jax.ShapeDtypeStruct((t, d), BF16),
                   jax.ShapeDtypeStruct((bsz, n_exp, seq), F32)],
        compiler_params=_params(("parallel",)),
        name="mix",
    )(a2, cv, ga, gb, x2, mod3, wf, wco, wo, g_moe, wr_t, b_r)


def _excl_cumsum(mask_f, upper, lower_strict):
    r, rows, lanes = mask_f.shape
    m2 = mask_f.reshape(r * rows, lanes).astype(BF16)
    incl = _dot(m2, upper)
    tot = jnp.broadcast_to(incl[:, lanes - 1:lanes], incl.shape).astype(BF16)
    tot3 = tot.reshape(r, rows, lanes)
    offs = [_dot(lower_strict, tot3[j]) for j in range(r)]
    off = jnp.stack(offs, axis=0)
    return incl.reshape(r, rows, lanes) - mask_f + off


def _select_kernel(p_ref, upper_ref, lower_ref, pos_ref, raw_ref, *, cap):
    p = p_ref[...]
    r = p.shape[0]

    def count(mask):
        c = jnp.sum(mask.astype(F32), axis=2, keepdims=True)
        return jnp.sum(c, axis=1, keepdims=True)

    prefix = jnp.zeros(p.shape, I32)
    for bit in range(29, -1, -1):
        cand = prefix | (1 << bit)
        keep = count(p >= pltpu.bitcast(cand, F32)) >= cap
        prefix = jnp.where(keep, cand, prefix)
    thr = pltpu.bitcast(prefix, F32)
    gt = p > thr
    eq = (p == thr).astype(F32)
    need = cap - count(gt)
    upper = upper_ref[...]
    lower = lower_ref[...]
    rank_eq = _excl_cumsum(eq, upper, lower)
    sel = jnp.where(gt, 1.0, jnp.where(rank_eq < need, eq, 0.0))
    raw = _excl_cumsum(sel, upper, lower)
    raw_i = raw.astype(I32)
    raw_ref[...] = raw_i
    pos_ref[...] = jnp.where(sel > 0.0, raw_i, -1)


def _select_call(p3, cap):
    r, rows, lanes = p3.shape
    upper = jnp.asarray(np.triu(np.ones((lanes, lanes))), BF16)
    lower = jnp.asarray(np.tril(np.ones((rows, rows)), -1), BF16)
    full = lambda shape: pl.BlockSpec(shape, lambda i: (0,) * len(shape))
    sds = jax.ShapeDtypeStruct(p3.shape, I32)
    return pl.pallas_call(
        functools.partial(_select_kernel, cap=cap),
        grid=(1,),
        in_specs=[full(p3.shape), full(upper.shape), full(lower.shape)],
        out_specs=[full(p3.shape), full(p3.shape)],
        out_shape=[sds, sds],
        compiler_params=_params(("arbitrary",)),
        name="select",
    )(p3, upper, lower)


def _gather_kernel(pfx_ref, h2_ref, pos_ref, prob_ref, xe_ref, val_ref, *, n_exp, nb, cap):
    b = pl.program_id(0)
    g = pl.program_id(1)
    tb = TOKEN_BLOCK
    ch = GATHER_CHUNK
    halves = tb // V7X_LANES
    sub = V7X_SUBLANES_BF16

    @pl.when(g == 0)
    def _():
        xe_ref[...] = jnp.zeros_like(xe_ref)
        val_ref[...] = jnp.zeros_like(val_ref)

    chunk_rows = lax.broadcasted_iota(I32, (ch, V7X_LANES), 0)
    expert_lane = lax.broadcasted_iota(I32, (1, n_exp), 1)

    def window(e, k):
        base = (b * n_exp + e) * (nb + 1) + k
        p0 = pfx_ref[base]
        start = pl.multiple_of((p0 // sub) * sub, sub)
        return start, pfx_ref[base + 1] - start

    def one_hot(pos_rows, prob_rows, ws):
        rows = chunk_rows + ws
        hits = [jnp.where(p == rows, 1.0, 0.0) for p in pos_rows]
        onehot = jnp.concatenate([h.astype(BF16) for h in hits], axis=1)
        v = hits[0] * prob_rows[0]
        for h, p in zip(hits[1:], prob_rows[1:]):
            v = v + h * p
        return onehot, jnp.sum(v, axis=1, keepdims=True)

    for j in range(GATHER_BLOCKS):
        k = g * GATHER_BLOCKS + j
        hk = h2_ref[0, j * tb:(j + 1) * tb, :]
        tok_rows = [slice(halves * j + h, halves * j + h + 1) for h in range(halves)]
        starts, pieces = [], []
        overflow = None
        for e in range(n_exp):
            ws, need = window(e, k)
            onehot, v = one_hot([pos_ref[e, r, :] for r in tok_rows],
                                [prob_ref[e, r, :] for r in tok_rows], ws)
            val_ref[0, pl.ds(ws, ch), e:e + 1] += v
            starts.append(ws)
            pieces.append(onehot)
            over = need > ch
            overflow = over if overflow is None else jnp.logical_or(overflow, over)
        gathered = _dot(jnp.concatenate(pieces, axis=0), hk)
        for e in range(n_exp):
            xe_ref[0, e, pl.ds(starts[e], ch), :] += gathered[e * ch:(e + 1) * ch, :].astype(BF16)

        @pl.when(overflow)
        def _():
            def per_expert(e, carry):
                ws, need = window(e, k)

                def chunk(c, carry2):
                    wsc = pl.multiple_of(ws + c * ch, sub)
                    onehot, v = one_hot([pos_ref[pl.ds(e, 1), r, :][0] for r in tok_rows],
                                        [prob_ref[pl.ds(e, 1), r, :][0] for r in tok_rows], wsc)
                    xe_ref[0, e, pl.ds(wsc, ch), :] += _dot(onehot, hk).astype(BF16)
                    val_ref[0, pl.ds(wsc, ch), :] += jnp.where(expert_lane == e, v, 0.0)
                    return carry2

                lax.fori_loop(1, (need + ch - 1) // ch, chunk, 0)
                return carry

            lax.fori_loop(0, n_exp, per_expert, 0)


def _gather_call(pfx, h2_3, pos3, prob3, *, n_exp, cap):
    bsz, seq, d = h2_3.shape
    nb = seq // TOKEN_BLOCK
    gb = GATHER_BLOCKS
    rows_per_step = gb * TOKEN_BLOCK // V7X_LANES
    slots = cap + GATHER_CHUNK
    grid_spec = pltpu.PrefetchScalarGridSpec(
        num_scalar_prefetch=1,
        grid=(bsz, nb // gb),
        in_specs=[pl.BlockSpec((1, gb * TOKEN_BLOCK, d), lambda b, g, p: (b, g, 0)),
                  pl.BlockSpec((n_exp, rows_per_step, V7X_LANES), lambda b, g, p: (b, g, 0)),
                  pl.BlockSpec((n_exp, rows_per_step, V7X_LANES), lambda b, g, p: (b, g, 0))],
        out_specs=[_resident((1, n_exp, slots, d), lambda b, g, p: (b, 0, 0, 0)),
                   pl.BlockSpec((1, slots, n_exp), lambda b, g, p: (b, 0, 0))],
    )
    return pl.pallas_call(
        functools.partial(_gather_kernel, n_exp=n_exp, nb=nb, cap=cap),
        grid_spec=grid_spec,
        out_shape=[jax.ShapeDtypeStruct((bsz, n_exp, slots, d), BF16),
                   jax.ShapeDtypeStruct((bsz, slots, n_exp), F32)],
        compiler_params=_params(("arbitrary", "arbitrary")),
        name="gather",
    )(pfx, h2_3, pos3, prob3)


def _ffn_kernel(xe_ref, val_ref, wg_ref, wu_ref, wd_ref, ye_ref, acc_ref):
    e = pl.program_id(0)
    f = pl.program_id(1)
    bsz, _, cap, d = xe_ref.shape

    @pl.when(f == 0)
    def _():
        acc_ref[...] = jnp.zeros_like(acc_ref)

    xe = xe_ref[...].reshape(bsz * cap, d)
    a = _dot(xe, wg_ref[0].astype(BF16))
    u = _dot(xe, wu_ref[0].astype(BF16))
    hm = (a * jax.nn.sigmoid(a) * u).astype(BF16)
    acc_ref[...] += _dot(hm, wd_ref[0].astype(BF16))

    @pl.when(f == pl.num_programs(1) - 1)
    def _():
        vals = val_ref[...].reshape(bsz * cap, val_ref.shape[2])
        mine = lax.broadcasted_iota(I32, (1, vals.shape[1]), 1) == e
        val = jnp.sum(jnp.where(mine, vals, 0.0), axis=1, keepdims=True)
        ye_ref[...] = (acc_ref[...] * val).astype(BF16).reshape(bsz, 1, cap, d)


def _ffn_call(xe, val, w_gate, w_up, w_down, *, cap):
    bsz, n_exp, _, d = xe.shape
    d_exp = w_gate.shape[2]
    fc = FFN_CHUNK
    return pl.pallas_call(
        _ffn_kernel,
        grid=(n_exp, d_exp // fc),
        in_specs=[pl.BlockSpec((bsz, 1, cap, d), lambda e, f: (0, e, 0, 0)),
                  pl.BlockSpec((bsz, cap, n_exp), lambda e, f: (0, 0, 0)),
                  pl.BlockSpec((1, d, fc), lambda e, f: (e, 0, f)),
                  pl.BlockSpec((1, d, fc), lambda e, f: (e, 0, f)),
                  pl.BlockSpec((1, fc, d), lambda e, f: (e, f, 0))],
        out_specs=pl.BlockSpec((bsz, 1, cap, d), lambda e, f: (0, e, 0, 0)),
        out_shape=jax.ShapeDtypeStruct((bsz, n_exp, cap, d), BF16),
        scratch_shapes=[pltpu.VMEM((bsz * cap, d), F32)],
        compiler_params=_params(("arbitrary", "arbitrary")),
        name="ffn",
    )(xe, val, w_gate, w_up, w_down)


def _combine_kernel(pfx_ref, x1_ref, ye_ref, post_ref, mod_ref, gfin_ref, o_ref,
                    ycat_ref, scat_ref, acc_ref, *, n_exp, nb, cap):
    b = pl.program_id(0)
    tb = TOKEN_BLOCK
    win = COMBINE_WINDOW
    align = V7X_SUBLANES_BF16
    gate_f = mod_ref[0, 5:6, :]
    lane = lax.broadcasted_iota(I32, (1, V7X_LANES), 1)
    per = V7X_LANES // win

    def finish(rows, moe):
        x2 = x1_ref[rows, :] + gate_f * moe
        ms = jnp.mean(x2 * x2, axis=-1, keepdims=True)
        o_ref[rows, :] = x2 * lax.rsqrt(ms + RMS_EPS) * gfin_ref[...]

    redo = []
    for s in range(COMBINE_BLOCKS):
        k = pl.program_id(1) * COMBINE_BLOCKS + s
        rows = slice(s * tb, (s + 1) * tb)
        p0s, ends, wss = [], [], []
        fits = None
        for e in range(n_exp):
            base = (b * n_exp + e) * (nb + 1) + k
            p0 = pfx_ref[base]
            end = pfx_ref[base + 1]
            ws = pl.multiple_of(jnp.minimum((p0 // align) * align, cap - win), align)
            ok = end <= ws + win
            fits = ok if fits is None else jnp.logical_and(fits, ok)
            p0s.append(p0)
            ends.append(end)
            wss.append(ws)
        for e0 in range(0, n_exp, per):
            target = lane + wss[e0]
            pos_col = post_ref[rows, e0:e0 + 1]
            for j in range(1, per):
                mine = lane >= j * win
                target = jnp.where(mine, lane + (wss[e0 + j] - j * win), target)
                pos_col = jnp.where(mine, post_ref[rows, e0 + j:e0 + j + 1], pos_col)
            g = e0 // per
            scat_ref[s, :, g * V7X_LANES:(g + 1) * V7X_LANES] = (
                jnp.where(pos_col == target, 1.0, 0.0).astype(BF16))
            for j in range(per):
                e = e0 + j
                ycat_ref[s, e * win:(e + 1) * win, :] = ye_ref[0, e, pl.ds(wss[e], win), :]
        finish(rows, _dot(scat_ref[s], ycat_ref[s]))
        redo.append((rows, jnp.logical_not(fits), p0s, ends))

    for rows, misfit, p0s, ends in redo:
        @pl.when(misfit)
        def _():
            acc_ref[...] = jnp.zeros_like(acc_ref)
            cols = lax.broadcasted_iota(I32, (tb, tb), 1)
            for e in range(n_exp):
                ws1 = pl.multiple_of(jnp.minimum((p0s[e] // align) * align, cap - tb), align)
                ws2 = pl.multiple_of(jnp.minimum(ws1 + tb, cap - tb), align)
                pos_col = post_ref[rows, e:e + 1]

                @pl.when(ends[e] > p0s[e])
                def _():
                    onehot = jnp.where(pos_col == cols + ws1, 1.0, 0.0).astype(BF16)
                    acc_ref[...] += _dot(onehot, ye_ref[0, e, pl.ds(ws1, tb), :])

                @pl.when(ends[e] > ws1 + tb)
                def _():
                    c2 = cols + ws2
                    hit = jnp.logical_and(pos_col == c2, c2 >= ws1 + tb)
                    onehot = jnp.where(hit, 1.0, 0.0).astype(BF16)
                    acc_ref[...] += _dot(onehot, ye_ref[0, e, pl.ds(ws2, tb), :])
            finish(rows, acc_ref[...])


def _combine_call(pfx, x1, ye, pos_t, mod3, g_final, *, seq):
    t, d = x1.shape
    bsz, n_exp, cap, _ = ye.shape
    tb = TOKEN_BLOCK
    nb = seq // tb
    step_rows = COMBINE_BLOCKS * tb
    steps = nb // COMBINE_BLOCKS
    row = lambda b, k, p: (b * steps + k, 0)
    grid_spec = pltpu.PrefetchScalarGridSpec(
        num_scalar_prefetch=1,
        grid=(bsz, steps),
        in_specs=[pl.BlockSpec((step_rows, d), row),
                  _resident((1, n_exp, cap, d), lambda b, k, p: (b, 0, 0, 0)),
                  pl.BlockSpec((step_rows, n_exp), row),
                  pl.BlockSpec((1, N_MOD, d), lambda b, k, p: (b, 0, 0)),
                  pl.BlockSpec((1, d), lambda b, k, p: (0, 0))],
        out_specs=pl.BlockSpec((step_rows, d), row),
        scratch_shapes=[pltpu.VMEM((COMBINE_BLOCKS, n_exp * COMBINE_WINDOW, d), BF16),
                        pltpu.VMEM((COMBINE_BLOCKS, tb, n_exp * COMBINE_WINDOW), BF16),
                        pltpu.VMEM((tb, d), F32)],
    )
    return pl.pallas_call(
        functools.partial(_combine_kernel, n_exp=n_exp, nb=nb, cap=cap),
        grid_spec=grid_spec,
        out_shape=jax.ShapeDtypeStruct((t, d), F32),
        compiler_params=_params(("arbitrary", "arbitrary")),
        name="combine",
    )(pfx, x1, ye, pos_t, mod3, g_final)


def _layer(x2, c_act_in, w_ada, b_ada, g_norm_mix, w_in, b_gate, w_fourier, w_conv, w_conv_out,
           w_o, g_norm_moe, w_router, b_router, w_gate_e, w_up_e, w_down_e, *, bsz, seq):
    t, d = x2.shape
    n_exp = w_router.shape[1]
    cap = EC_CAPACITY * seq // n_exp
    n1 = FFT_N1
    n2 = seq // n1
    fc, f1, f2, cos_t, sin_t = _dft_tables(seq)

    mod = _mod_call(c_act_in, w_ada, b_ada.reshape(1, -1))
    mod3 = mod.reshape(bsz, N_MOD, d)

    wre, wim, cv, ga, gb = _proj_call(
        x2, mod3, g_norm_mix.reshape(1, d), w_in.astype(BF16), b_gate.reshape(1, -1), fc, w_conv,
        seq=seq)

    tr, ti = _fft1_call(wre, wim, f1, cos_t, sin_t)
    a2 = _fft2_call(tr, ti, f2).reshape(t, D_FOURIER)

    x1, h2, probs_t = _mix_call(
        a2, cv, ga, gb, x2, mod3, w_fourier.astype(BF16), w_conv_out.astype(BF16),
        w_o.astype(BF16), g_norm_moe.reshape(1, d), w_router.T, b_router.reshape(n_exp, 1),
        seq=seq)

    prob3 = probs_t.reshape(bsz * n_exp, seq // V7X_LANES, V7X_LANES)
    pos, raw = _select_call(prob3, cap)
    stride = TOKEN_BLOCK // V7X_LANES
    starts = raw[:, ::stride, 0]
    pfx = jnp.concatenate([starts, jnp.full((bsz * n_exp, 1), cap, I32)], axis=1).reshape(-1)
    pos_t = jnp.transpose(pos.reshape(bsz, n_exp, seq), (0, 2, 1)).reshape(t, n_exp)

    xe, val = _gather_call(pfx, h2.reshape(bsz, seq, d), pos, prob3, n_exp=n_exp, cap=cap)
    ye = _ffn_call(xe, val, w_gate_e, w_up_e, w_down_e, cap=cap)
    return ye, pfx, x1, pos_t, mod3


def kernel(x, c, w_ada, b_ada, g_norm_mix, w_in, b_gate, w_fourier, w_conv, w_conv_out, w_o,
           g_norm_moe, w_router, b_router, w_gate_e, w_up_e, w_down_e, g_final):
    bsz, seq, d = x.shape
    assert w_ada.shape[0] == 1
    assert seq % (FFT_N1 * V7X_SUBLANES_F32) == 0 and seq % ROW_TILE == 0
    assert (EC_CAPACITY * seq // w_router.shape[2]) >= TOKEN_BLOCK
    c_pad = c.T
    x2 = x.reshape(bsz * seq, d)
    ye, pfx, x1, pos_t, mod3 = _layer(
        x2, c_pad, w_ada[0], b_ada[0], g_norm_mix[0], w_in[0], b_gate[0], w_fourier[0],
        w_conv[0], w_conv_out[0], w_o[0], g_norm_moe[0], w_router[0], b_router[0],
        w_gate_e[0], w_up_e[0], w_down_e[0], bsz=bsz, seq=seq)
    out = _combine_call(pfx, x1, ye, pos_t, mod3, g_final.reshape(1, d), seq=seq)
    return out.reshape(bsz, seq, d)
```

```python
import functools

import numpy as np
import jax
import jax.numpy as jnp
from jax import lax
from jax.experimental import pallas as pl
from jax.experimental.pallas import tpu as pltpu

F32 = jnp.float32
BF16 = jnp.bfloat16
I32 = jnp.int32

FOURIER_GROUPS = 4
FOURIER_GROUP_DIM = 128
D_FOURIER = FOURIER_GROUPS * FOURIER_GROUP_DIM
N_MOD = 6
EC_CAPACITY = 2
RMS_EPS = 1e-6

V7X_LANES = 128
V7X_SUBLANES_F32 = 8
V7X_SUBLANES_BF16 = 16
V7X_VMEM_BYTES = 64 * 1024 * 1024
VMEM_LIMIT_BYTES = V7X_VMEM_BYTES - 6 * 1024 * 1024

MOD_COL_TILE = 1536
MOD_ACC_COLS = 512
ROW_TILE = 1024
FFT_N1 = 128
FFT_N2_TILE = 16
FFT2_K_TILE = 16
TOKEN_BLOCK = 256
GATHER_CHUNK = 64
GATHER_BLOCKS = 8
COMBINE_WINDOW = 64
COMBINE_BLOCKS = 2
FFN_CHUNK = 512


def _dot(a, b):
    return jnp.dot(a, b, preferred_element_type=F32)


def _params(semantics):
    return pltpu.CompilerParams(dimension_semantics=semantics,
                                vmem_limit_bytes=VMEM_LIMIT_BYTES)


def _pitch(rows):
    p = -(-rows // V7X_SUBLANES_F32)
    return (p if p % 2 else p + 1) * V7X_SUBLANES_F32


def _stage_group(stage_ref, group, value):
    rows = value.shape[0]
    row0 = group * _pitch(rows)
    for l in range(stage_ref.shape[0]):
        stage_ref[l, row0:row0 + rows, :] = value[:, l * V7X_LANES:(l + 1) * V7X_LANES]


def _row_of_each_group(stage_ref, row, groups, rows):
    return jnp.concatenate([stage_ref[l, pl.ds(row, groups, stride=_pitch(rows)), :]
                            for l in range(stage_ref.shape[0])], axis=1)


def _stage_scratch(groups, rows, width):
    return pltpu.VMEM((width // V7X_LANES, groups * _pitch(rows), V7X_LANES), F32)


def _resident(block_shape, index_map):
    return pl.BlockSpec(block_shape, index_map, pipeline_mode=pl.Buffered(1))


def _mod_kernel(ct_ref, w_ref, b_ref, o_ref, *, bsz):
    ct = ct_ref[...]
    ct = ct * jax.nn.sigmoid(ct)
    sub = V7X_SUBLANES_F32
    d, tn = w_ref.shape
    cw = MOD_ACC_COLS
    for c0 in range(0, tn, cw):
        accs = [jnp.zeros((sub, cw), F32) for _ in range(bsz)]
        for g in range(d // sub):
            w = w_ref[g * sub:(g + 1) * sub, c0:c0 + cw]
            for r in range(bsz):
                accs[r] = accs[r] + ct[g * sub:(g + 1) * sub, r:r + 1] * w
        for r in range(bsz):
            o_ref[r:r + 1, c0:c0 + cw] = (jnp.sum(accs[r], axis=0, keepdims=True)
                                          + b_ref[:, c0:c0 + cw])


def _mod_call(c_t, w_ada, b_ada):
    d, bsz = c_t.shape
    n = w_ada.shape[1]
    tn = MOD_COL_TILE
    return pl.pallas_call(
        functools.partial(_mod_kernel, bsz=bsz),
        grid=(n // tn,),
        in_specs=[pl.BlockSpec((d, bsz), lambda j: (0, 0)),
                  pl.BlockSpec((d, tn), lambda j: (0, j)),
                  pl.BlockSpec((1, tn), lambda j: (0, j))],
        out_specs=pl.BlockSpec((bsz, tn), lambda j: (0, j)),
        out_shape=jax.ShapeDtypeStruct((bsz, n), F32),
        compiler_params=_params(("arbitrary",)),
        name="mod",
    )(c_t, w_ada, b_ada)


def _rms_modulate(x, g, shift, scale):
    ms = jnp.mean(x * x, axis=-1, keepdims=True)
    y = x * lax.rsqrt(ms + RMS_EPS) * g
    return y * (1.0 + scale) + shift


def _proj_kernel(x_ref, xprev_ref, xnext_ref, mod_ref, g_ref, win_ref, bgate_ref, fc_ref, wconv_ref,
                 wre_ref, wim_ref, cv_ref, ga_ref, gb_ref, re_s, im_s,
                 *, d, d_conv, n2, tm, tpb):
    i = pl.program_id(0)
    shift = mod_ref[0, 0:1, :]
    scale = mod_ref[0, 1:2, :]
    h = _rms_modulate(x_ref[...], g_ref[...], shift, scale).astype(BF16)
    x_halo = jnp.concatenate([xprev_ref[...], xnext_ref[...]], axis=0)
    h_halo = _rms_modulate(x_halo, g_ref[...], shift, scale).astype(BF16)
    h_ext = jnp.concatenate([h, h_halo], axis=0)

    o1 = D_FOURIER
    o2 = o1 + d_conv
    o3 = o2 + d_conv
    o4 = o3 + d_conv
    uf = _dot(h, win_ref[:, 0:o1]).astype(BF16)
    gd = FOURIER_GROUP_DIM
    ws = [_dot(uf[:, g * gd:(g + 1) * gd], fc_ref[...]) for g in range(FOURIER_GROUPS)]
    w_re = jnp.concatenate([w[:, :gd] for w in ws], axis=1)
    w_im = jnp.concatenate([w[:, gd:] for w in ws], axis=1)
    n1_rows = w_re.shape[0] // n2
    for r in range(n1_rows):
        _stage_group(re_s, r, w_re[r * n2:(r + 1) * n2, :])
        _stage_group(im_s, r, w_im[r * n2:(r + 1) * n2, :])
    for j in range(n2):
        cols = slice(j * D_FOURIER, (j + 1) * D_FOURIER)
        wre_ref[0, :, cols] = _row_of_each_group(re_s, j, n1_rows, n2).astype(BF16)
        wim_ref[0, :, cols] = _row_of_each_group(im_s, j, n1_rows, n2).astype(BF16)
    q_ext = _dot(h_ext, win_ref[:, o3:o4]) * _dot(h_ext, win_ref[:, o1:o2])
    q = q_ext[0:tm, :]
    nh = xprev_ref.shape[0]
    first = (i % tpb) == 0
    last = (i % tpb) == tpb - 1
    hp = jnp.where(first, 0.0, q_ext[tm + nh - 1:tm + nh, :])
    hn = jnp.where(last, 0.0, q_ext[tm + nh:tm + nh + 1, :])
    rows = lax.broadcasted_iota(I32, (tm, 1), 0)
    q_prev = jnp.where(rows == 0, hp, pltpu.roll(q, 1, axis=0))
    q_next = jnp.where(rows == tm - 1, hn, pltpu.roll(q, tm - 1, axis=0))
    conv = q_prev * wconv_ref[0:1, :] + q * wconv_ref[1:2, :] + q_next * wconv_ref[2:3, :]
    cv_ref[...] = (_dot(h, win_ref[:, o2:o3]) * conv).astype(BF16)
    ga_ref[...] = jax.nn.sigmoid(_dot(h, win_ref[:, o4:o4 + d]) + bgate_ref[:, 0:d]).astype(BF16)
    gb_ref[...] = jax.nn.sigmoid(
        _dot(h, win_ref[:, o4 + d:o4 + 2 * d]) + bgate_ref[:, d:2 * d]).astype(BF16)


def _proj_call(x2, mod3, g_mix, w_in, b_gate, fc, w_conv, *, seq):
    t, d = x2.shape
    k_in = w_in.shape[1]
    d_conv = (k_in - D_FOURIER - 2 * d) // 3
    tm = ROW_TILE
    tpb = seq // tm
    n2 = seq // FFT_N1
    n1_rows = tm // n2
    bsz = t // seq
    sub = V7X_SUBLANES_F32
    row = lambda i: (i, 0)
    const = lambda i: (0, 0)
    out_sds = lambda n: jax.ShapeDtypeStruct((t, n), BF16)
    assert n1_rows % V7X_SUBLANES_BF16 == 0
    dft_in = jax.ShapeDtypeStruct((bsz, FFT_N1, n2 * D_FOURIER), BF16)
    dft_blk = pl.BlockSpec((1, n1_rows, n2 * D_FOURIER), lambda i: (i // tpb, i % tpb, 0))
    return pl.pallas_call(
        functools.partial(_proj_kernel, d=d, d_conv=d_conv, n2=n2, tm=tm, tpb=tpb),
        grid=(t // tm,),
        in_specs=[pl.BlockSpec((tm, d), row),
                  pl.BlockSpec((sub, d), lambda i: (jnp.maximum(i * (tm // sub) - 1, 0), 0)),
                  pl.BlockSpec((sub, d),
                               lambda i: (jnp.minimum((i + 1) * (tm // sub), t // sub - 1), 0)),
                  pl.BlockSpec((1, N_MOD, d), lambda i: (i // tpb, 0, 0)),
                  pl.BlockSpec((1, d), const),
                  _resident((d, k_in), const),
                  pl.BlockSpec((1, 2 * d), const),
                  pl.BlockSpec(fc.shape, const),
                  pl.BlockSpec(w_conv.shape, const)],
        out_specs=[dft_blk, dft_blk, pl.BlockSpec((tm, d_conv), row),
                   pl.BlockSpec((tm, d), row), pl.BlockSpec((tm, d), row)],
        out_shape=[dft_in, dft_in, out_sds(d_conv), out_sds(d), out_sds(d)],
        scratch_shapes=[_stage_scratch(n1_rows, n2, D_FOURIER)] * 2,
        compiler_params=_params(("parallel",)),
        name="proj",
    )(x2, x2, x2, mod3, g_mix, w_in, b_gate, fc, w_conv)


def _fft1_kernel(wre_ref, wim_ref, f1_ref, cos_ref, sin_ref, tr_ref, ti_ref, re_s, im_s,
                 *, n1, nt):
    reps = D_FOURIER // V7X_LANES
    half = nt // 2
    for h in range(2):
        hcols = slice(h * half * D_FOURIER, (h + 1) * half * D_FOURIER)
        w = jnp.concatenate([wre_ref[0, :, hcols], wim_ref[0, :, hcols]], axis=0)
        t = _dot(f1_ref[...], w)
        for jj in range(half):
            j = h * half + jj
            cols = slice(jj * D_FOURIER, (jj + 1) * D_FOURIER)
            lanes = slice(j * V7X_LANES, (j + 1) * V7X_LANES)
            c = jnp.concatenate([cos_ref[:, lanes]] * reps, axis=1)
            s = jnp.concatenate([sin_ref[:, lanes]] * reps, axis=1)
            a = t[:n1, cols]
            b = t[n1:, cols]
            _stage_group(re_s, j, a * c + b * s)
            _stage_group(im_s, j, b * c - a * s)
    for k in range(n1):
        tr_ref[0, k] = _row_of_each_group(re_s, k, nt, n1).astype(BF16)
        ti_ref[0, k] = _row_of_each_group(im_s, k, nt, n1).astype(BF16)


def _fft1_call(wre3, wim3, f1, cos_t, sin_t):
    bsz, n1, cols = wre3.shape
    n2 = cols // D_FOURIER
    nt = FFT_N2_TILE
    tn = nt * D_FOURIER
    blk = lambda b, j: (b, 0, j)
    oblk = lambda b, j: (b, 0, j, 0)
    sds = jax.ShapeDtypeStruct((bsz, n1, n2, D_FOURIER), BF16)
    return pl.pallas_call(
        functools.partial(_fft1_kernel, n1=n1, nt=nt),
        grid=(bsz, n2 // nt),
        in_specs=[pl.BlockSpec((1, n1, tn), blk), pl.BlockSpec((1, n1, tn), blk),
                  pl.BlockSpec(f1.shape, lambda b, j: (0, 0)),
                  pl.BlockSpec((n1, nt * V7X_LANES), lambda b, j: (0, j)),
                  pl.BlockSpec((n1, nt * V7X_LANES), lambda b, j: (0, j))],
        out_specs=[pl.BlockSpec((1, n1, nt, D_FOURIER), oblk),
                   pl.BlockSpec((1, n1, nt, D_FOURIER), oblk)],
        out_shape=[sds, sds],
        scratch_shapes=[_stage_scratch(nt, n1, D_FOURIER)] * 2,
        compiler_params=_params(("parallel", "parallel")),
        name="fft1",
    )(wre3, wim3, f1, cos_t, sin_t)


def _fft2_kernel(tr_ref, ti_ref, f2_ref, o_ref, res_s, *, kb, n2):
    for k in range(kb):
        rhs = jnp.concatenate([tr_ref[0, k], ti_ref[0, k]], axis=0)
        _stage_group(res_s, k, _dot(f2_ref[...], rhs))
    for k2 in range(n2):
        o_ref[0, k2] = _row_of_each_group(res_s, k2, kb, n2).astype(BF16)


def _fft2_call(tr4, ti4, f2):
    bsz, n1, n2, dfo = tr4.shape
    kb = FFT2_K_TILE
    blk = lambda b, j: (b, j, 0, 0)
    return pl.pallas_call(
        functools.partial(_fft2_kernel, kb=kb, n2=n2),
        grid=(bsz, n1 // kb),
        in_specs=[pl.BlockSpec((1, kb, n2, dfo), blk), pl.BlockSpec((1, kb, n2, dfo), blk),
                  pl.BlockSpec(f2.shape, lambda b, j: (0, 0))],
        out_specs=pl.BlockSpec((1, n2, kb, dfo), lambda b, j: (b, 0, j, 0)),
        out_shape=jax.ShapeDtypeStruct((bsz, n2, n1, dfo), BF16),
        scratch_shapes=[_stage_scratch(kb, n2, dfo)],
        compiler_params=_params(("parallel", "parallel")),
        name="fft2",
    )(tr4, ti4, f2)


def _dft_tables(seq):
    n1 = FFT_N1
    n2 = seq // n1
    gd = FOURIER_GROUP_DIM
    total_scale = 1.0 / np.sqrt(float(seq) * gd)
    s_c = 2.0 ** -4
    s_1 = 2.0 ** -3
    s_2 = total_scale / (s_c * s_1)

    def cs(n):
        ang = 2.0 * np.pi * np.outer(np.arange(n), np.arange(n)) / n
        return np.cos(ang), np.sin(ang)

    cc, sc = cs(gd)
    fc = np.concatenate([cc, -sc], axis=1) * s_c
    c1, s1 = cs(n1)
    f1 = np.block([[c1, s1], [-s1, c1]]) * s_1
    c2, s2 = cs(n2)
    f2 = np.concatenate([c2, s2], axis=1) * s_2
    ang = 2.0 * np.pi * np.outer(np.arange(n1), np.arange(n2)) / seq
    cos_t = np.repeat(np.cos(ang), V7X_LANES, axis=1)
    sin_t = np.repeat(np.sin(ang), V7X_LANES, axis=1)
    as_bf16 = lambda m: jnp.asarray(m, F32).astype(BF16)
    return (as_bf16(fc), as_bf16(f1), as_bf16(f2),
            jnp.asarray(cos_t, F32), jnp.asarray(sin_t, F32))


def _mix_kernel(a_ref, cv_ref, ga_ref, gb_ref, x_ref, mod_ref,
                wf_ref, wco_ref, wo_ref, gmoe_ref, wrt_ref, br_ref,
                x1_ref, h2_ref, pt_ref):
    y_b = _dot(cv_ref[...], wco_ref[...])
    y_a = _dot(a_ref[...], wf_ref[...])
    z = ga_ref[...].astype(F32) * y_a + gb_ref[...].astype(F32) * y_b
    mix = _dot(z.astype(BF16), wo_ref[...])
    gate_m = mod_ref[0, 2:3, :]
    x1 = x_ref[...] + gate_m * mix
    x1_ref[...] = x1

    h2 = _rms_modulate(x1, gmoe_ref[...], mod_ref[0, 3:4, :], mod_ref[0, 4:5, :])
    h2_hi = h2.astype(BF16)
    h2_ref[...] = h2_hi
    h2_lo = (h2 - h2_hi.astype(F32)).astype(BF16)
    w_r = wrt_ref[...]
    w_hi = w_r.astype(BF16)
    w_lo = (w_r - w_hi.astype(F32)).astype(BF16)
    n_exp = w_r.shape[0]
    rows_t = lambda a, b: lax.dot_general(a, b, (((1,), (1,)), ((), ())),
                                          preferred_element_type=F32)
    both = rows_t(jnp.concatenate([w_hi, w_lo], axis=0), h2_hi)
    logits = both[:n_exp] + both[n_exp:] + rows_t(w_hi, h2_lo) + br_ref[...]
    m = jnp.max(logits, axis=0, keepdims=True)
    e = jnp.exp(logits - m)
    probs_t = e / jnp.sum(e, axis=0, keepdims=True)
    pt_ref[0] = probs_t


def _mix_call(a2, cv, ga, gb, x2, mod3, wf, wco, wo, g_moe, wr_t, b_r, *, seq):
    t, d = x2.shape
    n_exp = wr_t.shape[0]
    tm = ROW_TILE
    tpb = seq // tm
    bsz = t // seq
    row = lambda i: (i, 0)
    const = lambda i: (0, 0)
    return pl.pallas_call(
        _mix_kernel,
        grid=(t // tm,),
        in_specs=[pl.BlockSpec((tm, a2.shape[1]), row),
                  pl.BlockSpec((tm, cv.shape[1]), row),
                  pl.BlockSpec((tm, d), row), pl.BlockSpec((tm, d), row),
                  pl.BlockSpec((tm, d), row),
                  pl.BlockSpec((1, N_MOD, d), lambda i: (i // tpb, 0, 0)),
                  _resident(wf.shape, const), _resident(wco.shape, const),
                  _resident(wo.shape, const),
                  pl.BlockSpec((1, d), const),
                  pl.BlockSpec(wr_t.shape, const), pl.BlockSpec(b_r.shape, const)],
        out_specs=[pl.BlockSpec((tm, d), row), pl.BlockSpec((tm, d), row),
                   pl.BlockSpec((1, n_exp, tm), lambda i: (i // tpb, 0, i % tpb))],
        out_shape=[jax.ShapeDtypeStruct((t, d), F32), jax.ShapeDtypeStruct((t, d), BF16),
                   jax.ShapeDtypeStruct((bsz, n_exp, seq), F32)],
        compiler_params=_params(("parallel",)),
        name="mix",
    )(a2, cv, ga, gb, x2, mod3, wf, wco, wo, g_moe, wr_t, b_r)


def _excl_cumsum(mask_f, upper, lower_strict):
    r, rows, lanes = mask_f.shape
    m2 = mask_f.reshape(r * rows, lanes).astype(BF16)
    incl = _dot(m2, upper)
    tot = jnp.broadcast_to(incl[:, lanes - 1:lanes], incl.shape).astype(BF16)
    tot3 = tot.reshape(r, rows, lanes)
    offs = [_dot(lower_strict, tot3[j]) for j in range(r)]
    off = jnp.stack(offs, axis=0)
    return incl.reshape(r, rows, lanes) - mask_f + off


def _select_kernel(p_ref, upper_ref, lower_ref, pos_ref, raw_ref, *, cap):
    p = p_ref[...]
    r = p.shape[0]

    def count(mask):
        c = jnp.sum(mask.astype(F32), axis=2, keepdims=True)
        return jnp.sum(c, axis=1, keepdims=True)

    prefix = jnp.zeros(p.shape, I32)
    for bit in range(29, -1, -1):
        cand = prefix | (1 << bit)
        keep = count(p >= pltpu.bitcast(cand, F32)) >= cap
        prefix = jnp.where(keep, cand, prefix)
    thr = pltpu.bitcast(prefix, F32)
    gt = p > thr
    eq = (p == thr).astype(F32)
    need = cap - count(gt)
    upper = upper_ref[...]
    lower = lower_ref[...]
    rank_eq = _excl_cumsum(eq, upper, lower)
    sel = jnp.where(gt, 1.0, jnp.where(rank_eq < need, eq, 0.0))
    raw = _excl_cumsum(sel, upper, lower)
    raw_i = raw.astype(I32)
    raw_ref[...] = raw_i
    pos_ref[...] = jnp.where(sel > 0.0, raw_i, -1)


def _select_call(p3, cap):
    r, rows, lanes = p3.shape
    upper = jnp.asarray(np.triu(np.ones((lanes, lanes))), BF16)
    lower = jnp.asarray(np.tril(np.ones((rows, rows)), -1), BF16)
    full = lambda shape: pl.BlockSpec(shape, lambda i: (0,) * len(shape))
    sds = jax.ShapeDtypeStruct(p3.shape, I32)
    return pl.pallas_call(
        functools.partial(_select_kernel, cap=cap),
        grid=(1,),
        in_specs=[full(p3.shape), full(upper.shape), full(lower.shape)],
        out_specs=[full(p3.shape), full(p3.shape)],
        out_shape=[sds, sds],
        compiler_params=_params(("arbitrary",)),
        name="select",
    )(p3, upper, lower)


def _gather_kernel(pfx_ref, h2_ref, pos_ref, prob_ref, xe_ref, val_ref, *, n_exp, nb, cap):
    b = pl.program_id(0)
    g = pl.program_id(1)
    tb = TOKEN_BLOCK
    ch = GATHER_CHUNK
    halves = tb // V7X_LANES
    sub = V7X_SUBLANES_BF16

    @pl.when(g == 0)
    def _():
        xe_ref[...] = jnp.zeros_like(xe_ref)
        val_ref[...] = jnp.zeros_like(val_ref)

    chunk_rows = lax.broadcasted_iota(I32, (ch, V7X_LANES), 0)
    expert_lane = lax.broadcasted_iota(I32, (1, n_exp), 1)

    def window(e, k):
        base = (b * n_exp + e) * (nb + 1) + k
        p0 = pfx_ref[base]
        start = pl.multiple_of((p0 // sub) * sub, sub)
        return start, pfx_ref[base + 1] - start

    def one_hot(pos_rows, prob_rows, ws):
        rows = chunk_rows + ws
        hits = [jnp.where(p == rows, 1.0, 0.0) for p in pos_rows]
        onehot = jnp.concatenate([h.astype(BF16) for h in hits], axis=1)
        v = hits[0] * prob_rows[0]
        for h, p in zip(hits[1:], prob_rows[1:]):
            v = v + h * p
        return onehot, jnp.sum(v, axis=1, keepdims=True)

    for j in range(GATHER_BLOCKS):
        k = g * GATHER_BLOCKS + j
        hk = h2_ref[0, j * tb:(j + 1) * tb, :]
        tok_rows = [slice(halves * j + h, halves * j + h + 1) for h in range(halves)]
        starts, pieces = [], []
        overflow = None
        for e in range(n_exp):
            ws, need = window(e, k)
            onehot, v = one_hot([pos_ref[e, r, :] for r in tok_rows],
                                [prob_ref[e, r, :] for r in tok_rows], ws)
            val_ref[0, pl.ds(ws, ch), e:e + 1] += v
            starts.append(ws)
            pieces.append(onehot)
            over = need > ch
            overflow = over if overflow is None else jnp.logical_or(overflow, over)
        gathered = _dot(jnp.concatenate(pieces, axis=0), hk).astype(BF16)
        for e in range(n_exp):
            xe_ref[0, e, pl.ds(starts[e], ch), :] += gathered[e * ch:(e + 1) * ch, :]

        @pl.when(overflow)
        def _():
            def per_expert(e, carry):
                ws, need = window(e, k)

                def chunk(c, carry2):
                    wsc = pl.multiple_of(ws + c * ch, sub)
                    onehot, v = one_hot([pos_ref[pl.ds(e, 1), r, :][0] for r in tok_rows],
                                        [prob_ref[pl.ds(e, 1), r, :][0] for r in tok_rows], wsc)
                    xe_ref[0, e, pl.ds(wsc, ch), :] += _dot(onehot, hk).astype(BF16)
                    val_ref[0, pl.ds(wsc, ch), :] += jnp.where(expert_lane == e, v, 0.0)
                    return carry2

                lax.fori_loop(1, (need + ch - 1) // ch, chunk, 0)
                return carry

            lax.fori_loop(0, n_exp, per_expert, 0)


def _gather_call(pfx, h2_3, pos3, prob3, *, n_exp, cap):
    bsz, seq, d = h2_3.shape
    nb = seq // TOKEN_BLOCK
    gb = GATHER_BLOCKS
    rows_per_step = gb * TOKEN_BLOCK // V7X_LANES
    slots = cap + GATHER_CHUNK
    grid_spec = pltpu.PrefetchScalarGridSpec(
        num_scalar_prefetch=1,
        grid=(bsz, nb // gb),
        in_specs=[pl.BlockSpec((1, gb * TOKEN_BLOCK, d), lambda b, g, p: (b, g, 0)),
                  pl.BlockSpec((n_exp, rows_per_step, V7X_LANES), lambda b, g, p: (b, g, 0)),
                  pl.BlockSpec((n_exp, rows_per_step, V7X_LANES), lambda b, g, p: (b, g, 0))],
        out_specs=[_resident((1, n_exp, slots, d), lambda b, g, p: (b, 0, 0, 0)),
                   pl.BlockSpec((1, slots, n_exp), lambda b, g, p: (b, 0, 0))],
    )
    return pl.pallas_call(
        functools.partial(_gather_kernel, n_exp=n_exp, nb=nb, cap=cap),
        grid_spec=grid_spec,
        out_shape=[jax.ShapeDtypeStruct((bsz, n_exp, slots, d), BF16),
                   jax.ShapeDtypeStruct((bsz, slots, n_exp), F32)],
        compiler_params=_params(("arbitrary", "arbitrary")),
        name="gather",
    )(pfx, h2_3, pos3, prob3)


def _ffn_kernel(xe_ref, val_ref, wg_ref, wu_ref, wd_ref, ye_ref, acc_ref):
    e = pl.program_id(0)
    f = pl.program_id(1)
    bsz, _, cap, d = xe_ref.shape

    @pl.when(f == 0)
    def _():
        acc_ref[...] = jnp.zeros_like(acc_ref)

    xe = xe_ref[...].reshape(bsz * cap, d)
    a = _dot(xe, wg_ref[0].astype(BF16))
    u = _dot(xe, wu_ref[0].astype(BF16))
    hm = (a * jax.nn.sigmoid(a) * u).astype(BF16)
    acc_ref[...] += _dot(hm, wd_ref[0].astype(BF16))

    @pl.when(f == pl.num_programs(1) - 1)
    def _():
        vals = val_ref[...].reshape(bsz * cap, val_ref.shape[2])
        mine = lax.broadcasted_iota(I32, (1, vals.shape[1]), 1) == e
        val = jnp.sum(jnp.where(mine, vals, 0.0), axis=1, keepdims=True)
        ye_ref[...] = (acc_ref[...] * val).astype(BF16).reshape(bsz, 1, cap, d)


def _ffn_call(xe, val, w_gate, w_up, w_down, *, cap):
    bsz, n_exp, _, d = xe.shape
    d_exp = w_gate.shape[2]
    fc = FFN_CHUNK
    return pl.pallas_call(
        _ffn_kernel,
        grid=(n_exp, d_exp // fc),
        in_specs=[pl.BlockSpec((bsz, 1, cap, d), lambda e, f: (0, e, 0, 0)),
                  pl.BlockSpec((bsz, cap, n_exp), lambda e, f: (0, 0, 0)),
                  pl.BlockSpec((1, d, fc), lambda e, f: (e, 0, f)),
                  pl.BlockSpec((1, d, fc), lambda e, f: (e, 0, f)),
                  pl.BlockSpec((1, fc, d), lambda e, f: (e, f, 0))],
        out_specs=pl.BlockSpec((bsz, 1, cap, d), lambda e, f: (0, e, 0, 0)),
        out_shape=jax.ShapeDtypeStruct((bsz, n_exp, cap, d), BF16),
        scratch_shapes=[pltpu.VMEM((bsz * cap, d), F32)],
        compiler_params=_params(("arbitrary", "arbitrary")),
        name="ffn",
    )(xe, val, w_gate, w_up, w_down)


def _combine_kernel(pfx_ref, x1_ref, ye_ref, post_ref, mod_ref, gfin_ref, o_ref,
                    ycat_ref, scat_ref, acc_ref, *, n_exp, nb, cap):
    b = pl.program_id(0)
    tb = TOKEN_BLOCK
    win = COMBINE_WINDOW
    align = V7X_SUBLANES_BF16
    gate_f = mod_ref[0, 5:6, :]
    lane = lax.broadcasted_iota(I32, (1, V7X_LANES), 1)
    per = V7X_LANES // win

    def finish(rows, moe):
        x2 = x1_ref[rows, :] + gate_f * moe
        ms = jnp.mean(x2 * x2, axis=-1, keepdims=True)
        o_ref[rows, :] = x2 * lax.rsqrt(ms + RMS_EPS) * gfin_ref[...]

    redo = []
    for s in range(COMBINE_BLOCKS):
        k = pl.program_id(1) * COMBINE_BLOCKS + s
        rows = slice(s * tb, (s + 1) * tb)
        p0s, ends, wss = [], [], []
        fits = None
        for e in range(n_exp):
            base = (b * n_exp + e) * (nb + 1) + k
            p0 = pfx_ref[base]
            end = pfx_ref[base + 1]
            ws = pl.multiple_of(jnp.minimum((p0 // align) * align, cap - win), align)
            ok = end <= ws + win
            fits = ok if fits is None else jnp.logical_and(fits, ok)
            p0s.append(p0)
            ends.append(end)
            wss.append(ws)
        for e0 in range(0, n_exp, per):
            target = lane + wss[e0]
            pos_col = post_ref[rows, e0:e0 + 1]
            for j in range(1, per):
                mine = lane >= j * win
                target = jnp.where(mine, lane + (wss[e0 + j] - j * win), target)
                pos_col = jnp.where(mine, post_ref[rows, e0 + j:e0 + j + 1], pos_col)
            g = e0 // per
            scat_ref[s, :, g * V7X_LANES:(g + 1) * V7X_LANES] = (
                jnp.where(pos_col == target, 1.0, 0.0).astype(BF16))
            for j in range(per):
                e = e0 + j
                ycat_ref[s, e * win:(e + 1) * win, :] = ye_ref[0, e, pl.ds(wss[e], win), :]
        finish(rows, _dot(scat_ref[s], ycat_ref[s]))
        redo.append((rows, jnp.logical_not(fits), p0s, ends))

    for rows, misfit, p0s, ends in redo:
        @pl.when(misfit)
        def _():
            acc_ref[...] = jnp.zeros_like(acc_ref)
            cols = lax.broadcasted_iota(I32, (tb, tb), 1)
            for e in range(n_exp):
                ws1 = pl.multiple_of(jnp.minimum((p0s[e] // align) * align, cap - tb), align)
                ws2 = pl.multiple_of(jnp.minimum(ws1 + tb, cap - tb), align)
                pos_col = post_ref[rows, e:e + 1]

                @pl.when(ends[e] > p0s[e])
                def _():
                    onehot = jnp.where(pos_col == cols + ws1, 1.0, 0.0).astype(BF16)
                    acc_ref[...] += _dot(onehot, ye_ref[0, e, pl.ds(ws1, tb), :])

                @pl.when(ends[e] > ws1 + tb)
                def _():
                    c2 = cols + ws2
                    hit = jnp.logical_and(pos_col == c2, c2 >= ws1 + tb)
                    onehot = jnp.where(hit, 1.0, 0.0).astype(BF16)
                    acc_ref[...] += _dot(onehot, ye_ref[0, e, pl.ds(ws2, tb), :])
            finish(rows, acc_ref[...])


def _combine_call(pfx, x1, ye, pos_t, mod3, g_final, *, seq):
    t, d = x1.shape
    bsz, n_exp, cap, _ = ye.shape
    tb = TOKEN_BLOCK
    nb = seq // tb
    step_rows = COMBINE_BLOCKS * tb
    steps = nb // COMBINE_BLOCKS
    row = lambda b, k, p: (b * steps + k, 0)
    grid_spec = pltpu.PrefetchScalarGridSpec(
        num_scalar_prefetch=1,
        grid=(bsz, steps),
        in_specs=[pl.BlockSpec((step_rows, d), row),
                  _resident((1, n_exp, cap, d), lambda b, k, p: (b, 0, 0, 0)),
                  pl.BlockSpec((step_rows, n_exp), row),
                  pl.BlockSpec((1, N_MOD, d), lambda b, k, p: (b, 0, 0)),
                  pl.BlockSpec((1, d), lambda b, k, p: (0, 0))],
        out_specs=pl.BlockSpec((step_rows, d), row),
        scratch_shapes=[pltpu.VMEM((COMBINE_BLOCKS, n_exp * COMBINE_WINDOW, d), BF16),
                        pltpu.VMEM((COMBINE_BLOCKS, tb, n_exp * COMBINE_WINDOW), BF16),
                        pltpu.VMEM((tb, d), F32)],
    )
    return pl.pallas_call(
        functools.partial(_combine_kernel, n_exp=n_exp, nb=nb, cap=cap),
        grid_spec=grid_spec,
        out_shape=jax.ShapeDtypeStruct((t, d), F32),
        compiler_params=_params(("arbitrary", "arbitrary")),
        name="combine",
    )(pfx, x1, ye, pos_t, mod3, g_final)


def _layer(x2, c_t, w_ada, b_ada, g_norm_mix, w_in, b_gate, w_fourier, w_conv, w_conv_out,
           w_o, g_norm_moe, w_router, b_router, w_gate_e, w_up_e, w_down_e, *, bsz, seq):
    t, d = x2.shape
    n_exp = w_router.shape[1]
    cap = EC_CAPACITY * seq // n_exp
    n1 = FFT_N1
    n2 = seq // n1
    fc, f1, f2, cos_t, sin_t = _dft_tables(seq)

    mod = _mod_call(c_t, w_ada, b_ada.reshape(1, -1))
    mod3 = mod.reshape(bsz, N_MOD, d)

    wre, wim, cv, ga, gb = _proj_call(
        x2, mod3, g_norm_mix.reshape(1, d), w_in.astype(BF16), b_gate.reshape(1, -1), fc, w_conv,
        seq=seq)

    tr, ti = _fft1_call(wre, wim, f1, cos_t, sin_t)
    a2 = _fft2_call(tr, ti, f2).reshape(t, D_FOURIER)

    x1, h2, probs_t = _mix_call(
        a2, cv, ga, gb, x2, mod3, w_fourier.astype(BF16), w_conv_out.astype(BF16),
        w_o.astype(BF16), g_norm_moe.reshape(1, d), w_router.T, b_router.reshape(n_exp, 1),
        seq=seq)

    prob3 = probs_t.reshape(bsz * n_exp, seq // V7X_LANES, V7X_LANES)
    pos, raw = _select_call(prob3, cap)
    stride = TOKEN_BLOCK // V7X_LANES
    starts = raw[:, ::stride, 0]
    pfx = jnp.concatenate([starts, jnp.full((bsz * n_exp, 1), cap, I32)], axis=1).reshape(-1)
    pos_t = jnp.transpose(pos.reshape(bsz, n_exp, seq), (0, 2, 1)).reshape(t, n_exp)

    xe, val = _gather_call(pfx, h2.reshape(bsz, seq, d), pos, prob3, n_exp=n_exp, cap=cap)
    ye = _ffn_call(xe, val, w_gate_e, w_up_e, w_down_e, cap=cap)
    return ye, pfx, x1, pos_t, mod3


def kernel(x, c, w_ada, b_ada, g_norm_mix, w_in, b_gate, w_fourier, w_conv, w_conv_out, w_o,
           g_norm_moe, w_router, b_router, w_gate_e, w_up_e, w_down_e, g_final):
    bsz, seq, d = x.shape
    assert w_ada.shape[0] == 1
    assert seq % (FFT_N1 * V7X_SUBLANES_F32) == 0 and seq % ROW_TILE == 0
    assert (EC_CAPACITY * seq // w_router.shape[2]) >= TOKEN_BLOCK
    x2 = x.reshape(bsz * seq, d)
    ye, pfx, x1, pos_t, mod3 = _layer(
        x2, c.T, w_ada[0], b_ada[0], g_norm_mix[0], w_in[0], b_gate[0], w_fourier[0],
        w_conv[0], w_conv_out[0], w_o[0], g_norm_moe[0], w_router[0], b_router[0],
        w_gate_e[0], w_up_e[0], w_down_e[0], bsz=bsz, seq=seq)
    out = _combine_call(pfx, x1, ye, pos_t, mod3, g_final.reshape(1, d), seq=seq)
    return out.reshape(bsz, seq, d)
```

```python
import functools

import numpy as np
import jax
import jax.numpy as jnp
from jax import lax
from jax.experimental import pallas as pl
from jax.experimental.pallas import tpu as pltpu

F32 = jnp.float32
BF16 = jnp.bfloat16
I32 = jnp.int32

FOURIER_GROUPS = 4
FOURIER_GROUP_DIM = 128
D_FOURIER = FOURIER_GROUPS * FOURIER_GROUP_DIM
N_MOD = 6
EC_CAPACITY = 2
RMS_EPS = 1e-6

V7X_LANES = 128
V7X_SUBLANES_F32 = 8
V7X_SUBLANES_BF16 = 16
V7X_VMEM_BYTES = 64 * 1024 * 1024
VMEM_LIMIT_BYTES = V7X_VMEM_BYTES - 6 * 1024 * 1024

MOD_COL_TILE = 1536
MOD_ACC_COLS = 512
ROW_TILE = 1024
FFT_N1 = 128
FFT_N2_TILE = 16
FFT2_K_TILE = 32
TOKEN_BLOCK = 256
GATHER_CHUNK = 64
GATHER_BLOCKS = 8
COMBINE_WINDOW = 64
COMBINE_BLOCKS = 2
FFN_CHUNK = 512


def _dot(a, b):
    return jnp.dot(a, b, preferred_element_type=F32)


def _params(semantics):
    return pltpu.CompilerParams(dimension_semantics=semantics,
                                vmem_limit_bytes=VMEM_LIMIT_BYTES)


def _pitch(rows):
    p = -(-rows // V7X_SUBLANES_F32)
    return (p if p % 2 else p + 1) * V7X_SUBLANES_F32


def _stage_group(stage_ref, group, value):
    rows = value.shape[0]
    row0 = group * _pitch(rows)
    for l in range(stage_ref.shape[0]):
        stage_ref[l, row0:row0 + rows, :] = value[:, l * V7X_LANES:(l + 1) * V7X_LANES]


def _row_of_each_group(stage_ref, row, groups, rows):
    return jnp.concatenate([stage_ref[l, pl.ds(row, groups, stride=_pitch(rows)), :]
                            for l in range(stage_ref.shape[0])], axis=1)


def _stage_scratch(groups, rows, width):
    return pltpu.VMEM((width // V7X_LANES, groups * _pitch(rows), V7X_LANES), F32)


def _resident(block_shape, index_map):
    return pl.BlockSpec(block_shape, index_map, pipeline_mode=pl.Buffered(1))


def _mod_kernel(ct_ref, w_ref, b_ref, o_ref, *, bsz):
    ct = ct_ref[...]
    ct = ct * jax.nn.sigmoid(ct)
    sub = V7X_SUBLANES_F32
    d, tn = w_ref.shape
    cw = MOD_ACC_COLS
    for c0 in range(0, tn, cw):
        accs = [jnp.zeros((sub, cw), F32) for _ in range(bsz)]
        for g in range(d // sub):
            w = w_ref[g * sub:(g + 1) * sub, c0:c0 + cw]
            for r in range(bsz):
                accs[r] = accs[r] + ct[g * sub:(g + 1) * sub, r:r + 1] * w
        for r in range(bsz):
            o_ref[r:r + 1, c0:c0 + cw] = (jnp.sum(accs[r], axis=0, keepdims=True)
                                          + b_ref[:, c0:c0 + cw])


def _mod_call(c_t, w_ada, b_ada):
    d, bsz = c_t.shape
    n = w_ada.shape[1]
    tn = MOD_COL_TILE
    return pl.pallas_call(
        functools.partial(_mod_kernel, bsz=bsz),
        grid=(n // tn,),
        in_specs=[pl.BlockSpec((d, bsz), lambda j: (0, 0)),
                  pl.BlockSpec((d, tn), lambda j: (0, j)),
                  pl.BlockSpec((1, tn), lambda j: (0, j))],
        out_specs=pl.BlockSpec((bsz, tn), lambda j: (0, j)),
        out_shape=jax.ShapeDtypeStruct((bsz, n), F32),
        compiler_params=_params(("arbitrary",)),
        name="mod",
    )(c_t, w_ada, b_ada)


def _rms_modulate(x, g, shift, scale):
    ms = jnp.mean(x * x, axis=-1, keepdims=True)
    return x * lax.rsqrt(ms + RMS_EPS) * (g * (1.0 + scale)) + shift


def _proj_kernel(x_ref, xprev_ref, xnext_ref, mod_ref, g_ref, win_ref, bgate_ref, fc_ref, wconv_ref,
                 wre_ref, wim_ref, cv_ref, ga_ref, gb_ref, re_s, im_s,
                 *, d, d_conv, n2, tm, tpb):
    i = pl.program_id(0)
    shift = mod_ref[0, 0:1, :]
    scale = mod_ref[0, 1:2, :]
    h = _rms_modulate(x_ref[...], g_ref[...], shift, scale).astype(BF16)
    x_halo = jnp.concatenate([xprev_ref[...], xnext_ref[...]], axis=0)
    h_halo = _rms_modulate(x_halo, g_ref[...], shift, scale).astype(BF16)
    h_ext = jnp.concatenate([h, h_halo], axis=0)

    o1 = D_FOURIER
    o2 = o1 + d_conv
    o3 = o2 + d_conv
    o4 = o3 + d_conv
    uf = _dot(h, win_ref[:, 0:o1]).astype(BF16)
    gd = FOURIER_GROUP_DIM
    ws = [_dot(uf[:, g * gd:(g + 1) * gd], fc_ref[...]) for g in range(FOURIER_GROUPS)]
    w_re = jnp.concatenate([w[:, :gd] for w in ws], axis=1)
    w_im = jnp.concatenate([w[:, gd:] for w in ws], axis=1)
    n1_rows = w_re.shape[0] // n2
    for r in range(n1_rows):
        _stage_group(re_s, r, w_re[r * n2:(r + 1) * n2, :])
        _stage_group(im_s, r, w_im[r * n2:(r + 1) * n2, :])
    for j in range(n2):
        cols = slice(j * D_FOURIER, (j + 1) * D_FOURIER)
        wre_ref[0, :, cols] = _row_of_each_group(re_s, j, n1_rows, n2).astype(BF16)
        wim_ref[0, :, cols] = _row_of_each_group(im_s, j, n1_rows, n2).astype(BF16)
    q_ext = _dot(h_ext, win_ref[:, o3:o4]) * _dot(h_ext, win_ref[:, o1:o2])
    q = q_ext[0:tm, :]
    nh = xprev_ref.shape[0]
    first = (i % tpb) == 0
    last = (i % tpb) == tpb - 1
    hp = jnp.where(first, 0.0, q_ext[tm + nh - 1:tm + nh, :])
    hn = jnp.where(last, 0.0, q_ext[tm + nh:tm + nh + 1, :])
    rows = lax.broadcasted_iota(I32, (tm, 1), 0)
    q_prev = jnp.where(rows == 0, hp, pltpu.roll(q, 1, axis=0))
    q_next = jnp.where(rows == tm - 1, hn, pltpu.roll(q, tm - 1, axis=0))
    conv = q_prev * wconv_ref[0:1, :] + q * wconv_ref[1:2, :] + q_next * wconv_ref[2:3, :]
    cv_ref[...] = (_dot(h, win_ref[:, o2:o3]) * conv).astype(BF16)
    ga_ref[...] = jax.nn.sigmoid(_dot(h, win_ref[:, o4:o4 + d]) + bgate_ref[:, 0:d]).astype(BF16)
    gb_ref[...] = jax.nn.sigmoid(
        _dot(h, win_ref[:, o4 + d:o4 + 2 * d]) + bgate_ref[:, d:2 * d]).astype(BF16)


def _proj_call(x2, mod3, g_mix, w_in, b_gate, fc, w_conv, *, seq):
    t, d = x2.shape
    k_in = w_in.shape[1]
    d_conv = (k_in - D_FOURIER - 2 * d) // 3
    tm = ROW_TILE
    tpb = seq // tm
    n2 = seq // FFT_N1
    n1_rows = tm // n2
    bsz = t // seq
    sub = V7X_SUBLANES_F32
    row = lambda i: (i, 0)
    const = lambda i: (0, 0)
    out_sds = lambda n: jax.ShapeDtypeStruct((t, n), BF16)
    assert n1_rows % V7X_SUBLANES_BF16 == 0
    dft_in = jax.ShapeDtypeStruct((bsz, FFT_N1, n2 * D_FOURIER), BF16)
    dft_blk = pl.BlockSpec((1, n1_rows, n2 * D_FOURIER), lambda i: (i // tpb, i % tpb, 0))
    return pl.pallas_call(
        functools.partial(_proj_kernel, d=d, d_conv=d_conv, n2=n2, tm=tm, tpb=tpb),
        grid=(t // tm,),
        in_specs=[pl.BlockSpec((tm, d), row),
                  pl.BlockSpec((sub, d), lambda i: (jnp.maximum(i * (tm // sub) - 1, 0), 0)),
                  pl.BlockSpec((sub, d),
                               lambda i: (jnp.minimum((i + 1) * (tm // sub), t // sub - 1), 0)),
                  pl.BlockSpec((1, N_MOD, d), lambda i: (i // tpb, 0, 0)),
                  pl.BlockSpec((1, d), const),
                  _resident((d, k_in), const),
                  pl.BlockSpec((1, 2 * d), const),
                  pl.BlockSpec(fc.shape, const),
                  pl.BlockSpec(w_conv.shape, const)],
        out_specs=[dft_blk, dft_blk, pl.BlockSpec((tm, d_conv), row),
                   pl.BlockSpec((tm, d), row), pl.BlockSpec((tm, d), row)],
        out_shape=[dft_in, dft_in, out_sds(d_conv), out_sds(d), out_sds(d)],
        scratch_shapes=[_stage_scratch(n1_rows, n2, D_FOURIER)] * 2,
        compiler_params=_params(("parallel",)),
        name="proj",
    )(x2, x2, x2, mod3, g_mix, w_in, b_gate, fc, w_conv)


def _fft1_kernel(wre_ref, wim_ref, f1_ref, cos_ref, sin_ref, tr_ref, ti_ref, re_s, im_s,
                 *, n1, nt):
    reps = D_FOURIER // V7X_LANES
    half = nt // 2
    for h in range(2):
        hcols = slice(h * half * D_FOURIER, (h + 1) * half * D_FOURIER)
        w = jnp.concatenate([wre_ref[0, :, hcols], wim_ref[0, :, hcols]], axis=0)
        t = _dot(f1_ref[...], w)
        for jj in range(half):
            j = h * half + jj
            cols = slice(jj * D_FOURIER, (jj + 1) * D_FOURIER)
            lanes = slice(j * V7X_LANES, (j + 1) * V7X_LANES)
            c = jnp.concatenate([cos_ref[:, lanes]] * reps, axis=1)
            s = jnp.concatenate([sin_ref[:, lanes]] * reps, axis=1)
            a = t[:n1, cols]
            b = t[n1:, cols]
            _stage_group(re_s, j, a * c + b * s)
            _stage_group(im_s, j, b * c - a * s)
    for k in range(n1):
        tr_ref[0, k] = _row_of_each_group(re_s, k, nt, n1).astype(BF16)
        ti_ref[0, k] = _row_of_each_group(im_s, k, nt, n1).astype(BF16)


def _fft1_call(wre3, wim3, f1, cos_t, sin_t):
    bsz, n1, cols = wre3.shape
    n2 = cols // D_FOURIER
    nt = FFT_N2_TILE
    tn = nt * D_FOURIER
    blk = lambda b, j: (b, 0, j)
    oblk = lambda b, j: (b, 0, j, 0)
    sds = jax.ShapeDtypeStruct((bsz, n1, n2, D_FOURIER), BF16)
    return pl.pallas_call(
        functools.partial(_fft1_kernel, n1=n1, nt=nt),
        grid=(bsz, n2 // nt),
        in_specs=[pl.BlockSpec((1, n1, tn), blk), pl.BlockSpec((1, n1, tn), blk),
                  pl.BlockSpec(f1.shape, lambda b, j: (0, 0)),
                  pl.BlockSpec((n1, nt * V7X_LANES), lambda b, j: (0, j)),
                  pl.BlockSpec((n1, nt * V7X_LANES), lambda b, j: (0, j))],
        out_specs=[pl.BlockSpec((1, n1, nt, D_FOURIER), oblk),
                   pl.BlockSpec((1, n1, nt, D_FOURIER), oblk)],
        out_shape=[sds, sds],
        scratch_shapes=[_stage_scratch(nt, n1, D_FOURIER)] * 2,
        compiler_params=_params(("parallel", "parallel")),
        name="fft1",
    )(wre3, wim3, f1, cos_t, sin_t)


def _fft2_kernel(tr_ref, ti_ref, f2_ref, o_ref, res_s, *, kb, n2):
    for k in range(kb):
        rhs = jnp.concatenate([tr_ref[0, k], ti_ref[0, k]], axis=0)
        _stage_group(res_s, k, _dot(f2_ref[...], rhs))
    for k2 in range(n2):
        o_ref[0, k2] = _row_of_each_group(res_s, k2, kb, n2).astype(BF16)


def _fft2_call(tr4, ti4, f2):
    bsz, n1, n2, dfo = tr4.shape
    kb = FFT2_K_TILE
    blk = lambda b, j: (b, j, 0, 0)
    return pl.pallas_call(
        functools.partial(_fft2_kernel, kb=kb, n2=n2),
        grid=(bsz, n1 // kb),
        in_specs=[pl.BlockSpec((1, kb, n2, dfo), blk), pl.BlockSpec((1, kb, n2, dfo), blk),
                  pl.BlockSpec(f2.shape, lambda b, j: (0, 0))],
        out_specs=pl.BlockSpec((1, n2, kb, dfo), lambda b, j: (b, 0, j, 0)),
        out_shape=jax.ShapeDtypeStruct((bsz, n2, n1, dfo), BF16),
        scratch_shapes=[_stage_scratch(kb, n2, dfo)],
        compiler_params=_params(("parallel", "parallel")),
        name="fft2",
    )(tr4, ti4, f2)


def _dft_tables(seq):
    n1 = FFT_N1
    n2 = seq // n1
    gd = FOURIER_GROUP_DIM
    total_scale = 1.0 / np.sqrt(float(seq) * gd)
    s_c = 2.0 ** -4
    s_1 = 2.0 ** -3
    s_2 = total_scale / (s_c * s_1)

    def cs(n):
        ang = 2.0 * np.pi * np.outer(np.arange(n), np.arange(n)) / n
        return np.cos(ang), np.sin(ang)

    cc, sc = cs(gd)
    fc = np.concatenate([cc, -sc], axis=1) * s_c
    c1, s1 = cs(n1)
    f1 = np.block([[c1, s1], [-s1, c1]]) * s_1
    c2, s2 = cs(n2)
    f2 = np.concatenate([c2, s2], axis=1) * s_2
    ang = 2.0 * np.pi * np.outer(np.arange(n1), np.arange(n2)) / seq
    cos_t = np.repeat(np.cos(ang), V7X_LANES, axis=1)
    sin_t = np.repeat(np.sin(ang), V7X_LANES, axis=1)
    as_bf16 = lambda m: jnp.asarray(m, F32).astype(BF16)
    return (as_bf16(fc), as_bf16(f1), as_bf16(f2),
            jnp.asarray(cos_t, F32), jnp.asarray(sin_t, F32))


def _mix_kernel(a_ref, cv_ref, ga_ref, gb_ref, x_ref, mod_ref,
                wf_ref, wco_ref, wo_ref, gmoe_ref, wrt_ref, br_ref,
                x1_ref, h2_ref, pt_ref):
    y_b = _dot(cv_ref[...], wco_ref[...])
    y_a = _dot(a_ref[...], wf_ref[...])
    z = ga_ref[...] * y_a.astype(BF16) + gb_ref[...] * y_b.astype(BF16)
    mix = _dot(z, wo_ref[...])
    gate_m = mod_ref[0, 2:3, :]
    x1 = x_ref[...] + gate_m * mix
    x1_ref[...] = x1

    h2 = _rms_modulate(x1, gmoe_ref[...], mod_ref[0, 3:4, :], mod_ref[0, 4:5, :])
    h2_hi = h2.astype(BF16)
    h2_ref[...] = h2_hi
    h2_lo = (h2 - h2_hi.astype(F32)).astype(BF16)
    w_r = wrt_ref[...]
    w_hi = w_r.astype(BF16)
    w_lo = (w_r - w_hi.astype(F32)).astype(BF16)
    n_exp = w_r.shape[0]
    rows_t = lambda a, b: lax.dot_general(a, b, (((1,), (1,)), ((), ())),
                                          preferred_element_type=F32)
    both = rows_t(jnp.concatenate([w_hi, w_lo], axis=0), h2_hi)
    logits = both[:n_exp] + both[n_exp:] + rows_t(w_hi, h2_lo) + br_ref[...]
    m = jnp.max(logits, axis=0, keepdims=True)
    e = jnp.exp(logits - m)
    probs_t = e / jnp.sum(e, axis=0, keepdims=True)
    pt_ref[0] = probs_t


def _mix_call(a2, cv, ga, gb, x2, mod3, wf, wco, wo, g_moe, wr_t, b_r, *, seq):
    t, d = x2.shape
    n_exp = wr_t.shape[0]
    tm = ROW_TILE
    tpb = seq // tm
    bsz = t // seq
    row = lambda i: (i, 0)
    const = lambda i: (0, 0)
    return pl.pallas_call(
        _mix_kernel,
        grid=(t // tm,),
        in_specs=[pl.BlockSpec((tm, a2.shape[1]), row),
                  pl.BlockSpec((tm, cv.shape[1]), row),
                  pl.BlockSpec((tm, d), row), pl.BlockSpec((tm, d), row),
                  pl.BlockSpec((tm, d), row),
                  pl.BlockSpec((1, N_MOD, d), lambda i: (i // tpb, 0, 0)),
                  _resident(wf.shape, const), _resident(wco.shape, const),
                  _resident(wo.shape, const),
                  pl.BlockSpec((1, d), const),
                  pl.BlockSpec(wr_t.shape, const), pl.BlockSpec(b_r.shape, const)],
        out_specs=[pl.BlockSpec((tm, d), row), pl.BlockSpec((tm, d), row),
                   pl.BlockSpec((1, n_exp, tm), lambda i: (i // tpb, 0, i % tpb))],
        out_shape=[jax.ShapeDtypeStruct((t, d), F32), jax.ShapeDtypeStruct((t, d), BF16),
                   jax.ShapeDtypeStruct((bsz, n_exp, seq), F32)],
        compiler_params=_params(("parallel",)),
        name="mix",
    )(a2, cv, ga, gb, x2, mod3, wf, wco, wo, g_moe, wr_t, b_r)


def _excl_cumsum(mask_f, upper, lower_strict):
    r, rows, lanes = mask_f.shape
    m2 = mask_f.reshape(r * rows, lanes).astype(BF16)
    incl = _dot(m2, upper)
    tot = jnp.broadcast_to(incl[:, lanes - 1:lanes], incl.shape).astype(BF16)
    tot3 = tot.reshape(r, rows, lanes)
    offs = [_dot(lower_strict, tot3[j]) for j in range(r)]
    off = jnp.stack(offs, axis=0)
    return incl.reshape(r, rows, lanes) - mask_f + off


def _select_kernel(p_ref, upper_ref, lower_ref, pos_ref, raw_ref, *, cap):
    p = p_ref[...]
    r = p.shape[0]

    def count(mask):
        c = jnp.sum(mask.astype(F32), axis=2, keepdims=True)
        return jnp.sum(c, axis=1, keepdims=True)

    prefix = jnp.zeros(p.shape, I32)
    for bit in range(29, -1, -1):
        cand = prefix | (1 << bit)
        keep = count(p >= pltpu.bitcast(cand, F32)) >= cap
        prefix = jnp.where(keep, cand, prefix)
    thr = pltpu.bitcast(prefix, F32)
    gt = p > thr
    eq = (p == thr).astype(F32)
    need = cap - count(gt)
    upper = upper_ref[...]
    lower = lower_ref[...]
    rank_eq = _excl_cumsum(eq, upper, lower)
    sel = jnp.where(gt, 1.0, jnp.where(rank_eq < need, eq, 0.0))
    raw = _excl_cumsum(sel, upper, lower)
    raw_i = raw.astype(I32)
    raw_ref[...] = raw_i
    pos_ref[...] = jnp.where(sel > 0.0, raw_i, -1)


def _select_call(p3, cap):
    r, rows, lanes = p3.shape
    upper = jnp.asarray(np.triu(np.ones((lanes, lanes))), BF16)
    lower = jnp.asarray(np.tril(np.ones((rows, rows)), -1), BF16)
    full = lambda shape: pl.BlockSpec(shape, lambda i: (0,) * len(shape))
    sds = jax.ShapeDtypeStruct(p3.shape, I32)
    return pl.pallas_call(
        functools.partial(_select_kernel, cap=cap),
        grid=(1,),
        in_specs=[full(p3.shape), full(upper.shape), full(lower.shape)],
        out_specs=[full(p3.shape), full(p3.shape)],
        out_shape=[sds, sds],
        compiler_params=_params(("arbitrary",)),
        name="select",
    )(p3, upper, lower)


def _gather_kernel(pfx_ref, h2_ref, pos_ref, prob_ref, xe_ref, val_ref, *, n_exp, nb, cap):
    b = pl.program_id(0)
    g = pl.program_id(1)
    tb = TOKEN_BLOCK
    ch = GATHER_CHUNK
    halves = tb // V7X_LANES
    sub = V7X_SUBLANES_BF16

    @pl.when(g == 0)
    def _():
        xe_ref[...] = jnp.zeros_like(xe_ref)
        val_ref[...] = jnp.zeros_like(val_ref)

    chunk_rows = lax.broadcasted_iota(I32, (ch, V7X_LANES), 0)
    expert_lane = lax.broadcasted_iota(I32, (1, n_exp), 1)

    def window(e, k):
        base = (b * n_exp + e) * (nb + 1) + k
        p0 = pfx_ref[base]
        start = pl.multiple_of((p0 // sub) * sub, sub)
        return start, pfx_ref[base + 1] - start

    def one_hot(pos_rows, prob_rows, ws):
        rows = chunk_rows + ws
        hits = [jnp.where(p == rows, 1.0, 0.0) for p in pos_rows]
        onehot = jnp.concatenate([h.astype(BF16) for h in hits], axis=1)
        v = hits[0] * prob_rows[0]
        for h, p in zip(hits[1:], prob_rows[1:]):
            v = v + h * p
        return onehot, jnp.sum(v, axis=1, keepdims=True)

    for j in range(GATHER_BLOCKS):
        k = g * GATHER_BLOCKS + j
        hk = h2_ref[0, j * tb:(j + 1) * tb, :]
        tok_rows = [slice(halves * j + h, halves * j + h + 1) for h in range(halves)]
        starts, pieces = [], []
        overflow = None
        for e in range(n_exp):
            ws, need = window(e, k)
            onehot, v = one_hot([pos_ref[e, r, :] for r in tok_rows],
                                [prob_ref[e, r, :] for r in tok_rows], ws)
            val_ref[0, pl.ds(ws, ch), e:e + 1] += v
            starts.append(ws)
            pieces.append(onehot)
            over = need > ch
            overflow = over if overflow is None else jnp.logical_or(overflow, over)
        gathered = _dot(jnp.concatenate(pieces, axis=0), hk).astype(BF16)
        for e in range(n_exp):
            xe_ref[0, e, pl.ds(starts[e], ch), :] += gathered[e * ch:(e + 1) * ch, :]

        @pl.when(overflow)
        def _():
            def per_expert(e, carry):
                ws, need = window(e, k)

                def chunk(c, carry2):
                    wsc = pl.multiple_of(ws + c * ch, sub)
                    onehot, v = one_hot([pos_ref[pl.ds(e, 1), r, :][0] for r in tok_rows],
                                        [prob_ref[pl.ds(e, 1), r, :][0] for r in tok_rows], wsc)
                    xe_ref[0, e, pl.ds(wsc, ch), :] += _dot(onehot, hk).astype(BF16)
                    val_ref[0, pl.ds(wsc, ch), :] += jnp.where(expert_lane == e, v, 0.0)
                    return carry2

                lax.fori_loop(1, (need + ch - 1) // ch, chunk, 0)
                return carry

            lax.fori_loop(0, n_exp, per_expert, 0)


def _gather_call(pfx, h2_3, pos3, prob3, *, n_exp, cap):
    bsz, seq, d = h2_3.shape
    nb = seq // TOKEN_BLOCK
    gb = GATHER_BLOCKS
    rows_per_step = gb * TOKEN_BLOCK // V7X_LANES
    slots = cap + GATHER_CHUNK
    grid_spec = pltpu.PrefetchScalarGridSpec(
        num_scalar_prefetch=1,
        grid=(bsz, nb // gb),
        in_specs=[pl.BlockSpec((1, gb * TOKEN_BLOCK, d), lambda b, g, p: (b, g, 0)),
                  pl.BlockSpec((n_exp, rows_per_step, V7X_LANES), lambda b, g, p: (b, g, 0)),
                  pl.BlockSpec((n_exp, rows_per_step, V7X_LANES), lambda b, g, p: (b, g, 0))],
        out_specs=[_resident((1, n_exp, slots, d), lambda b, g, p: (b, 0, 0, 0)),
                   pl.BlockSpec((1, slots, n_exp), lambda b, g, p: (b, 0, 0))],
    )
    return pl.pallas_call(
        functools.partial(_gather_kernel, n_exp=n_exp, nb=nb, cap=cap),
        grid_spec=grid_spec,
        out_shape=[jax.ShapeDtypeStruct((bsz, n_exp, slots, d), BF16),
                   jax.ShapeDtypeStruct((bsz, slots, n_exp), F32)],
        compiler_params=_params(("arbitrary", "arbitrary")),
        name="gather",
    )(pfx, h2_3, pos3, prob3)


def _ffn_kernel(xe_ref, val_ref, wg_ref, wu_ref, wd_ref, ye_ref, acc_ref):
    e = pl.program_id(0)
    f = pl.program_id(1)
    bsz, _, cap, d = xe_ref.shape

    @pl.when(f == 0)
    def _():
        acc_ref[...] = jnp.zeros_like(acc_ref)

    xe = xe_ref[...].reshape(bsz * cap, d)
    a = _dot(xe, wg_ref[0].astype(BF16))
    u = _dot(xe, wu_ref[0].astype(BF16))
    hm = (a * jax.nn.sigmoid(a) * u).astype(BF16)
    acc_ref[...] += _dot(hm, wd_ref[0].astype(BF16))

    @pl.when(f == pl.num_programs(1) - 1)
    def _():
        vals = val_ref[...].reshape(bsz * cap, val_ref.shape[2])
        mine = lax.broadcasted_iota(I32, (1, vals.shape[1]), 1) == e
        val = jnp.sum(jnp.where(mine, vals, 0.0), axis=1, keepdims=True)
        ye_ref[...] = (acc_ref[...] * val).astype(BF16).reshape(bsz, 1, cap, d)


def _ffn_call(xe, val, w_gate, w_up, w_down, *, cap):
    bsz, n_exp, _, d = xe.shape
    d_exp = w_gate.shape[2]
    fc = FFN_CHUNK
    return pl.pallas_call(
        _ffn_kernel,
        grid=(n_exp, d_exp // fc),
        in_specs=[pl.BlockSpec((bsz, 1, cap, d), lambda e, f: (0, e, 0, 0)),
                  pl.BlockSpec((bsz, cap, n_exp), lambda e, f: (0, 0, 0)),
                  pl.BlockSpec((1, d, fc), lambda e, f: (e, 0, f)),
                  pl.BlockSpec((1, d, fc), lambda e, f: (e, 0, f)),
                  pl.BlockSpec((1, fc, d), lambda e, f: (e, f, 0))],
        out_specs=pl.BlockSpec((bsz, 1, cap, d), lambda e, f: (0, e, 0, 0)),
        out_shape=jax.ShapeDtypeStruct((bsz, n_exp, cap, d), BF16),
        scratch_shapes=[pltpu.VMEM((bsz * cap, d), F32)],
        compiler_params=_params(("arbitrary", "arbitrary")),
        name="ffn",
    )(xe, val, w_gate, w_up, w_down)


def _combine_kernel(pfx_ref, x1_ref, ye_ref, post_ref, mod_ref, gfin_ref, o_ref,
                    ycat_ref, scat_ref, acc_ref, *, n_exp, nb, cap):
    b = pl.program_id(0)
    tb = TOKEN_BLOCK
    win = COMBINE_WINDOW
    align = V7X_SUBLANES_BF16
    gate_f = mod_ref[0, 5:6, :]
    lane = lax.broadcasted_iota(I32, (1, V7X_LANES), 1)
    per = V7X_LANES // win

    def finish(rows, moe):
        x2 = x1_ref[rows, :] + gate_f * moe
        ms = jnp.mean(x2 * x2, axis=-1, keepdims=True)
        o_ref[rows, :] = x2 * lax.rsqrt(ms + RMS_EPS) * gfin_ref[...]

    redo = []
    for s in range(COMBINE_BLOCKS):
        k = pl.program_id(1) * COMBINE_BLOCKS + s
        rows = slice(s * tb, (s + 1) * tb)
        p0s, ends, wss = [], [], []
        fits = None
        for e in range(n_exp):
            base = (b * n_exp + e) * (nb + 1) + k
            p0 = pfx_ref[base]
            end = pfx_ref[base + 1]
            ws = pl.multiple_of(jnp.minimum((p0 // align) * align, cap - win), align)
            ok = end <= ws + win
            fits = ok if fits is None else jnp.logical_and(fits, ok)
            p0s.append(p0)
            ends.append(end)
            wss.append(ws)
        for e0 in range(0, n_exp, per):
            target = lane + wss[e0]
            pos_col = post_ref[rows, e0:e0 + 1]
            for j in range(1, per):
                mine = lane >= j * win
                target = jnp.where(mine, lane + (wss[e0 + j] - j * win), target)
                pos_col = jnp.where(mine, post_ref[rows, e0 + j:e0 + j + 1], pos_col)
            g = e0 // per
            scat_ref[s, :, g * V7X_LANES:(g + 1) * V7X_LANES] = (
                jnp.where(pos_col == target, 1.0, 0.0).astype(BF16))
            for j in range(per):
                e = e0 + j
                ycat_ref[s, e * win:(e + 1) * win, :] = ye_ref[0, e, pl.ds(wss[e], win), :]
        finish(rows, _dot(scat_ref[s], ycat_ref[s]))
        redo.append((rows, jnp.logical_not(fits), p0s, ends))

    for rows, misfit, p0s, ends in redo:
        @pl.when(misfit)
        def _():
            acc_ref[...] = jnp.zeros_like(acc_ref)
            cols = lax.broadcasted_iota(I32, (tb, tb), 1)
            for e in range(n_exp):
                ws1 = pl.multiple_of(jnp.minimum((p0s[e] // align) * align, cap - tb), align)
                ws2 = pl.multiple_of(jnp.minimum(ws1 + tb, cap - tb), align)
                pos_col = post_ref[rows, e:e + 1]

                @pl.when(ends[e] > p0s[e])
                def _():
                    onehot = jnp.where(pos_col == cols + ws1, 1.0, 0.0).astype(BF16)
                    acc_ref[...] += _dot(onehot, ye_ref[0, e, pl.ds(ws1, tb), :])

                @pl.when(ends[e] > ws1 + tb)
                def _():
                    c2 = cols + ws2
                    hit = jnp.logical_and(pos_col == c2, c2 >= ws1 + tb)
                    onehot = jnp.where(hit, 1.0, 0.0).astype(BF16)
                    acc_ref[...] += _dot(onehot, ye_ref[0, e, pl.ds(ws2, tb), :])
            finish(rows, acc_ref[...])


def _combine_call(pfx, x1, ye, pos_t, mod3, g_final, *, seq):
    t, d = x1.shape
    bsz, n_exp, cap, _ = ye.shape
    tb = TOKEN_BLOCK
    nb = seq // tb
    step_rows = COMBINE_BLOCKS * tb
    steps = nb // COMBINE_BLOCKS
    row = lambda b, k, p: (b * steps + k, 0)
    grid_spec = pltpu.PrefetchScalarGridSpec(
        num_scalar_prefetch=1,
        grid=(bsz, steps),
        in_specs=[pl.BlockSpec((step_rows, d), row),
                  _resident((1, n_exp, cap, d), lambda b, k, p: (b, 0, 0, 0)),
                  pl.BlockSpec((step_rows, n_exp), row),
                  pl.BlockSpec((1, N_MOD, d), lambda b, k, p: (b, 0, 0)),
                  pl.BlockSpec((1, d), lambda b, k, p: (0, 0))],
        out_specs=pl.BlockSpec((step_rows, d), row),
        scratch_shapes=[pltpu.VMEM((COMBINE_BLOCKS, n_exp * COMBINE_WINDOW, d), BF16),
                        pltpu.VMEM((COMBINE_BLOCKS, tb, n_exp * COMBINE_WINDOW), BF16),
                        pltpu.VMEM((tb, d), F32)],
    )
    return pl.pallas_call(
        functools.partial(_combine_kernel, n_exp=n_exp, nb=nb, cap=cap),
        grid_spec=grid_spec,
        out_shape=jax.ShapeDtypeStruct((t, d), F32),
        compiler_params=_params(("arbitrary", "arbitrary")),
        name="combine",
    )(pfx, x1, ye, pos_t, mod3, g_final)


def _layer(x2, c_t, w_ada, b_ada, g_norm_mix, w_in, b_gate, w_fourier, w_conv, w_conv_out,
           w_o, g_norm_moe, w_router, b_router, w_gate_e, w_up_e, w_down_e, *, bsz, seq):
    t, d = x2.shape
    n_exp = w_router.shape[1]
    cap = EC_CAPACITY * seq // n_exp
    n1 = FFT_N1
    n2 = seq // n1
    fc, f1, f2, cos_t, sin_t = _dft_tables(seq)

    mod = _mod_call(c_t, w_ada, b_ada.reshape(1, -1))
    mod3 = mod.reshape(bsz, N_MOD, d)

    wre, wim, cv, ga, gb = _proj_call(
        x2, mod3, g_norm_mix.reshape(1, d), w_in.astype(BF16), b_gate.reshape(1, -1), fc, w_conv,
        seq=seq)

    tr, ti = _fft1_call(wre, wim, f1, cos_t, sin_t)
    a2 = _fft2_call(tr, ti, f2).reshape(t, D_FOURIER)

    x1, h2, probs_t = _mix_call(
        a2, cv, ga, gb, x2, mod3, w_fourier.astype(BF16), w_conv_out.astype(BF16),
        w_o.astype(BF16), g_norm_moe.reshape(1, d), w_router.T, b_router.reshape(n_exp, 1),
        seq=seq)

    prob3 = probs_t.reshape(bsz * n_exp, seq // V7X_LANES, V7X_LANES)
    pos, raw = _select_call(prob3, cap)
    stride = TOKEN_BLOCK // V7X_LANES
    starts = raw[:, ::stride, 0]
    pfx = jnp.concatenate([starts, jnp.full((bsz * n_exp, 1), cap, I32)], axis=1).reshape(-1)
    pos_t = jnp.transpose(pos.reshape(bsz, n_exp, seq), (0, 2, 1)).reshape(t, n_exp)

    xe, val = _gather_call(pfx, h2.reshape(bsz, seq, d), pos, prob3, n_exp=n_exp, cap=cap)
    ye = _ffn_call(xe, val, w_gate_e, w_up_e, w_down_e, cap=cap)
    return ye, pfx, x1, pos_t, mod3


def kernel(x, c, w_ada, b_ada, g_norm_mix, w_in, b_gate, w_fourier, w_conv, w_conv_out, w_o,
           g_norm_moe, w_router, b_router, w_gate_e, w_up_e, w_down_e, g_final):
    bsz, seq, d = x.shape
    assert w_ada.shape[0] == 1
    assert seq % (FFT_N1 * V7X_SUBLANES_F32) == 0 and seq % ROW_TILE == 0
    assert (EC_CAPACITY * seq // w_router.shape[2]) >= TOKEN_BLOCK
    x2 = x.reshape(bsz * seq, d)
    ye, pfx, x1, pos_t, mod3 = _layer(
        x2, c.T, w_ada[0], b_ada[0], g_norm_mix[0], w_in[0], b_gate[0], w_fourier[0],
        w_conv[0], w_conv_out[0], w_o[0], g_norm_moe[0], w_router[0], b_router[0],
        w_gate_e[0], w_up_e[0], w_down_e[0], bsz=bsz, seq=seq)
    out = _combine_call(pfx, x1, ye, pos_t, mod3, g_final.reshape(1, d), seq=seq)
    return out.reshape(bsz, seq, d)
```

```python
import functools

import numpy as np
import jax
import jax.numpy as jnp
from jax import lax
from jax.experimental import pallas as pl
from jax.experimental.pallas import tpu as pltpu

F32 = jnp.float32
BF16 = jnp.bfloat16
I32 = jnp.int32

FOURIER_GROUPS = 4
FOURIER_GROUP_DIM = 128
D_FOURIER = FOURIER_GROUPS * FOURIER_GROUP_DIM
N_MOD = 6
EC_CAPACITY = 2
RMS_EPS = 1e-6

V7X_LANES = 128
V7X_SUBLANES_F32 = 8
V7X_SUBLANES_BF16 = 16
V7X_VMEM_BYTES = 64 * 1024 * 1024
VMEM_LIMIT_BYTES = V7X_VMEM_BYTES - 6 * 1024 * 1024

MOD_COL_TILE = 1536
MOD_ACC_COLS = 512
ROW_TILE = 1024
FFT_N1 = 128
FFT_N2_TILE = 16
FFT2_K_TILE = 32
TOKEN_BLOCK = 256
GATHER_CHUNK = 64
GATHER_BLOCKS = 8
COMBINE_WINDOW = 64
COMBINE_BLOCKS = 2
FFN_CHUNK = 512


def _dot(a, b):
    return jnp.dot(a, b, preferred_element_type=F32)


def _sigmoid(x):
    return 0.5 * jnp.tanh(0.5 * x) + 0.5


def _params(semantics):
    return pltpu.CompilerParams(dimension_semantics=semantics,
                                vmem_limit_bytes=VMEM_LIMIT_BYTES)


def _pitch(rows):
    p = -(-rows // V7X_SUBLANES_F32)
    return (p if p % 2 else p + 1) * V7X_SUBLANES_F32


def _stage_group(stage_ref, group, value):
    rows = value.shape[0]
    row0 = group * _pitch(rows)
    for l in range(stage_ref.shape[0]):
        stage_ref[l, row0:row0 + rows, :] = value[:, l * V7X_LANES:(l + 1) * V7X_LANES]


def _row_of_each_group(stage_ref, row, groups, rows):
    return jnp.concatenate([stage_ref[l, pl.ds(row, groups, stride=_pitch(rows)), :]
                            for l in range(stage_ref.shape[0])], axis=1)


def _stage_scratch(groups, rows, width):
    return pltpu.VMEM((width // V7X_LANES, groups * _pitch(rows), V7X_LANES), F32)


def _resident(block_shape, index_map):
    return pl.BlockSpec(block_shape, index_map, pipeline_mode=pl.Buffered(1))


def _mod_kernel(ct_ref, w_ref, b_ref, o_ref, *, bsz):
    ct = ct_ref[...]
    ct = ct * jax.nn.sigmoid(ct)
    sub = V7X_SUBLANES_F32
    d, tn = w_ref.shape
    cw = MOD_ACC_COLS
    for c0 in range(0, tn, cw):
        accs = [jnp.zeros((sub, cw), F32) for _ in range(bsz)]
        for g in range(d // sub):
            w = w_ref[g * sub:(g + 1) * sub, c0:c0 + cw]
            for r in range(bsz):
                accs[r] = accs[r] + ct[g * sub:(g + 1) * sub, r:r + 1] * w
        for r in range(bsz):
            o_ref[r:r + 1, c0:c0 + cw] = (jnp.sum(accs[r], axis=0, keepdims=True)
                                          + b_ref[:, c0:c0 + cw])


def _mod_call(c_t, w_ada, b_ada):
    d, bsz = c_t.shape
    n = w_ada.shape[1]
    tn = MOD_COL_TILE
    return pl.pallas_call(
        functools.partial(_mod_kernel, bsz=bsz),
        grid=(n // tn,),
        in_specs=[pl.BlockSpec((d, bsz), lambda j: (0, 0)),
                  pl.BlockSpec((d, tn), lambda j: (0, j)),
                  pl.BlockSpec((1, tn), lambda j: (0, j))],
        out_specs=pl.BlockSpec((bsz, tn), lambda j: (0, j)),
        out_shape=jax.ShapeDtypeStruct((bsz, n), F32),
        compiler_params=_params(("arbitrary",)),
        name="mod",
    )(c_t, w_ada, b_ada)


def _rms_modulate(x, g, shift, scale):
    ms = jnp.mean(x * x, axis=-1, keepdims=True)
    return x * lax.rsqrt(ms + RMS_EPS) * (g * (1.0 + scale)) + shift


def _proj_kernel(x_ref, xprev_ref, xnext_ref, mod_ref, g_ref, win_ref, bgate_ref, fc_ref, wconv_ref,
                 wre_ref, wim_ref, cv_ref, ga_ref, gb_ref, re_s, im_s,
                 *, d, d_conv, n2, tm, tpb):
    i = pl.program_id(0)
    shift = mod_ref[0, 0:1, :]
    scale = mod_ref[0, 1:2, :]
    h = _rms_modulate(x_ref[...], g_ref[...], shift, scale).astype(BF16)
    x_halo = jnp.concatenate([xprev_ref[...], xnext_ref[...]], axis=0)
    h_halo = _rms_modulate(x_halo, g_ref[...], shift, scale).astype(BF16)
    h_ext = jnp.concatenate([h, h_halo], axis=0)

    o1 = D_FOURIER
    o2 = o1 + d_conv
    o3 = o2 + d_conv
    o4 = o3 + d_conv
    uf = _dot(h, win_ref[:, 0:o1]).astype(BF16)
    gd = FOURIER_GROUP_DIM
    ws = [_dot(uf[:, g * gd:(g + 1) * gd], fc_ref[...]) for g in range(FOURIER_GROUPS)]
    w_re = jnp.concatenate([w[:, :gd] for w in ws], axis=1)
    w_im = jnp.concatenate([w[:, gd:] for w in ws], axis=1)
    n1_rows = w_re.shape[0] // n2
    for r in range(n1_rows):
        _stage_group(re_s, r, w_re[r * n2:(r + 1) * n2, :])
        _stage_group(im_s, r, w_im[r * n2:(r + 1) * n2, :])
    for j in range(n2):
        cols = slice(j * D_FOURIER, (j + 1) * D_FOURIER)
        wre_ref[0, :, cols] = _row_of_each_group(re_s, j, n1_rows, n2).astype(BF16)
        wim_ref[0, :, cols] = _row_of_each_group(im_s, j, n1_rows, n2).astype(BF16)
    q_ext = _dot(h_ext, win_ref[:, o3:o4]) * _dot(h_ext, win_ref[:, o1:o2])
    q = q_ext[0:tm, :]
    nh = xprev_ref.shape[0]
    first = (i % tpb) == 0
    last = (i % tpb) == tpb - 1
    hp = jnp.where(first, 0.0, q_ext[tm + nh - 1:tm + nh, :])
    hn = jnp.where(last, 0.0, q_ext[tm + nh:tm + nh + 1, :])
    rows = lax.broadcasted_iota(I32, (tm, 1), 0)
    q_prev = jnp.where(rows == 0, hp, pltpu.roll(q, 1, axis=0))
    q_next = jnp.where(rows == tm - 1, hn, pltpu.roll(q, tm - 1, axis=0))
    conv = q_prev * wconv_ref[0:1, :] + q * wconv_ref[1:2, :] + q_next * wconv_ref[2:3, :]
    cv_ref[...] = (_dot(h, win_ref[:, o2:o3]) * conv).astype(BF16)
    ga_ref[...] = _sigmoid(_dot(h, win_ref[:, o4:o4 + d]) + bgate_ref[:, 0:d]).astype(BF16)
    gb_ref[...] = _sigmoid(
        _dot(h, win_ref[:, o4 + d:o4 + 2 * d]) + bgate_ref[:, d:2 * d]).astype(BF16)


def _proj_call(x2, mod3, g_mix, w_in, b_gate, fc, w_conv, *, seq):
    t, d = x2.shape
    k_in = w_in.shape[1]
    d_conv = (k_in - D_FOURIER - 2 * d) // 3
    tm = ROW_TILE
    tpb = seq // tm
    n2 = seq // FFT_N1
    n1_rows = tm // n2
    bsz = t // seq
    sub = V7X_SUBLANES_F32
    row = lambda i: (i, 0)
    const = lambda i: (0, 0)
    out_sds = lambda n: jax.ShapeDtypeStruct((t, n), BF16)
    assert n1_rows % V7X_SUBLANES_BF16 == 0
    dft_in = jax.ShapeDtypeStruct((bsz, FFT_N1, n2 * D_FOURIER), BF16)
    dft_blk = pl.BlockSpec((1, n1_rows, n2 * D_FOURIER), lambda i: (i // tpb, i % tpb, 0))
    return pl.pallas_call(
        functools.partial(_proj_kernel, d=d, d_conv=d_conv, n2=n2, tm=tm, tpb=tpb),
        grid=(t // tm,),
        in_specs=[pl.BlockSpec((tm, d), row),
                  pl.BlockSpec((sub, d), lambda i: (jnp.maximum(i * (tm // sub) - 1, 0), 0)),
                  pl.BlockSpec((sub, d),
                               lambda i: (jnp.minimum((i + 1) * (tm // sub), t // sub - 1), 0)),
                  pl.BlockSpec((1, N_MOD, d), lambda i: (i // tpb, 0, 0)),
                  pl.BlockSpec((1, d), const),
                  _resident((d, k_in), const),
                  pl.BlockSpec((1, 2 * d), const),
                  pl.BlockSpec(fc.shape, const),
                  pl.BlockSpec(w_conv.shape, const)],
        out_specs=[dft_blk, dft_blk, pl.BlockSpec((tm, d_conv), row),
                   pl.BlockSpec((tm, d), row), pl.BlockSpec((tm, d), row)],
        out_shape=[dft_in, dft_in, out_sds(d_conv), out_sds(d), out_sds(d)],
        scratch_shapes=[_stage_scratch(n1_rows, n2, D_FOURIER)] * 2,
        compiler_params=_params(("parallel",)),
        name="proj",
    )(x2, x2, x2, mod3, g_mix, w_in, b_gate, fc, w_conv)


def _fft1_kernel(wre_ref, wim_ref, f1_ref, cos_ref, sin_ref, tr_ref, ti_ref, re_s, im_s,
                 *, n1, nt):
    reps = D_FOURIER // V7X_LANES
    half = nt // 2
    for h in range(2):
        hcols = slice(h * half * D_FOURIER, (h + 1) * half * D_FOURIER)
        w = jnp.concatenate([wre_ref[0, :, hcols], wim_ref[0, :, hcols]], axis=0)
        t = _dot(f1_ref[...], w)
        for jj in range(half):
            j = h * half + jj
            cols = slice(jj * D_FOURIER, (jj + 1) * D_FOURIER)
            lanes = slice(j * V7X_LANES, (j + 1) * V7X_LANES)
            c = jnp.concatenate([cos_ref[:, lanes]] * reps, axis=1)
            s = jnp.concatenate([sin_ref[:, lanes]] * reps, axis=1)
            a = t[:n1, cols]
            b = t[n1:, cols]
            _stage_group(re_s, j, a * c + b * s)
            _stage_group(im_s, j, b * c - a * s)
    for k in range(n1):
        tr_ref[0, k] = _row_of_each_group(re_s, k, nt, n1).astype(BF16)
        ti_ref[0, k] = _row_of_each_group(im_s, k, nt, n1).astype(BF16)


def _fft1_call(wre3, wim3, f1, cos_t, sin_t):
    bsz, n1, cols = wre3.shape
    n2 = cols // D_FOURIER
    nt = FFT_N2_TILE
    tn = nt * D_FOURIER
    blk = lambda b, j: (b, 0, j)
    oblk = lambda b, j: (b, 0, j, 0)
    sds = jax.ShapeDtypeStruct((bsz, n1, n2, D_FOURIER), BF16)
    return pl.pallas_call(
        functools.partial(_fft1_kernel, n1=n1, nt=nt),
        grid=(bsz, n2 // nt),
        in_specs=[pl.BlockSpec((1, n1, tn), blk), pl.BlockSpec((1, n1, tn), blk),
                  pl.BlockSpec(f1.shape, lambda b, j: (0, 0)),
                  pl.BlockSpec((n1, nt * V7X_LANES), lambda b, j: (0, j)),
                  pl.BlockSpec((n1, nt * V7X_LANES), lambda b, j: (0, j))],
        out_specs=[pl.BlockSpec((1, n1, nt, D_FOURIER), oblk),
                   pl.BlockSpec((1, n1, nt, D_FOURIER), oblk)],
        out_shape=[sds, sds],
        scratch_shapes=[_stage_scratch(nt, n1, D_FOURIER)] * 2,
        compiler_params=_params(("parallel", "parallel")),
        name="fft1",
    )(wre3, wim3, f1, cos_t, sin_t)


def _fft2_kernel(tr_ref, ti_ref, f2_ref, o_ref, res_s, *, kb, n2):
    for k in range(kb):
        rhs = jnp.concatenate([tr_ref[0, k], ti_ref[0, k]], axis=0)
        _stage_group(res_s, k, _dot(f2_ref[...], rhs))
    for k2 in range(n2):
        o_ref[0, k2] = _row_of_each_group(res_s, k2, kb, n2).astype(BF16)


def _fft2_call(tr4, ti4, f2):
    bsz, n1, n2, dfo = tr4.shape
    kb = FFT2_K_TILE
    blk = lambda b, j: (b, j, 0, 0)
    return pl.pallas_call(
        functools.partial(_fft2_kernel, kb=kb, n2=n2),
        grid=(bsz, n1 // kb),
        in_specs=[pl.BlockSpec((1, kb, n2, dfo), blk), pl.BlockSpec((1, kb, n2, dfo), blk),
                  pl.BlockSpec(f2.shape, lambda b, j: (0, 0))],
        out_specs=pl.BlockSpec((1, n2, kb, dfo), lambda b, j: (b, 0, j, 0)),
        out_shape=jax.ShapeDtypeStruct((bsz, n2, n1, dfo), BF16),
        scratch_shapes=[_stage_scratch(kb, n2, dfo)],
        compiler_params=_params(("parallel", "parallel")),
        name="fft2",
    )(tr4, ti4, f2)


def _dft_tables(seq):
    n1 = FFT_N1
    n2 = seq // n1
    gd = FOURIER_GROUP_DIM
    total_scale = 1.0 / np.sqrt(float(seq) * gd)
    s_c = 2.0 ** -4
    s_1 = 2.0 ** -3
    s_2 = total_scale / (s_c * s_1)

    def cs(n):
        ang = 2.0 * np.pi * np.outer(np.arange(n), np.arange(n)) / n
        return np.cos(ang), np.sin(ang)

    cc, sc = cs(gd)
    fc = np.concatenate([cc, -sc], axis=1) * s_c
    c1, s1 = cs(n1)
    f1 = np.block([[c1, s1], [-s1, c1]]) * s_1
    c2, s2 = cs(n2)
    f2 = np.concatenate([c2, s2], axis=1) * s_2
    ang = 2.0 * np.pi * np.outer(np.arange(n1), np.arange(n2)) / seq
    cos_t = np.repeat(np.cos(ang), V7X_LANES, axis=1)
    sin_t = np.repeat(np.sin(ang), V7X_LANES, axis=1)
    as_bf16 = lambda m: jnp.asarray(m, F32).astype(BF16)
    return (as_bf16(fc), as_bf16(f1), as_bf16(f2),
            jnp.asarray(cos_t, F32), jnp.asarray(sin_t, F32))


def _mix_kernel(a_ref, cv_ref, ga_ref, gb_ref, x_ref, mod_ref,
                wf_ref, wco_ref, wo_ref, gmoe_ref, wrt_ref, br_ref,
                x1_ref, h2_ref, pt_ref):
    y_b = _dot(cv_ref[...], wco_ref[...])
    y_a = _dot(a_ref[...], wf_ref[...])
    z = ga_ref[...] * y_a.astype(BF16) + gb_ref[...] * y_b.astype(BF16)
    mix = _dot(z, wo_ref[...])
    gate_m = mod_ref[0, 2:3, :]
    x1 = x_ref[...] + gate_m * mix
    x1_ref[...] = x1

    h2 = _rms_modulate(x1, gmoe_ref[...], mod_ref[0, 3:4, :], mod_ref[0, 4:5, :])
    h2_hi = h2.astype(BF16)
    h2_ref[...] = h2_hi
    h2_lo = (h2 - h2_hi.astype(F32)).astype(BF16)
    w_r = wrt_ref[...]
    w_hi = w_r.astype(BF16)
    w_lo = (w_r - w_hi.astype(F32)).astype(BF16)
    n_exp = w_r.shape[0]
    rows_t = lambda a, b: lax.dot_general(a, b, (((1,), (1,)), ((), ())),
                                          preferred_element_type=F32)
    both = rows_t(jnp.concatenate([w_hi, w_lo], axis=0), h2_hi)
    logits = both[:n_exp] + both[n_exp:] + rows_t(w_hi, h2_lo) + br_ref[...]
    m = jnp.max(logits, axis=0, keepdims=True)
    e = jnp.exp(logits - m)
    probs_t = e / jnp.sum(e, axis=0, keepdims=True)
    pt_ref[0] = probs_t


def _mix_call(a2, cv, ga, gb, x2, mod3, wf, wco, wo, g_moe, wr_t, b_r, *, seq):
    t, d = x2.shape
    n_exp = wr_t.shape[0]
    tm = ROW_TILE
    tpb = seq // tm
    bsz = t // seq
    row = lambda i: (i, 0)
    const = lambda i: (0, 0)
    return pl.pallas_call(
        _mix_kernel,
        grid=(t // tm,),
        in_specs=[pl.BlockSpec((tm, a2.shape[1]), row),
                  pl.BlockSpec((tm, cv.shape[1]), row),
                  pl.BlockSpec((tm, d), row), pl.BlockSpec((tm, d), row),
                  pl.BlockSpec((tm, d), row),
                  pl.BlockSpec((1, N_MOD, d), lambda i: (i // tpb, 0, 0)),
                  _resident(wf.shape, const), _resident(wco.shape, const),
                  _resident(wo.shape, const),
                  pl.BlockSpec((1, d), const),
                  pl.BlockSpec(wr_t.shape, const), pl.BlockSpec(b_r.shape, const)],
        out_specs=[pl.BlockSpec((tm, d), row), pl.BlockSpec((tm, d), row),
                   pl.BlockSpec((1, n_exp, tm), lambda i: (i // tpb, 0, i % tpb))],
        out_shape=[jax.ShapeDtypeStruct((t, d), F32), jax.ShapeDtypeStruct((t, d), BF16),
                   jax.ShapeDtypeStruct((bsz, n_exp, seq), F32)],
        compiler_params=_params(("parallel",)),
        name="mix",
    )(a2, cv, ga, gb, x2, mod3, wf, wco, wo, g_moe, wr_t, b_r)


def _excl_cumsum(mask_f, upper, lower_strict):
    r, rows, lanes = mask_f.shape
    m2 = mask_f.reshape(r * rows, lanes).astype(BF16)
    incl = _dot(m2, upper)
    tot = jnp.broadcast_to(incl[:, lanes - 1:lanes], incl.shape).astype(BF16)
    tot3 = tot.reshape(r, rows, lanes)
    offs = [_dot(lower_strict, tot3[j]) for j in range(r)]
    off = jnp.stack(offs, axis=0)
    return incl.reshape(r, rows, lanes) - mask_f + off


def _select_kernel(p_ref, upper_ref, lower_ref, pos_ref, raw_ref, *, cap):
    p = p_ref[...]
    r = p.shape[0]

    def count(mask):
        c = jnp.sum(mask.astype(F32), axis=2, keepdims=True)
        return jnp.sum(c, axis=1, keepdims=True)

    prefix = jnp.zeros(p.shape, I32)
    for bit in range(29, -1, -1):
        cand = prefix | (1 << bit)
        keep = count(p >= pltpu.bitcast(cand, F32)) >= cap
        prefix = jnp.where(keep, cand, prefix)
    thr = pltpu.bitcast(prefix, F32)
    gt = p > thr
    eq = (p == thr).astype(F32)
    need = cap - count(gt)
    upper = upper_ref[...]
    lower = lower_ref[...]
    rank_eq = _excl_cumsum(eq, upper, lower)
    sel = jnp.where(gt, 1.0, jnp.where(rank_eq < need, eq, 0.0))
    raw = _excl_cumsum(sel, upper, lower)
    raw_i = raw.astype(I32)
    raw_ref[...] = raw_i
    pos_ref[...] = jnp.where(sel > 0.0, raw_i, -1)


def _select_call(p3, cap):
    r, rows, lanes = p3.shape
    upper = jnp.asarray(np.triu(np.ones((lanes, lanes))), BF16)
    lower = jnp.asarray(np.tril(np.ones((rows, rows)), -1), BF16)
    full = lambda shape: pl.BlockSpec(shape, lambda i: (0,) * len(shape))
    sds = jax.ShapeDtypeStruct(p3.shape, I32)
    return pl.pallas_call(
        functools.partial(_select_kernel, cap=cap),
        grid=(1,),
        in_specs=[full(p3.shape), full(upper.shape), full(lower.shape)],
        out_specs=[full(p3.shape), full(p3.shape)],
        out_shape=[sds, sds],
        compiler_params=_params(("arbitrary",)),
        name="select",
    )(p3, upper, lower)


def _gather_kernel(pfx_ref, h2_ref, pos_ref, prob_ref, xe_ref, val_ref, *, n_exp, nb, cap):
    b = pl.program_id(0)
    g = pl.program_id(1)
    tb = TOKEN_BLOCK
    ch = GATHER_CHUNK
    halves = tb // V7X_LANES
    sub = V7X_SUBLANES_BF16

    @pl.when(g == 0)
    def _():
        xe_ref[...] = jnp.zeros_like(xe_ref)
        val_ref[...] = jnp.zeros_like(val_ref)

    chunk_rows = lax.broadcasted_iota(I32, (ch, V7X_LANES), 0)
    expert_lane = lax.broadcasted_iota(I32, (1, n_exp), 1)

    def window(e, k):
        base = (b * n_exp + e) * (nb + 1) + k
        p0 = pfx_ref[base]
        start = pl.multiple_of(p0 & -sub, sub)
        return start, pfx_ref[base + 1] - start

    def one_hot(pos_rows, prob_rows, ws):
        rows = chunk_rows + ws
        hits = [jnp.where(p == rows, 1.0, 0.0) for p in pos_rows]
        onehot = jnp.concatenate([h.astype(BF16) for h in hits], axis=1)
        v = hits[0] * prob_rows[0]
        for h, p in zip(hits[1:], prob_rows[1:]):
            v = v + h * p
        return onehot, jnp.sum(v, axis=1, keepdims=True)

    for j in range(GATHER_BLOCKS):
        k = g * GATHER_BLOCKS + j
        hk = h2_ref[0, j * tb:(j + 1) * tb, :]
        tok_rows = [slice(halves * j + h, halves * j + h + 1) for h in range(halves)]
        starts, pieces = [], []
        overflow = None
        for e in range(n_exp):
            ws, need = window(e, k)
            onehot, v = one_hot([pos_ref[e, r, :] for r in tok_rows],
                                [prob_ref[e, r, :] for r in tok_rows], ws)
            val_ref[0, pl.ds(ws, ch), e:e + 1] += v
            starts.append(ws)
            pieces.append(onehot)
            over = need > ch
            overflow = over if overflow is None else jnp.logical_or(overflow, over)
        gathered = _dot(jnp.concatenate(pieces, axis=0), hk).astype(BF16)
        for e in range(n_exp):
            xe_ref[0, e, pl.ds(starts[e], ch), :] += gathered[e * ch:(e + 1) * ch, :]

        @pl.when(overflow)
        def _():
            def per_expert(e, carry):
                ws, need = window(e, k)

                def chunk(c, carry2):
                    wsc = pl.multiple_of(ws + c * ch, sub)
                    onehot, v = one_hot([pos_ref[pl.ds(e, 1), r, :][0] for r in tok_rows],
                                        [prob_ref[pl.ds(e, 1), r, :][0] for r in tok_rows], wsc)
                    xe_ref[0, e, pl.ds(wsc, ch), :] += _dot(onehot, hk).astype(BF16)
                    val_ref[0, pl.ds(wsc, ch), :] += jnp.where(expert_lane == e, v, 0.0)
                    return carry2

                lax.fori_loop(1, (need + ch - 1) // ch, chunk, 0)
                return carry

            lax.fori_loop(0, n_exp, per_expert, 0)


def _gather_call(pfx, h2_3, pos3, prob3, *, n_exp, cap):
    bsz, seq, d = h2_3.shape
    nb = seq // TOKEN_BLOCK
    gb = GATHER_BLOCKS
    rows_per_step = gb * TOKEN_BLOCK // V7X_LANES
    slots = cap + GATHER_CHUNK
    grid_spec = pltpu.PrefetchScalarGridSpec(
        num_scalar_prefetch=1,
        grid=(bsz, nb // gb),
        in_specs=[pl.BlockSpec((1, gb * TOKEN_BLOCK, d), lambda b, g, p: (b, g, 0)),
                  pl.BlockSpec((n_exp, rows_per_step, V7X_LANES), lambda b, g, p: (b, g, 0)),
                  pl.BlockSpec((n_exp, rows_per_step, V7X_LANES), lambda b, g, p: (b, g, 0))],
        out_specs=[_resident((1, n_exp, slots, d), lambda b, g, p: (b, 0, 0, 0)),
                   pl.BlockSpec((1, slots, n_exp), lambda b, g, p: (b, 0, 0))],
    )
    return pl.pallas_call(
        functools.partial(_gather_kernel, n_exp=n_exp, nb=nb, cap=cap),
        grid_spec=grid_spec,
        out_shape=[jax.ShapeDtypeStruct((bsz, n_exp, slots, d), BF16),
                   jax.ShapeDtypeStruct((bsz, slots, n_exp), F32)],
        compiler_params=_params(("arbitrary", "arbitrary")),
        name="gather",
    )(pfx, h2_3, pos3, prob3)


def _ffn_kernel(xe_ref, val_ref, wg_ref, wu_ref, wd_ref, ye_ref, acc_ref):
    e = pl.program_id(0)
    f = pl.program_id(1)
    bsz, _, cap, d = xe_ref.shape

    @pl.when(f == 0)
    def _():
        acc_ref[...] = jnp.zeros_like(acc_ref)

    xe = xe_ref[...].reshape(bsz * cap, d)
    a = _dot(xe, wg_ref[0].astype(BF16))
    u = _dot(xe, wu_ref[0].astype(BF16))
    half_a = 0.5 * a
    hm = (half_a * (jnp.tanh(half_a) + 1.0) * u).astype(BF16)
    acc_ref[...] += _dot(hm, wd_ref[0].astype(BF16))

    @pl.when(f == pl.num_programs(1) - 1)
    def _():
        vals = val_ref[...].reshape(bsz * cap, val_ref.shape[2])
        mine = lax.broadcasted_iota(I32, (1, vals.shape[1]), 1) == e
        val = jnp.sum(jnp.where(mine, vals, 0.0), axis=1, keepdims=True)
        ye_ref[...] = (acc_ref[...] * val).astype(BF16).reshape(bsz, 1, cap, d)


def _ffn_call(xe, val, w_gate, w_up, w_down, *, cap):
    bsz, n_exp, _, d = xe.shape
    d_exp = w_gate.shape[2]
    fc = FFN_CHUNK
    return pl.pallas_call(
        _ffn_kernel,
        grid=(n_exp, d_exp // fc),
        in_specs=[pl.BlockSpec((bsz, 1, cap, d), lambda e, f: (0, e, 0, 0)),
                  pl.BlockSpec((bsz, cap, n_exp), lambda e, f: (0, 0, 0)),
                  pl.BlockSpec((1, d, fc), lambda e, f: (e, 0, f)),
                  pl.BlockSpec((1, d, fc), lambda e, f: (e, 0, f)),
                  pl.BlockSpec((1, fc, d), lambda e, f: (e, f, 0))],
        out_specs=pl.BlockSpec((bsz, 1, cap, d), lambda e, f: (0, e, 0, 0)),
        out_shape=jax.ShapeDtypeStruct((bsz, n_exp, cap, d), BF16),
        scratch_shapes=[pltpu.VMEM((bsz * cap, d), F32)],
        compiler_params=_params(("arbitrary", "arbitrary")),
        name="ffn",
    )(xe, val, w_gate, w_up, w_down)


def _combine_kernel(pfx_ref, x1_ref, ye_ref, post_ref, posq_ref, spread_ref, mod_ref, gfin_ref,
                    o_ref, ycat_ref, scat_ref, acc_ref, *, n_exp, nb, cap):
    b = pl.program_id(0)
    tb = TOKEN_BLOCK
    win = COMBINE_WINDOW
    align = V7X_SUBLANES_BF16
    gate_f = mod_ref[0, 5:6, :]
    lane = lax.broadcasted_iota(I32, (1, V7X_LANES), 1)
    per = V7X_LANES // win

    def finish(rows, moe):
        x2 = x1_ref[rows, :] + gate_f * moe
        ms = jnp.mean(x2 * x2, axis=-1, keepdims=True)
        o_ref[rows, :] = x2 * lax.rsqrt(ms + RMS_EPS) * gfin_ref[...]

    redo = []
    for s in range(COMBINE_BLOCKS):
        k = pl.program_id(1) * COMBINE_BLOCKS + s
        rows = slice(s * tb, (s + 1) * tb)
        p0s, ends, wss = [], [], []
        fits = None
        for e in range(n_exp):
            base = (b * n_exp + e) * (nb + 1) + k
            p0 = pfx_ref[base]
            end = pfx_ref[base + 1]
            ws = pl.multiple_of(jnp.minimum(p0 & -align, cap - win), align)
            ok = end <= ws + win
            fits = ok if fits is None else jnp.logical_and(fits, ok)
            p0s.append(p0)
            ends.append(end)
            wss.append(ws)
        slot_of_col = _dot(posq_ref[rows, :], spread_ref[...])
        targets = []
        for e0 in range(0, n_exp, per):
            target = lane + (wss[e0] + 1)
            for j in range(1, per):
                target = jnp.where(lane >= j * win, lane + (wss[e0 + j] + 1 - j * win), target)
            targets.append(target)
            for j in range(per):
                e = e0 + j
                ycat_ref[s, e * win:(e + 1) * win, :] = ye_ref[0, e, pl.ds(wss[e], win), :]
        target_row = jnp.concatenate(targets, axis=1).astype(F32)
        scat_ref[s] = jnp.where(slot_of_col == target_row, 1.0, 0.0).astype(BF16)
        finish(rows, _dot(scat_ref[s], ycat_ref[s]))
        redo.append((rows, jnp.logical_not(fits), p0s, ends))

    for rows, misfit, p0s, ends in redo:
        @pl.when(misfit)
        def _():
            acc_ref[...] = jnp.zeros_like(acc_ref)
            cols = lax.broadcasted_iota(I32, (tb, tb), 1)
            for e in range(n_exp):
                ws1 = pl.multiple_of(jnp.minimum(p0s[e] & -align, cap - tb), align)
                ws2 = pl.multiple_of(jnp.minimum(ws1 + tb, cap - tb), align)
                pos_col = post_ref[rows, e:e + 1]

                @pl.when(ends[e] > p0s[e])
                def _():
                    onehot = jnp.where(pos_col == cols + ws1, 1.0, 0.0).astype(BF16)
                    acc_ref[...] += _dot(onehot, ye_ref[0, e, pl.ds(ws1, tb), :])

                @pl.when(ends[e] > ws1 + tb)
                def _():
                    c2 = cols + ws2
                    hit = jnp.logical_and(pos_col == c2, c2 >= ws1 + tb)
                    onehot = jnp.where(hit, 1.0, 0.0).astype(BF16)
                    acc_ref[...] += _dot(onehot, ye_ref[0, e, pl.ds(ws2, tb), :])
            finish(rows, acc_ref[...])


def _combine_call(pfx, x1, ye, pos_t, mod3, g_final, *, seq):
    t, d = x1.shape
    bsz, n_exp, cap, _ = ye.shape
    tb = TOKEN_BLOCK
    nb = seq // tb
    step_rows = COMBINE_BLOCKS * tb
    steps = nb // COMBINE_BLOCKS
    row = lambda b, k, p: (b * steps + k, 0)
    digit_bits = 5
    digit = 1 << digit_bits
    assert cap <= digit * digit
    slot1 = pos_t + 1
    posq = jnp.concatenate([slot1 >> digit_bits, slot1 & (digit - 1)], axis=1).astype(BF16)
    spread = np.kron(np.eye(n_exp), np.ones((1, COMBINE_WINDOW)))
    spread = jnp.asarray(np.concatenate([digit * spread, spread], axis=0), F32).astype(BF16)
    grid_spec = pltpu.PrefetchScalarGridSpec(
        num_scalar_prefetch=1,
        grid=(bsz, steps),
        in_specs=[pl.BlockSpec((step_rows, d), row),
                  _resident((1, n_exp, cap, d), lambda b, k, p: (b, 0, 0, 0)),
                  pl.BlockSpec((step_rows, n_exp), row),
                  pl.BlockSpec((step_rows, 2 * n_exp), row),
                  pl.BlockSpec(spread.shape, lambda b, k, p: (0, 0)),
                  pl.BlockSpec((1, N_MOD, d), lambda b, k, p: (b, 0, 0)),
                  pl.BlockSpec((1, d), lambda b, k, p: (0, 0))],
        out_specs=pl.BlockSpec((step_rows, d), row),
        scratch_shapes=[pltpu.VMEM((COMBINE_BLOCKS, n_exp * COMBINE_WINDOW, d), BF16),
                        pltpu.VMEM((COMBINE_BLOCKS, tb, n_exp * COMBINE_WINDOW), BF16),
                        pltpu.VMEM((tb, d), F32)],
    )
    return pl.pallas_call(
        functools.partial(_combine_kernel, n_exp=n_exp, nb=nb, cap=cap),
        grid_spec=grid_spec,
        out_shape=jax.ShapeDtypeStruct((t, d), F32),
        compiler_params=_params(("arbitrary", "arbitrary")),
        name="combine",
    )(pfx, x1, ye, pos_t, posq, spread, mod3, g_final)


def _layer(x2, c_t, w_ada, b_ada, g_norm_mix, w_in, b_gate, w_fourier, w_conv, w_conv_out,
           w_o, g_norm_moe, w_router, b_router, w_gate_e, w_up_e, w_down_e, *, bsz, seq):
    t, d = x2.shape
    n_exp = w_router.shape[1]
    cap = EC_CAPACITY * seq // n_exp
    n1 = FFT_N1
    n2 = seq // n1
    fc, f1, f2, cos_t, sin_t = _dft_tables(seq)

    mod = _mod_call(c_t, w_ada, b_ada.reshape(1, -1))
    mod3 = mod.reshape(bsz, N_MOD, d)

    wre, wim, cv, ga, gb = _proj_call(
        x2, mod3, g_norm_mix.reshape(1, d), w_in.astype(BF16), b_gate.reshape(1, -1), fc, w_conv,
        seq=seq)

    tr, ti = _fft1_call(wre, wim, f1, cos_t, sin_t)
    a2 = _fft2_call(tr, ti, f2).reshape(t, D_FOURIER)

    x1, h2, probs_t = _mix_call(
        a2, cv, ga, gb, x2, mod3, w_fourier.astype(BF16), w_conv_out.astype(BF16),
        w_o.astype(BF16), g_norm_moe.reshape(1, d), w_router.T, b_router.reshape(n_exp, 1),
        seq=seq)

    prob3 = probs_t.reshape(bsz * n_exp, seq // V7X_LANES, V7X_LANES)
    pos, raw = _select_call(prob3, cap)
    stride = TOKEN_BLOCK // V7X_LANES
    starts = raw[:, ::stride, 0]
    pfx = jnp.concatenate([starts, jnp.full((bsz * n_exp, 1), cap, I32)], axis=1).reshape(-1)
    pos_t = jnp.transpose(pos.reshape(bsz, n_exp, seq), (0, 2, 1)).reshape(t, n_exp)

    xe, val = _gather_call(pfx, h2.reshape(bsz, seq, d), pos, prob3, n_exp=n_exp, cap=cap)
    ye = _ffn_call(xe, val, w_gate_e, w_up_e, w_down_e, cap=cap)
    return ye, pfx, x1, pos_t, mod3


def kernel(x, c, w_ada, b_ada, g_norm_mix, w_in, b_gate, w_fourier, w_conv, w_conv_out, w_o,
           g_norm_moe, w_router, b_router, w_gate_e, w_up_e, w_down_e, g_final):
    bsz, seq, d = x.shape
    assert w_ada.shape[0] == 1
    assert seq % (FFT_N1 * V7X_SUBLANES_F32) == 0 and seq % ROW_TILE == 0
    assert (EC_CAPACITY * seq // w_router.shape[2]) >= TOKEN_BLOCK
    x2 = x.reshape(bsz * seq, d)
    ye, pfx, x1, pos_t, mod3 = _layer(
        x2, c.T, w_ada[0], b_ada[0], g_norm_mix[0], w_in[0], b_gate[0], w_fourier[0],
        w_conv[0], w_conv_out[0], w_o[0], g_norm_moe[0], w_router[0], b_router[0],
        w_gate_e[0], w_up_e[0], w_down_e[0], bsz=bsz, seq=seq)
    out = _combine_call(pfx, x1, ye, pos_t, mod3, g_final.reshape(1, d), seq=seq)
    return out.reshape(bsz, seq, d)
```

```python
import functools

import numpy as np
import jax
import jax.numpy as jnp
from jax import lax
from jax.experimental import pallas as pl
from jax.experimental.pallas import tpu as pltpu

F32 = jnp.float32
BF16 = jnp.bfloat16
I32 = jnp.int32

FOURIER_GROUPS = 4
FOURIER_GROUP_DIM = 128
D_FOURIER = FOURIER_GROUPS * FOURIER_GROUP_DIM
N_MOD = 6
EC_CAPACITY = 2
RMS_EPS = 1e-6

V7X_LANES = 128
V7X_SUBLANES_F32 = 8
V7X_SUBLANES_BF16 = 16
V7X_VMEM_BYTES = 64 * 1024 * 1024
VMEM_LIMIT_BYTES = V7X_VMEM_BYTES - 6 * 1024 * 1024

MOD_COL_TILE = 1536
MOD_ACC_COLS = 512
ROW_TILE = 1024
FFT_N1 = 128
FFT_N2_TILE = 16
FFT2_K_TILE = 32
TOKEN_BLOCK = 256
GATHER_CHUNK = 64
GATHER_BLOCKS = 8
COMBINE_WINDOW = 64
COMBINE_BLOCKS = 2
SLOT_DIGIT_BITS = 5
FFN_CHUNK = 512


def _dot(a, b):
    return jnp.dot(a, b, preferred_element_type=F32)


def _sigmoid(x):
    return 0.5 * jnp.tanh(0.5 * x) + 0.5


def _params(semantics):
    return pltpu.CompilerParams(dimension_semantics=semantics,
                                vmem_limit_bytes=VMEM_LIMIT_BYTES)


def _pitch(rows):
    p = -(-rows // V7X_SUBLANES_F32)
    return (p if p % 2 else p + 1) * V7X_SUBLANES_F32


def _stage_group(stage_ref, group, value):
    rows = value.shape[0]
    row0 = group * _pitch(rows)
    for l in range(stage_ref.shape[0]):
        stage_ref[l, row0:row0 + rows, :] = value[:, l * V7X_LANES:(l + 1) * V7X_LANES]


def _row_of_each_group(stage_ref, row, groups, rows):
    return jnp.concatenate([stage_ref[l, pl.ds(row, groups, stride=_pitch(rows)), :]
                            for l in range(stage_ref.shape[0])], axis=1)


def _stage_scratch(groups, rows, width):
    return pltpu.VMEM((width // V7X_LANES, groups * _pitch(rows), V7X_LANES), F32)


def _resident(block_shape, index_map):
    return pl.BlockSpec(block_shape, index_map, pipeline_mode=pl.Buffered(1))


def _mod_kernel(ct_ref, w_ref, b_ref, o_ref, *, bsz):
    ct = ct_ref[...]
    ct = ct * jax.nn.sigmoid(ct)
    sub = V7X_SUBLANES_F32
    d, tn = w_ref.shape
    cw = MOD_ACC_COLS
    for c0 in range(0, tn, cw):
        accs = [jnp.zeros((sub, cw), F32) for _ in range(bsz)]
        for g in range(d // sub):
            w = w_ref[g * sub:(g + 1) * sub, c0:c0 + cw]
            for r in range(bsz):
                accs[r] = accs[r] + ct[g * sub:(g + 1) * sub, r:r + 1] * w
        for r in range(bsz):
            o_ref[r:r + 1, c0:c0 + cw] = (jnp.sum(accs[r], axis=0, keepdims=True)
                                          + b_ref[:, c0:c0 + cw])


def _mod_call(c_t, w_ada, b_ada):
    d, bsz = c_t.shape
    n = w_ada.shape[1]
    tn = MOD_COL_TILE
    return pl.pallas_call(
        functools.partial(_mod_kernel, bsz=bsz),
        grid=(n // tn,),
        in_specs=[pl.BlockSpec((d, bsz), lambda j: (0, 0)),
                  pl.BlockSpec((d, tn), lambda j: (0, j)),
                  pl.BlockSpec((1, tn), lambda j: (0, j))],
        out_specs=pl.BlockSpec((bsz, tn), lambda j: (0, j)),
        out_shape=jax.ShapeDtypeStruct((bsz, n), F32),
        compiler_params=_params(("arbitrary",)),
        name="mod",
    )(c_t, w_ada, b_ada)


def _rms_modulate(x, g, shift, scale):
    ms = jnp.mean(x * x, axis=-1, keepdims=True)
    return x * lax.rsqrt(ms + RMS_EPS) * (g * (1.0 + scale)) + shift


def _proj_kernel(x_ref, xprev_ref, xnext_ref, mod_ref, g_ref, win_ref, bgate_ref, fc_ref, wconv_ref,
                 wre_ref, wim_ref, cv_ref, ga_ref, gb_ref, re_s, im_s,
                 *, d, d_conv, n2, tm, tpb):
    i = pl.program_id(0)
    shift = mod_ref[0, 0:1, :]
    scale = mod_ref[0, 1:2, :]
    h = _rms_modulate(x_ref[...], g_ref[...], shift, scale).astype(BF16)
    x_halo = jnp.concatenate([xprev_ref[...], xnext_ref[...]], axis=0)
    h_halo = _rms_modulate(x_halo, g_ref[...], shift, scale).astype(BF16)
    h_ext = jnp.concatenate([h, h_halo], axis=0)

    o1 = D_FOURIER
    o2 = o1 + d_conv
    o3 = o2 + d_conv
    o4 = o3 + d_conv
    uf = _dot(h, win_ref[:, 0:o1]).astype(BF16)
    gd = FOURIER_GROUP_DIM
    ws = [_dot(uf[:, g * gd:(g + 1) * gd], fc_ref[...]) for g in range(FOURIER_GROUPS)]
    w_re = jnp.concatenate([w[:, :gd] for w in ws], axis=1)
    w_im = jnp.concatenate([w[:, gd:] for w in ws], axis=1)
    n1_rows = w_re.shape[0] // n2
    for r in range(n1_rows):
        _stage_group(re_s, r, w_re[r * n2:(r + 1) * n2, :])
        _stage_group(im_s, r, w_im[r * n2:(r + 1) * n2, :])
    for j in range(n2):
        cols = slice(j * D_FOURIER, (j + 1) * D_FOURIER)
        wre_ref[0, :, cols] = _row_of_each_group(re_s, j, n1_rows, n2).astype(BF16)
        wim_ref[0, :, cols] = _row_of_each_group(im_s, j, n1_rows, n2).astype(BF16)
    q_ext = _dot(h_ext, win_ref[:, o3:o4]) * _dot(h_ext, win_ref[:, o1:o2])
    q = q_ext[0:tm, :]
    nh = xprev_ref.shape[0]
    first = (i % tpb) == 0
    last = (i % tpb) == tpb - 1
    hp = jnp.where(first, 0.0, q_ext[tm + nh - 1:tm + nh, :])
    hn = jnp.where(last, 0.0, q_ext[tm + nh:tm + nh + 1, :])
    rows = lax.broadcasted_iota(I32, (tm, 1), 0)
    q_prev = jnp.where(rows == 0, hp, pltpu.roll(q, 1, axis=0))
    q_next = jnp.where(rows == tm - 1, hn, pltpu.roll(q, tm - 1, axis=0))
    conv = q_prev * wconv_ref[0:1, :] + q * wconv_ref[1:2, :] + q_next * wconv_ref[2:3, :]
    cv_ref[...] = (_dot(h, win_ref[:, o2:o3]) * conv).astype(BF16)
    ga_ref[...] = _sigmoid(_dot(h, win_ref[:, o4:o4 + d]) + bgate_ref[:, 0:d]).astype(BF16)
    gb_ref[...] = _sigmoid(
        _dot(h, win_ref[:, o4 + d:o4 + 2 * d]) + bgate_ref[:, d:2 * d]).astype(BF16)


def _proj_call(x2, mod3, g_mix, w_in, b_gate, fc, w_conv, *, seq):
    t, d = x2.shape
    k_in = w_in.shape[1]
    d_conv = (k_in - D_FOURIER - 2 * d) // 3
    tm = ROW_TILE
    tpb = seq // tm
    n2 = seq // FFT_N1
    n1_rows = tm // n2
    bsz = t // seq
    sub = V7X_SUBLANES_F32
    row = lambda i: (i, 0)
    const = lambda i: (0, 0)
    out_sds = lambda n: jax.ShapeDtypeStruct((t, n), BF16)
    assert n1_rows % V7X_SUBLANES_BF16 == 0
    dft_in = jax.ShapeDtypeStruct((bsz, FFT_N1, n2 * D_FOURIER), BF16)
    dft_blk = pl.BlockSpec((1, n1_rows, n2 * D_FOURIER), lambda i: (i // tpb, i % tpb, 0))
    return pl.pallas_call(
        functools.partial(_proj_kernel, d=d, d_conv=d_conv, n2=n2, tm=tm, tpb=tpb),
        grid=(t // tm,),
        in_specs=[pl.BlockSpec((tm, d), row),
                  pl.BlockSpec((sub, d), lambda i: (jnp.maximum(i * (tm // sub) - 1, 0), 0)),
                  pl.BlockSpec((sub, d),
                               lambda i: (jnp.minimum((i + 1) * (tm // sub), t // sub - 1), 0)),
                  pl.BlockSpec((1, N_MOD, d), lambda i: (i // tpb, 0, 0)),
                  pl.BlockSpec((1, d), const),
                  _resident((d, k_in), const),
                  pl.BlockSpec((1, 2 * d), const),
                  pl.BlockSpec(fc.shape, const),
                  pl.BlockSpec(w_conv.shape, const)],
        out_specs=[dft_blk, dft_blk, pl.BlockSpec((tm, d_conv), row),
                   pl.BlockSpec((tm, d), row), pl.BlockSpec((tm, d), row)],
        out_shape=[dft_in, dft_in, out_sds(d_conv), out_sds(d), out_sds(d)],
        scratch_shapes=[_stage_scratch(n1_rows, n2, D_FOURIER)] * 2,
        compiler_params=_params(("parallel",)),
        name="proj",
    )(x2, x2, x2, mod3, g_mix, w_in, b_gate, fc, w_conv)


def _fft1_kernel(wre_ref, wim_ref, f1_ref, cos_ref, sin_ref, tr_ref, ti_ref, re_s, im_s,
                 *, n1, nt):
    reps = D_FOURIER // V7X_LANES
    half = nt // 2
    for h in range(2):
        hcols = slice(h * half * D_FOURIER, (h + 1) * half * D_FOURIER)
        w = jnp.concatenate([wre_ref[0, :, hcols], wim_ref[0, :, hcols]], axis=0)
        t = _dot(f1_ref[...], w)
        for jj in range(half):
            j = h * half + jj
            cols = slice(jj * D_FOURIER, (jj + 1) * D_FOURIER)
            lanes = slice(j * V7X_LANES, (j + 1) * V7X_LANES)
            c = jnp.concatenate([cos_ref[:, lanes]] * reps, axis=1)
            s = jnp.concatenate([sin_ref[:, lanes]] * reps, axis=1)
            a = t[:n1, cols]
            b = t[n1:, cols]
            _stage_group(re_s, j, a * c + b * s)
            _stage_group(im_s, j, b * c - a * s)
    for k in range(n1):
        tr_ref[0, k] = _row_of_each_group(re_s, k, nt, n1).astype(BF16)
        ti_ref[0, k] = _row_of_each_group(im_s, k, nt, n1).astype(BF16)


def _fft1_call(wre3, wim3, f1, cos_t, sin_t):
    bsz, n1, cols = wre3.shape
    n2 = cols // D_FOURIER
    nt = FFT_N2_TILE
    tn = nt * D_FOURIER
    blk = lambda b, j: (b, 0, j)
    oblk = lambda b, j: (b, 0, j, 0)
    sds = jax.ShapeDtypeStruct((bsz, n1, n2, D_FOURIER), BF16)
    return pl.pallas_call(
        functools.partial(_fft1_kernel, n1=n1, nt=nt),
        grid=(bsz, n2 // nt),
        in_specs=[pl.BlockSpec((1, n1, tn), blk), pl.BlockSpec((1, n1, tn), blk),
                  pl.BlockSpec(f1.shape, lambda b, j: (0, 0)),
                  pl.BlockSpec((n1, nt * V7X_LANES), lambda b, j: (0, j)),
                  pl.BlockSpec((n1, nt * V7X_LANES), lambda b, j: (0, j))],
        out_specs=[pl.BlockSpec((1, n1, nt, D_FOURIER), oblk),
                   pl.BlockSpec((1, n1, nt, D_FOURIER), oblk)],
        out_shape=[sds, sds],
        scratch_shapes=[_stage_scratch(nt, n1, D_FOURIER)] * 2,
        compiler_params=_params(("parallel", "parallel")),
        name="fft1",
    )(wre3, wim3, f1, cos_t, sin_t)


def _fft2_kernel(tr_ref, ti_ref, f2_ref, o_ref, res_s, *, kb, n2):
    for k in range(kb):
        rhs = jnp.concatenate([tr_ref[0, k], ti_ref[0, k]], axis=0)
        _stage_group(res_s, k, _dot(f2_ref[...], rhs))
    for k2 in range(n2):
        o_ref[0, k2] = _row_of_each_group(res_s, k2, kb, n2).astype(BF16)


def _fft2_call(tr4, ti4, f2):
    bsz, n1, n2, dfo = tr4.shape
    kb = FFT2_K_TILE
    blk = lambda b, j: (b, j, 0, 0)
    return pl.pallas_call(
        functools.partial(_fft2_kernel, kb=kb, n2=n2),
        grid=(bsz, n1 // kb),
        in_specs=[pl.BlockSpec((1, kb, n2, dfo), blk), pl.BlockSpec((1, kb, n2, dfo), blk),
                  pl.BlockSpec(f2.shape, lambda b, j: (0, 0))],
        out_specs=pl.BlockSpec((1, n2, kb, dfo), lambda b, j: (b, 0, j, 0)),
        out_shape=jax.ShapeDtypeStruct((bsz, n2, n1, dfo), BF16),
        scratch_shapes=[_stage_scratch(kb, n2, dfo)],
        compiler_params=_params(("parallel", "parallel")),
        name="fft2",
    )(tr4, ti4, f2)


def _dft_tables(seq):
    n1 = FFT_N1
    n2 = seq // n1
    gd = FOURIER_GROUP_DIM
    total_scale = 1.0 / np.sqrt(float(seq) * gd)
    s_c = 2.0 ** -4
    s_1 = 2.0 ** -3
    s_2 = total_scale / (s_c * s_1)

    def cs(n):
        ang = 2.0 * np.pi * np.outer(np.arange(n), np.arange(n)) / n
        return np.cos(ang), np.sin(ang)

    cc, sc = cs(gd)
    fc = np.concatenate([cc, -sc], axis=1) * s_c
    c1, s1 = cs(n1)
    f1 = np.block([[c1, s1], [-s1, c1]]) * s_1
    c2, s2 = cs(n2)
    f2 = np.concatenate([c2, s2], axis=1) * s_2
    ang = 2.0 * np.pi * np.outer(np.arange(n1), np.arange(n2)) / seq
    cos_t = np.repeat(np.cos(ang), V7X_LANES, axis=1)
    sin_t = np.repeat(np.sin(ang), V7X_LANES, axis=1)
    as_bf16 = lambda m: jnp.asarray(m, F32).astype(BF16)
    return (as_bf16(fc), as_bf16(f1), as_bf16(f2),
            jnp.asarray(cos_t, F32), jnp.asarray(sin_t, F32))


def _mix_kernel(a_ref, cv_ref, ga_ref, gb_ref, x_ref, mod_ref,
                wf_ref, wco_ref, wo_ref, gmoe_ref, wrt_ref, br_ref,
                x1_ref, h2_ref, pt_ref):
    y_b = _dot(cv_ref[...], wco_ref[...])
    y_a = _dot(a_ref[...], wf_ref[...])
    z = ga_ref[...] * y_a.astype(BF16) + gb_ref[...] * y_b.astype(BF16)
    mix = _dot(z, wo_ref[...])
    gate_m = mod_ref[0, 2:3, :]
    x1 = x_ref[...] + gate_m * mix
    x1_ref[...] = x1

    h2 = _rms_modulate(x1, gmoe_ref[...], mod_ref[0, 3:4, :], mod_ref[0, 4:5, :])
    h2_hi = h2.astype(BF16)
    h2_ref[...] = h2_hi
    h2_lo = (h2 - h2_hi.astype(F32)).astype(BF16)
    w_r = wrt_ref[...]
    w_hi = w_r.astype(BF16)
    w_lo = (w_r - w_hi.astype(F32)).astype(BF16)
    n_exp = w_r.shape[0]
    rows_t = lambda a, b: lax.dot_general(a, b, (((1,), (1,)), ((), ())),
                                          preferred_element_type=F32)
    both = rows_t(jnp.concatenate([w_hi, w_lo], axis=0), h2_hi)
    logits = both[:n_exp] + both[n_exp:] + rows_t(w_hi, h2_lo) + br_ref[...]
    m = jnp.max(logits, axis=0, keepdims=True)
    e = jnp.exp(logits - m)
    probs_t = e / jnp.sum(e, axis=0, keepdims=True)
    pt_ref[...] = probs_t.reshape(pt_ref.shape)


def _mix_call(a2, cv, ga, gb, x2, mod3, wf, wco, wo, g_moe, wr_t, b_r, *, seq):
    t, d = x2.shape
    n_exp = wr_t.shape[0]
    tm = ROW_TILE
    tpb = seq // tm
    bsz = t // seq
    row = lambda i: (i, 0)
    const = lambda i: (0, 0)
    return pl.pallas_call(
        _mix_kernel,
        grid=(t // tm,),
        in_specs=[pl.BlockSpec((tm, a2.shape[1]), row),
                  pl.BlockSpec((tm, cv.shape[1]), row),
                  pl.BlockSpec((tm, d), row), pl.BlockSpec((tm, d), row),
                  pl.BlockSpec((tm, d), row),
                  pl.BlockSpec((1, N_MOD, d), lambda i: (i // tpb, 0, 0)),
                  _resident(wf.shape, const), _resident(wco.shape, const),
                  _resident(wo.shape, const),
                  pl.BlockSpec((1, d), const),
                  pl.BlockSpec(wr_t.shape, const), pl.BlockSpec(b_r.shape, const)],
        out_specs=[pl.BlockSpec((tm, d), row), pl.BlockSpec((tm, d), row),
                   pl.BlockSpec((n_exp, tm // V7X_LANES, V7X_LANES),
                                lambda i: (i // tpb, i % tpb, 0))],
        out_shape=[jax.ShapeDtypeStruct((t, d), F32), jax.ShapeDtypeStruct((t, d), BF16),
                   jax.ShapeDtypeStruct((bsz * n_exp, seq // V7X_LANES, V7X_LANES), F32)],
        compiler_params=_params(("parallel",)),
        name="mix",
    )(a2, cv, ga, gb, x2, mod3, wf, wco, wo, g_moe, wr_t, b_r)


def _excl_cumsum(mask_f, upper, lower_strict):
    r, rows, lanes = mask_f.shape
    m2 = mask_f.reshape(r * rows, lanes).astype(BF16)
    incl = _dot(m2, upper)
    tot = jnp.broadcast_to(incl[:, lanes - 1:lanes], incl.shape).astype(BF16)
    tot3 = tot.reshape(r, rows, lanes)
    offs = [_dot(lower_strict, tot3[j]) for j in range(r)]
    off = jnp.stack(offs, axis=0)
    return incl.reshape(r, rows, lanes) - mask_f + off


def _select_kernel(p_ref, upper_ref, lower_ref, pos_ref, raw_ref, *, cap):
    p = p_ref[...]
    r = p.shape[0]

    def count(mask):
        c = jnp.sum(mask.astype(F32), axis=1, keepdims=True)
        return jnp.sum(c, axis=2, keepdims=True)

    prefix = jnp.zeros(p.shape, I32)
    for bit in range(29, -1, -1):
        cand = prefix | (1 << bit)
        keep = count(p >= pltpu.bitcast(cand, F32)) >= cap
        prefix = jnp.where(keep, cand, prefix)
    thr = pltpu.bitcast(prefix, F32)
    gt = p > thr
    eq = (p == thr).astype(F32)
    need = cap - count(gt)
    upper = upper_ref[...]
    lower = lower_ref[...]
    rank_eq = _excl_cumsum(eq, upper, lower)
    sel = jnp.where(gt, 1.0, jnp.where(rank_eq < need, eq, 0.0))
    raw = _excl_cumsum(sel, upper, lower)
    raw_i = raw.astype(I32)
    raw_ref[...] = raw_i
    pos_ref[...] = jnp.where(sel > 0.0, raw_i, -1)


def _select_call(p3, cap):
    r, rows, lanes = p3.shape
    upper = jnp.asarray(np.triu(np.ones((lanes, lanes))), BF16)
    lower = jnp.asarray(np.tril(np.ones((rows, rows)), -1), BF16)
    full = lambda shape: pl.BlockSpec(shape, lambda i: (0,) * len(shape))
    sds = jax.ShapeDtypeStruct(p3.shape, I32)
    return pl.pallas_call(
        functools.partial(_select_kernel, cap=cap),
        grid=(1,),
        in_specs=[full(p3.shape), full(upper.shape), full(lower.shape)],
        out_specs=[full(p3.shape), full(p3.shape)],
        out_shape=[sds, sds],
        compiler_params=_params(("arbitrary",)),
        name="select",
    )(p3, upper, lower)


def _gather_kernel(pfx_ref, h2_ref, pos_ref, prob_ref, xe_ref, val_ref, *, n_exp, nb, cap):
    b = pl.program_id(0)
    g = pl.program_id(1)
    tb = TOKEN_BLOCK
    ch = GATHER_CHUNK
    halves = tb // V7X_LANES
    sub = V7X_SUBLANES_BF16

    @pl.when(g == 0)
    def _():
        xe_ref[...] = jnp.zeros_like(xe_ref)
        val_ref[...] = jnp.zeros_like(val_ref)

    chunk_rows = lax.broadcasted_iota(I32, (ch, V7X_LANES), 0)
    expert_lane = lax.broadcasted_iota(I32, (1, n_exp), 1)

    def window(e, k):
        base = (b * n_exp + e) * (nb + 1) + k
        p0 = pfx_ref[base]
        start = pl.multiple_of(p0 & -sub, sub)
        return start, pfx_ref[base + 1] - start

    def one_hot(pos_rows, prob_rows, ws):
        rows = chunk_rows + ws
        hits = [jnp.where(p == rows, 1.0, 0.0) for p in pos_rows]
        onehot = jnp.concatenate([h.astype(BF16) for h in hits], axis=1)
        v = hits[0] * prob_rows[0]
        for h, p in zip(hits[1:], prob_rows[1:]):
            v = v + h * p
        return onehot, jnp.sum(v, axis=1, keepdims=True)

    for j in range(GATHER_BLOCKS):
        k = g * GATHER_BLOCKS + j
        hk = h2_ref[0, j * tb:(j + 1) * tb, :]
        tok_rows = [slice(halves * j + h, halves * j + h + 1) for h in range(halves)]
        starts, pieces = [], []
        overflow = None
        for e in range(n_exp):
            ws, need = window(e, k)
            onehot, v = one_hot([pos_ref[e, r, :] for r in tok_rows],
                                [prob_ref[e, r, :] for r in tok_rows], ws)
            val_ref[0, pl.ds(ws, ch), e:e + 1] += v
            starts.append(ws)
            pieces.append(onehot)
            over = need > ch
            overflow = over if overflow is None else jnp.logical_or(overflow, over)
        gathered = _dot(jnp.concatenate(pieces, axis=0), hk).astype(BF16)
        for e in range(n_exp):
            xe_ref[0, e, pl.ds(starts[e], ch), :] += gathered[e * ch:(e + 1) * ch, :]

        @pl.when(overflow)
        def _():
            def per_expert(e, carry):
                ws, need = window(e, k)

                def chunk(c, carry2):
                    wsc = pl.multiple_of(ws + c * ch, sub)
                    onehot, v = one_hot([pos_ref[pl.ds(e, 1), r, :][0] for r in tok_rows],
                                        [prob_ref[pl.ds(e, 1), r, :][0] for r in tok_rows], wsc)
                    xe_ref[0, e, pl.ds(wsc, ch), :] += _dot(onehot, hk).astype(BF16)
                    val_ref[0, pl.ds(wsc, ch), :] += jnp.where(expert_lane == e, v, 0.0)
                    return carry2

                lax.fori_loop(1, (need + ch - 1) // ch, chunk, 0)
                return carry

            lax.fori_loop(0, n_exp, per_expert, 0)


def _gather_call(pfx, h2_3, pos3, prob3, *, n_exp, cap):
    bsz, seq, d = h2_3.shape
    nb = seq // TOKEN_BLOCK
    gb = GATHER_BLOCKS
    rows_per_step = gb * TOKEN_BLOCK // V7X_LANES
    slots = cap + GATHER_CHUNK
    grid_spec = pltpu.PrefetchScalarGridSpec(
        num_scalar_prefetch=1,
        grid=(bsz, nb // gb),
        in_specs=[pl.BlockSpec((1, gb * TOKEN_BLOCK, d), lambda b, g, p: (b, g, 0)),
                  pl.BlockSpec((n_exp, rows_per_step, V7X_LANES), lambda b, g, p: (b, g, 0)),
                  pl.BlockSpec((n_exp, rows_per_step, V7X_LANES), lambda b, g, p: (b, g, 0))],
        out_specs=[_resident((1, n_exp, slots, d), lambda b, g, p: (b, 0, 0, 0)),
                   pl.BlockSpec((1, slots, n_exp), lambda b, g, p: (b, 0, 0))],
    )
    return pl.pallas_call(
        functools.partial(_gather_kernel, n_exp=n_exp, nb=nb, cap=cap),
        grid_spec=grid_spec,
        out_shape=[jax.ShapeDtypeStruct((bsz, n_exp, slots, d), BF16),
                   jax.ShapeDtypeStruct((bsz, slots, n_exp), F32)],
        compiler_params=_params(("arbitrary", "arbitrary")),
        name="gather",
    )(pfx, h2_3, pos3, prob3)


def _ffn_kernel(xe_ref, val_ref, wg_ref, wu_ref, wd_ref, ye_ref, acc_ref):
    e = pl.program_id(0)
    f = pl.program_id(1)
    bsz, _, cap, d = xe_ref.shape

    @pl.when(f == 0)
    def _():
        acc_ref[...] = jnp.zeros_like(acc_ref)

    xe = xe_ref[...].reshape(bsz * cap, d)
    a = _dot(xe, wg_ref[0].astype(BF16))
    u = _dot(xe, wu_ref[0].astype(BF16))
    half_a = 0.5 * a
    hm = (half_a * (jnp.tanh(half_a) + 1.0) * u).astype(BF16)
    acc_ref[...] += _dot(hm, wd_ref[0].astype(BF16))

    @pl.when(f == pl.num_programs(1) - 1)
    def _():
        vals = val_ref[...].reshape(bsz * cap, val_ref.shape[2])
        mine = lax.broadcasted_iota(I32, (1, vals.shape[1]), 1) == e
        val = jnp.sum(jnp.where(mine, vals, 0.0), axis=1, keepdims=True)
        ye_ref[...] = (acc_ref[...] * val).astype(BF16).reshape(bsz, 1, cap, d)


def _ffn_call(xe, val, w_gate, w_up, w_down, *, cap):
    bsz, n_exp, _, d = xe.shape
    d_exp = w_gate.shape[2]
    fc = FFN_CHUNK
    return pl.pallas_call(
        _ffn_kernel,
        grid=(n_exp, d_exp // fc),
        in_specs=[pl.BlockSpec((bsz, 1, cap, d), lambda e, f: (0, e, 0, 0)),
                  pl.BlockSpec((bsz, cap, n_exp), lambda e, f: (0, 0, 0)),
                  pl.BlockSpec((1, d, fc), lambda e, f: (e, 0, f)),
                  pl.BlockSpec((1, d, fc), lambda e, f: (e, 0, f)),
                  pl.BlockSpec((1, fc, d), lambda e, f: (e, f, 0))],
        out_specs=pl.BlockSpec((bsz, 1, cap, d), lambda e, f: (0, e, 0, 0)),
        out_shape=jax.ShapeDtypeStruct((bsz, n_exp, cap, d), BF16),
        scratch_shapes=[pltpu.VMEM((bsz * cap, d), F32)],
        compiler_params=_params(("arbitrary", "arbitrary")),
        name="ffn",
    )(xe, val, w_gate, w_up, w_down)


def _combine_kernel(pfx_ref, x1_ref, ye_ref, posq_ref, spread_ref, mod_ref, gfin_ref,
                    o_ref, ycat_ref, scat_ref, acc_ref, *, n_exp, nb, cap):
    b = pl.program_id(0)
    tb = TOKEN_BLOCK
    win = COMBINE_WINDOW
    align = V7X_SUBLANES_BF16
    gate_f = mod_ref[0, 5:6, :]
    lane = lax.broadcasted_iota(I32, (1, V7X_LANES), 1)
    per = V7X_LANES // win

    def finish(rows, moe):
        x2 = x1_ref[rows, :] + gate_f * moe
        ms = jnp.mean(x2 * x2, axis=-1, keepdims=True)
        o_ref[rows, :] = x2 * lax.rsqrt(ms + RMS_EPS) * gfin_ref[...]

    redo = []
    for s in range(COMBINE_BLOCKS):
        k = pl.program_id(1) * COMBINE_BLOCKS + s
        rows = slice(s * tb, (s + 1) * tb)
        p0s, ends, wss = [], [], []
        fits = None
        for e in range(n_exp):
            base = (b * n_exp + e) * (nb + 1) + k
            p0 = pfx_ref[base]
            end = pfx_ref[base + 1]
            ws = pl.multiple_of(jnp.minimum(p0 & -align, cap - win), align)
            ok = end <= ws + win
            fits = ok if fits is None else jnp.logical_and(fits, ok)
            p0s.append(p0)
            ends.append(end)
            wss.append(ws)
        slot_of_col = _dot(posq_ref[rows, :], spread_ref[...])
        targets = []
        for e0 in range(0, n_exp, per):
            target = lane + (wss[e0] + 1)
            for j in range(1, per):
                target = jnp.where(lane >= j * win, lane + (wss[e0 + j] + 1 - j * win), target)
            targets.append(target)
            for j in range(per):
                e = e0 + j
                ycat_ref[s, e * win:(e + 1) * win, :] = ye_ref[0, e, pl.ds(wss[e], win), :]
        target_row = jnp.concatenate(targets, axis=1).astype(F32)
        scat_ref[s] = jnp.where(slot_of_col == target_row, 1.0, 0.0).astype(BF16)
        finish(rows, _dot(scat_ref[s], ycat_ref[s]))
        redo.append((rows, jnp.logical_not(fits), p0s, ends))

    for rows, misfit, p0s, ends in redo:
        @pl.when(misfit)
        def _():
            acc_ref[...] = jnp.zeros_like(acc_ref)
            cols = lax.broadcasted_iota(I32, (tb, tb), 1) + 1
            for e in range(n_exp):
                ws1 = pl.multiple_of(jnp.minimum(p0s[e] & -align, cap - tb), align)
                ws2 = pl.multiple_of(jnp.minimum(ws1 + tb, cap - tb), align)
                high = posq_ref[rows, e:e + 1].astype(F32)
                low = posq_ref[rows, n_exp + e:n_exp + e + 1].astype(F32)
                pos_col = (high * (1 << SLOT_DIGIT_BITS) + low).astype(I32)

                @pl.when(ends[e] > p0s[e])
                def _():
                    onehot = jnp.where(pos_col == cols + ws1, 1.0, 0.0).astype(BF16)
                    acc_ref[...] += _dot(onehot, ye_ref[0, e, pl.ds(ws1, tb), :])

                @pl.when(ends[e] > ws1 + tb)
                def _():
                    c2 = cols + ws2
                    hit = jnp.logical_and(pos_col == c2, c2 > ws1 + tb)
                    onehot = jnp.where(hit, 1.0, 0.0).astype(BF16)
                    acc_ref[...] += _dot(onehot, ye_ref[0, e, pl.ds(ws2, tb), :])
            finish(rows, acc_ref[...])


def _combine_call(pfx, x1, ye, posq, mod3, g_final, *, seq):
    t, d = x1.shape
    bsz, n_exp, cap, _ = ye.shape
    tb = TOKEN_BLOCK
    nb = seq // tb
    step_rows = COMBINE_BLOCKS * tb
    steps = nb // COMBINE_BLOCKS
    row = lambda b, k, p: (b * steps + k, 0)
    digit = 1 << SLOT_DIGIT_BITS
    assert cap <= digit * digit
    spread = np.kron(np.eye(n_exp), np.ones((1, COMBINE_WINDOW)))
    spread = jnp.asarray(np.concatenate([digit * spread, spread], axis=0), F32).astype(BF16)
    grid_spec = pltpu.PrefetchScalarGridSpec(
        num_scalar_prefetch=1,
        grid=(bsz, steps),
        in_specs=[pl.BlockSpec((step_rows, d), row),
                  _resident((1, n_exp, cap, d), lambda b, k, p: (b, 0, 0, 0)),
                  pl.BlockSpec((step_rows, 2 * n_exp), row),
                  pl.BlockSpec(spread.shape, lambda b, k, p: (0, 0)),
                  pl.BlockSpec((1, N_MOD, d), lambda b, k, p: (b, 0, 0)),
                  pl.BlockSpec((1, d), lambda b, k, p: (0, 0))],
        out_specs=pl.BlockSpec((step_rows, d), row),
        scratch_shapes=[pltpu.VMEM((COMBINE_BLOCKS, n_exp * COMBINE_WINDOW, d), BF16),
                        pltpu.VMEM((COMBINE_BLOCKS, tb, n_exp * COMBINE_WINDOW), BF16),
                        pltpu.VMEM((tb, d), F32)],
    )
    return pl.pallas_call(
        functools.partial(_combine_kernel, n_exp=n_exp, nb=nb, cap=cap),
        grid_spec=grid_spec,
        out_shape=jax.ShapeDtypeStruct((t, d), F32),
        compiler_params=_params(("arbitrary", "arbitrary")),
        name="combine",
    )(pfx, x1, ye, posq, spread, mod3, g_final)


def _layer(x2, c_t, w_ada, b_ada, g_norm_mix, w_in, b_gate, w_fourier, w_conv, w_conv_out,
           w_o, g_norm_moe, w_router, b_router, w_gate_e, w_up_e, w_down_e, *, bsz, seq):
    t, d = x2.shape
    n_exp = w_router.shape[1]
    cap = EC_CAPACITY * seq // n_exp
    n1 = FFT_N1
    n2 = seq // n1
    fc, f1, f2, cos_t, sin_t = _dft_tables(seq)

    mod = _mod_call(c_t, w_ada, b_ada.reshape(1, -1))
    mod3 = mod.reshape(bsz, N_MOD, d)

    wre, wim, cv, ga, gb = _proj_call(
        x2, mod3, g_norm_mix.reshape(1, d), w_in.astype(BF16), b_gate.reshape(1, -1), fc, w_conv,
        seq=seq)

    tr, ti = _fft1_call(wre, wim, f1, cos_t, sin_t)
    a2 = _fft2_call(tr, ti, f2).reshape(t, D_FOURIER)

    x1, h2, probs_t = _mix_call(
        a2, cv, ga, gb, x2, mod3, w_fourier.astype(BF16), w_conv_out.astype(BF16),
        w_o.astype(BF16), g_norm_moe.reshape(1, d), w_router.T, b_router.reshape(n_exp, 1),
        seq=seq)

    prob3 = probs_t
    pos, raw = _select_call(prob3, cap)
    stride = TOKEN_BLOCK // V7X_LANES
    starts = raw[:, ::stride, 0]
    pfx = jnp.concatenate([starts, jnp.full((bsz * n_exp, 1), cap, I32)], axis=1).reshape(-1)
    slot1 = pos.reshape(bsz, n_exp, seq) + 1
    digits = jnp.concatenate([slot1 >> SLOT_DIGIT_BITS, slot1 & ((1 << SLOT_DIGIT_BITS) - 1)],
                             axis=1)
    posq = jnp.transpose(digits, (0, 2, 1)).reshape(t, 2 * n_exp).astype(BF16)

    xe, val = _gather_call(pfx, h2.reshape(bsz, seq, d), pos, prob3, n_exp=n_exp, cap=cap)
    ye = _ffn_call(xe, val, w_gate_e, w_up_e, w_down_e, cap=cap)
    return ye, pfx, x1, posq, mod3


def kernel(x, c, w_ada, b_ada, g_norm_mix, w_in, b_gate, w_fourier, w_conv, w_conv_out, w_o,
           g_norm_moe, w_router, b_router, w_gate_e, w_up_e, w_down_e, g_final):
    bsz, seq, d = x.shape
    assert w_ada.shape[0] == 1
    assert seq % (FFT_N1 * V7X_SUBLANES_F32) == 0 and seq % ROW_TILE == 0
    assert (EC_CAPACITY * seq // w_router.shape[2]) >= TOKEN_BLOCK
    x2 = x.reshape(bsz * seq, d)
    ye, pfx, x1, posq, mod3 = _layer(
        x2, c.T, w_ada[0], b_ada[0], g_norm_mix[0], w_in[0], b_gate[0], w_fourier[0],
        w_conv[0], w_conv_out[0], w_o[0], g_norm_moe[0], w_router[0], b_router[0],
        w_gate_e[0], w_up_e[0], w_down_e[0], bsz=bsz, seq=seq)
    out = _combine_call(pfx, x1, ye, posq, mod3, g_final.reshape(1, d), seq=seq)
    return out.reshape(bsz, seq, d)
```

```python
import functools

import numpy as np
import jax
import jax.numpy as jnp
from jax import lax
from jax.experimental import pallas as pl
from jax.experimental.pallas import tpu as pltpu

F32 = jnp.float32
BF16 = jnp.bfloat16
I32 = jnp.int32

FOURIER_GROUPS = 4
FOURIER_GROUP_DIM = 128
D_FOURIER = FOURIER_GROUPS * FOURIER_GROUP_DIM
N_MOD = 6
EC_CAPACITY = 2
RMS_EPS = 1e-6

V7X_LANES = 128
V7X_SUBLANES_F32 = 8
V7X_SUBLANES_BF16 = 16
V7X_VMEM_BYTES = 64 * 1024 * 1024
VMEM_LIMIT_BYTES = V7X_VMEM_BYTES - 6 * 1024 * 1024

MOD_COL_TILE = 1536
MOD_ACC_COLS = 512
ROW_TILE = 1024
FFT_N1 = 128
FFT_N2_TILE = 16
FFT2_K_TILE = 32
TOKEN_BLOCK = 256
GATHER_CHUNK = 64
GATHER_BLOCKS = 8
COMBINE_WINDOW = 64
COMBINE_BLOCKS = 2
SLOT_DIGIT_BITS = 5
FFN_CHUNK = 512


def _dot(a, b):
    return jnp.dot(a, b, preferred_element_type=F32)


def _sigmoid(x):
    return 0.5 * jnp.tanh(0.5 * x) + 0.5


def _params(semantics):
    return pltpu.CompilerParams(dimension_semantics=semantics,
                                vmem_limit_bytes=VMEM_LIMIT_BYTES)


def _pitch(rows):
    p = -(-rows // V7X_SUBLANES_F32)
    return (p if p % 2 else p + 1) * V7X_SUBLANES_F32


def _stage_group(stage_ref, group, value):
    rows = value.shape[0]
    row0 = group * _pitch(rows)
    for l in range(stage_ref.shape[0]):
        stage_ref[l, row0:row0 + rows, :] = value[:, l * V7X_LANES:(l + 1) * V7X_LANES]


def _row_of_each_group(stage_ref, row, groups, rows):
    return jnp.concatenate([stage_ref[l, pl.ds(row, groups, stride=_pitch(rows)), :]
                            for l in range(stage_ref.shape[0])], axis=1)


def _stage_scratch(groups, rows, width):
    return pltpu.VMEM((width // V7X_LANES, groups * _pitch(rows), V7X_LANES), F32)


def _resident(block_shape, index_map):
    return pl.BlockSpec(block_shape, index_map, pipeline_mode=pl.Buffered(1))


def _mod_kernel(ct_ref, w_ref, b_ref, o_ref, *, bsz):
    ct = ct_ref[...]
    ct = ct * jax.nn.sigmoid(ct)
    sub = V7X_SUBLANES_F32
    d, tn = w_ref.shape
    cw = MOD_ACC_COLS
    for c0 in range(0, tn, cw):
        accs = [jnp.zeros((sub, cw), F32) for _ in range(bsz)]
        for g in range(d // sub):
            w = w_ref[g * sub:(g + 1) * sub, c0:c0 + cw]
            for r in range(bsz):
                accs[r] = accs[r] + ct[g * sub:(g + 1) * sub, r:r + 1] * w
        for r in range(bsz):
            o_ref[r:r + 1, c0:c0 + cw] = (jnp.sum(accs[r], axis=0, keepdims=True)
                                          + b_ref[:, c0:c0 + cw])


def _mod_call(c_t, w_ada, b_ada):
    d, bsz = c_t.shape
    n = w_ada.shape[1]
    tn = MOD_COL_TILE
    return pl.pallas_call(
        functools.partial(_mod_kernel, bsz=bsz),
        grid=(n // tn,),
        in_specs=[pl.BlockSpec((d, bsz), lambda j: (0, 0)),
                  pl.BlockSpec((d, tn), lambda j: (0, j)),
                  pl.BlockSpec((1, tn), lambda j: (0, j))],
        out_specs=pl.BlockSpec((bsz, tn), lambda j: (0, j)),
        out_shape=jax.ShapeDtypeStruct((bsz, n), F32),
        compiler_params=_params(("arbitrary",)),
        name="mod",
    )(c_t, w_ada, b_ada)


def _rms_modulate(x, g, shift, scale):
    ms = jnp.mean(x * x, axis=-1, keepdims=True)
    return x * lax.rsqrt(ms + RMS_EPS) * (g * (1.0 + scale)) + shift


def _proj_kernel(x_ref, xprev_ref, xnext_ref, mod_ref, g_ref, win_ref, bgate_ref, fc_ref, wconv_ref,
                 wre_ref, wim_ref, cv_ref, ga_ref, gb_ref, re_s, im_s,
                 *, d, d_conv, n2, tm, tpb):
    i = pl.program_id(0)
    shift = mod_ref[0, 0:1, :]
    scale = mod_ref[0, 1:2, :]
    h = _rms_modulate(x_ref[...], g_ref[...], shift, scale).astype(BF16)
    x_halo = jnp.concatenate([xprev_ref[...], xnext_ref[...]], axis=0)
    h_halo = _rms_modulate(x_halo, g_ref[...], shift, scale).astype(BF16)
    h_ext = jnp.concatenate([h, h_halo], axis=0)

    o1 = D_FOURIER
    o2 = o1 + d_conv
    o3 = o2 + d_conv
    o4 = o3 + d_conv
    uf = _dot(h, win_ref[:, 0:o1]).astype(BF16)
    gd = FOURIER_GROUP_DIM
    ws = [_dot(uf[:, g * gd:(g + 1) * gd], fc_ref[...]) for g in range(FOURIER_GROUPS)]
    w_re = jnp.concatenate([w[:, :gd] for w in ws], axis=1)
    w_im = jnp.concatenate([w[:, gd:] for w in ws], axis=1)
    n1_rows = w_re.shape[0] // n2
    for r in range(n1_rows):
        _stage_group(re_s, r, w_re[r * n2:(r + 1) * n2, :])
        _stage_group(im_s, r, w_im[r * n2:(r + 1) * n2, :])
    for j in range(n2):
        cols = slice(j * D_FOURIER, (j + 1) * D_FOURIER)
        wre_ref[0, :, cols] = _row_of_each_group(re_s, j, n1_rows, n2).astype(BF16)
        wim_ref[0, :, cols] = _row_of_each_group(im_s, j, n1_rows, n2).astype(BF16)
    q_ext = _dot(h_ext, win_ref[:, o3:o4]) * _dot(h_ext, win_ref[:, o1:o2])
    q = q_ext[0:tm, :]
    nh = xprev_ref.shape[0]
    first = (i % tpb) == 0
    last = (i % tpb) == tpb - 1
    hp = jnp.where(first, 0.0, q_ext[tm + nh - 1:tm + nh, :])
    hn = jnp.where(last, 0.0, q_ext[tm + nh:tm + nh + 1, :])
    rows = lax.broadcasted_iota(I32, (tm, 1), 0)
    q_prev = jnp.where(rows == 0, hp, pltpu.roll(q, 1, axis=0))
    q_next = jnp.where(rows == tm - 1, hn, pltpu.roll(q, tm - 1, axis=0))
    conv = q_prev * wconv_ref[0:1, :] + q * wconv_ref[1:2, :] + q_next * wconv_ref[2:3, :]
    cv_ref[...] = (_dot(h, win_ref[:, o2:o3]) * conv).astype(BF16)
    ga_ref[...] = _sigmoid(_dot(h, win_ref[:, o4:o4 + d]) + bgate_ref[:, 0:d]).astype(BF16)
    gb_ref[...] = _sigmoid(
        _dot(h, win_ref[:, o4 + d:o4 + 2 * d]) + bgate_ref[:, d:2 * d]).astype(BF16)


def _proj_call(x2, mod3, g_mix, w_in, b_gate, fc, w_conv, *, seq):
    t, d = x2.shape
    k_in = w_in.shape[1]
    d_conv = (k_in - D_FOURIER - 2 * d) // 3
    tm = ROW_TILE
    tpb = seq // tm
    n2 = seq // FFT_N1
    n1_rows = tm // n2
    bsz = t // seq
    sub = V7X_SUBLANES_F32
    row = lambda i: (i, 0)
    const = lambda i: (0, 0)
    out_sds = lambda n: jax.ShapeDtypeStruct((t, n), BF16)
    assert n1_rows % V7X_SUBLANES_BF16 == 0
    dft_in = jax.ShapeDtypeStruct((bsz, FFT_N1, n2 * D_FOURIER), BF16)
    dft_blk = pl.BlockSpec((1, n1_rows, n2 * D_FOURIER), lambda i: (i // tpb, i % tpb, 0))
    return pl.pallas_call(
        functools.partial(_proj_kernel, d=d, d_conv=d_conv, n2=n2, tm=tm, tpb=tpb),
        grid=(t // tm,),
        in_specs=[pl.BlockSpec((tm, d), row),
                  pl.BlockSpec((sub, d), lambda i: (jnp.maximum(i * (tm // sub) - 1, 0), 0)),
                  pl.BlockSpec((sub, d),
                               lambda i: (jnp.minimum((i + 1) * (tm // sub), t // sub - 1), 0)),
                  pl.BlockSpec((1, N_MOD, d), lambda i: (i // tpb, 0, 0)),
                  pl.BlockSpec((1, d), const),
                  _resident((d, k_in), const),
                  pl.BlockSpec((1, 2 * d), const),
                  pl.BlockSpec(fc.shape, const),
                  pl.BlockSpec(w_conv.shape, const)],
        out_specs=[dft_blk, dft_blk, pl.BlockSpec((tm, d_conv), row),
                   pl.BlockSpec((tm, d), row), pl.BlockSpec((tm, d), row)],
        out_shape=[dft_in, dft_in, out_sds(d_conv), out_sds(d), out_sds(d)],
        scratch_shapes=[_stage_scratch(n1_rows, n2, D_FOURIER)] * 2,
        compiler_params=_params(("parallel",)),
        name="proj",
    )(x2, x2, x2, mod3, g_mix, w_in, b_gate, fc, w_conv)


def _fft1_kernel(wre_ref, wim_ref, f1_ref, tr_ref, ti_ref, re_s, im_s, *, n1, nt):
    half = nt // 2
    for h in range(2):
        hcols = slice(h * half * D_FOURIER, (h + 1) * half * D_FOURIER)
        w = jnp.concatenate([wre_ref[0, :, hcols], wim_ref[0, :, hcols]], axis=0)
        t = _dot(f1_ref[...], w)
        for jj in range(half):
            j = h * half + jj
            cols = slice(jj * D_FOURIER, (jj + 1) * D_FOURIER)
            _stage_group(re_s, j, t[:n1, cols])
            _stage_group(im_s, j, t[n1:, cols])
    for k in range(n1):
        tr_ref[0, k] = _row_of_each_group(re_s, k, nt, n1).astype(BF16)
        ti_ref[0, k] = _row_of_each_group(im_s, k, nt, n1).astype(BF16)


def _fft1_call(wre3, wim3, f1):
    bsz, n1, cols = wre3.shape
    n2 = cols // D_FOURIER
    nt = FFT_N2_TILE
    tn = nt * D_FOURIER
    blk = lambda b, j: (b, 0, j)
    oblk = lambda b, j: (b, 0, j, 0)
    sds = jax.ShapeDtypeStruct((bsz, n1, n2, D_FOURIER), BF16)
    return pl.pallas_call(
        functools.partial(_fft1_kernel, n1=n1, nt=nt),
        grid=(bsz, n2 // nt),
        in_specs=[pl.BlockSpec((1, n1, tn), blk), pl.BlockSpec((1, n1, tn), blk),
                  pl.BlockSpec(f1.shape, lambda b, j: (0, 0))],
        out_specs=[pl.BlockSpec((1, n1, nt, D_FOURIER), oblk),
                   pl.BlockSpec((1, n1, nt, D_FOURIER), oblk)],
        out_shape=[sds, sds],
        scratch_shapes=[_stage_scratch(nt, n1, D_FOURIER)] * 2,
        compiler_params=_params(("parallel", "parallel")),
        name="fft1",
    )(wre3, wim3, f1)


def _fft2_kernel(tr_ref, ti_ref, f2_ref, o_ref, res_s, *, kb, n2):
    for k in range(kb):
        rhs = jnp.concatenate([tr_ref[0, k], ti_ref[0, k]], axis=0)
        _stage_group(res_s, k, _dot(f2_ref[k], rhs))
    for k2 in range(n2):
        o_ref[0, k2] = _row_of_each_group(res_s, k2, kb, n2).astype(BF16)


def _fft2_call(tr4, ti4, f2):
    bsz, n1, n2, dfo = tr4.shape
    kb = FFT2_K_TILE
    blk = lambda b, j: (b, j, 0, 0)
    return pl.pallas_call(
        functools.partial(_fft2_kernel, kb=kb, n2=n2),
        grid=(bsz, n1 // kb),
        in_specs=[pl.BlockSpec((1, kb, n2, dfo), blk), pl.BlockSpec((1, kb, n2, dfo), blk),
                  pl.BlockSpec((kb,) + f2.shape[1:], lambda b, j: (j, 0, 0))],
        out_specs=pl.BlockSpec((1, n2, kb, dfo), lambda b, j: (b, 0, j, 0)),
        out_shape=jax.ShapeDtypeStruct((bsz, n2, n1, dfo), BF16),
        scratch_shapes=[_stage_scratch(kb, n2, dfo)],
        compiler_params=_params(("parallel", "parallel")),
        name="fft2",
    )(tr4, ti4, f2)


def _dft_tables(seq):
    n1 = FFT_N1
    n2 = seq // n1
    gd = FOURIER_GROUP_DIM
    total_scale = 1.0 / np.sqrt(float(seq) * gd)
    s_c = 2.0 ** -4
    s_1 = 2.0 ** -3
    s_2 = total_scale / (s_c * s_1)

    def cs(n):
        ang = 2.0 * np.pi * np.outer(np.arange(n), np.arange(n)) / n
        return np.cos(ang), np.sin(ang)

    cc, sc = cs(gd)
    fc = np.concatenate([cc, -sc], axis=1) * s_c
    c1, s1 = cs(n1)
    f1 = np.block([[c1, s1], [-s1, c1]]) * s_1
    k_out = np.arange(n1)[:, None, None] + n1 * np.arange(n2)[None, :, None]
    phase = 2.0 * np.pi * k_out * np.arange(n2)[None, None, :] / seq
    f2 = np.concatenate([np.cos(phase), np.sin(phase)], axis=2) * s_2
    as_bf16 = lambda m: jnp.asarray(m, F32).astype(BF16)
    return as_bf16(fc), as_bf16(f1), as_bf16(f2)


def _mix_kernel(a_ref, cv_ref, ga_ref, gb_ref, x_ref, mod_ref,
                wf_ref, wco_ref, wo_ref, gmoe_ref, wrt_ref, br_ref,
                x1_ref, h2_ref, pt_ref):
    y_b = _dot(cv_ref[...], wco_ref[...])
    y_a = _dot(a_ref[...], wf_ref[...])
    z = ga_ref[...] * y_a.astype(BF16) + gb_ref[...] * y_b.astype(BF16)
    mix = _dot(z, wo_ref[...])
    gate_m = mod_ref[0, 2:3, :]
    x1 = x_ref[...] + gate_m * mix
    x1_ref[...] = x1

    h2 = _rms_modulate(x1, gmoe_ref[...], mod_ref[0, 3:4, :], mod_ref[0, 4:5, :])
    h2_hi = h2.astype(BF16)
    h2_ref[...] = h2_hi
    h2_lo = (h2 - h2_hi.astype(F32)).astype(BF16)
    w_r = wrt_ref[...]
    w_hi = w_r.astype(BF16)
    w_lo = (w_r - w_hi.astype(F32)).astype(BF16)
    n_exp = w_r.shape[0]
    rows_t = lambda a, b: lax.dot_general(a, b, (((1,), (1,)), ((), ())),
                                          preferred_element_type=F32)
    both = rows_t(jnp.concatenate([w_hi, w_lo], axis=0), h2_hi)
    logits = both[:n_exp] + both[n_exp:] + rows_t(w_hi, h2_lo) + br_ref[...]
    m = jnp.max(logits, axis=0, keepdims=True)
    e = jnp.exp(logits - m)
    probs_t = e / jnp.sum(e, axis=0, keepdims=True)
    pt_ref[...] = probs_t.reshape(pt_ref.shape)


def _mix_call(a2, cv, ga, gb, x2, mod3, wf, wco, wo, g_moe, wr_t, b_r, *, seq):
    t, d = x2.shape
    n_exp = wr_t.shape[0]
    tm = ROW_TILE
    tpb = seq // tm
    bsz = t // seq
    row = lambda i: (i, 0)
    const = lambda i: (0, 0)
    return pl.pallas_call(
        _mix_kernel,
        grid=(t // tm,),
        in_specs=[pl.BlockSpec((tm, a2.shape[1]), row),
                  pl.BlockSpec((tm, cv.shape[1]), row),
                  pl.BlockSpec((tm, d), row), pl.BlockSpec((tm, d), row),
                  pl.BlockSpec((tm, d), row),
                  pl.BlockSpec((1, N_MOD, d), lambda i: (i // tpb, 0, 0)),
                  _resident(wf.shape, const), _resident(wco.shape, const),
                  _resident(wo.shape, const),
                  pl.BlockSpec((1, d), const),
                  pl.BlockSpec(wr_t.shape, const), pl.BlockSpec(b_r.shape, const)],
        out_specs=[pl.BlockSpec((tm, d), row), pl.BlockSpec((tm, d), row),
                   pl.BlockSpec((n_exp, tm // V7X_LANES, V7X_LANES),
                                lambda i: (i // tpb, i % tpb, 0))],
        out_shape=[jax.ShapeDtypeStruct((t, d), F32), jax.ShapeDtypeStruct((t, d), BF16),
                   jax.ShapeDtypeStruct((bsz * n_exp, seq // V7X_LANES, V7X_LANES), F32)],
        compiler_params=_params(("parallel",)),
        name="mix",
    )(a2, cv, ga, gb, x2, mod3, wf, wco, wo, g_moe, wr_t, b_r)


def _excl_cumsum(mask_f, upper, lower_strict):
    r, rows, lanes = mask_f.shape
    m2 = mask_f.reshape(r * rows, lanes).astype(BF16)
    incl = _dot(m2, upper)
    tot = jnp.broadcast_to(incl[:, lanes - 1:lanes], incl.shape).astype(BF16)
    tot3 = tot.reshape(r, rows, lanes)
    offs = [_dot(lower_strict, tot3[j]) for j in range(r)]
    off = jnp.stack(offs, axis=0)
    return incl.reshape(r, rows, lanes) - mask_f + off


def _select_kernel(p_ref, upper_ref, lower_ref, pos_ref, raw_ref, *, cap):
    p = p_ref[...]
    r = p.shape[0]

    def count(mask):
        c = jnp.sum(mask.astype(F32), axis=1, keepdims=True)
        return jnp.sum(c, axis=2, keepdims=True)

    prefix = jnp.zeros(p.shape, I32)
    for bit in range(29, -1, -1):
        cand = prefix | (1 << bit)
        keep = count(p >= pltpu.bitcast(cand, F32)) >= cap
        prefix = jnp.where(keep, cand, prefix)
    thr = pltpu.bitcast(prefix, F32)
    gt = p > thr
    eq = (p == thr).astype(F32)
    need = cap - count(gt)
    upper = upper_ref[...]
    lower = lower_ref[...]
    rank_eq = _excl_cumsum(eq, upper, lower)
    sel = jnp.where(gt, 1.0, jnp.where(rank_eq < need, eq, 0.0))
    raw = _excl_cumsum(sel, upper, lower)
    raw_i = raw.astype(I32)
    raw_ref[...] = raw_i
    pos_ref[...] = jnp.where(sel > 0.0, raw_i, -1)


def _select_call(p3, cap):
    r, rows, lanes = p3.shape
    upper = jnp.asarray(np.triu(np.ones((lanes, lanes))), BF16)
    lower = jnp.asarray(np.tril(np.ones((rows, rows)), -1), BF16)
    full = lambda shape: pl.BlockSpec(shape, lambda i: (0,) * len(shape))
    sds = jax.ShapeDtypeStruct(p3.shape, I32)
    return pl.pallas_call(
        functools.partial(_select_kernel, cap=cap),
        grid=(1,),
        in_specs=[full(p3.shape), full(upper.shape), full(lower.shape)],
        out_specs=[full(p3.shape), full(p3.shape)],
        out_shape=[sds, sds],
        compiler_params=_params(("arbitrary",)),
        name="select",
    )(p3, upper, lower)


def _gather_kernel(pfx_ref, h2_ref, pos_ref, prob_ref, xe_ref, val_ref, *, n_exp, nb, cap):
    b = pl.program_id(0)
    g = pl.program_id(1)
    tb = TOKEN_BLOCK
    ch = GATHER_CHUNK
    halves = tb // V7X_LANES
    sub = V7X_SUBLANES_BF16

    @pl.when(g == 0)
    def _():
        xe_ref[...] = jnp.zeros_like(xe_ref)
        val_ref[...] = jnp.zeros_like(val_ref)

    chunk_rows = lax.broadcasted_iota(I32, (ch, V7X_LANES), 0)
    expert_lane = lax.broadcasted_iota(I32, (1, n_exp), 1)

    def window(e, k):
        base = (b * n_exp + e) * (nb + 1) + k
        p0 = pfx_ref[base]
        start = pl.multiple_of(p0 & -sub, sub)
        return start, pfx_ref[base + 1] - start

    def one_hot(pos_rows, prob_rows, ws):
        rows = chunk_rows + ws
        hits = [jnp.where(p == rows, 1.0, 0.0) for p in pos_rows]
        onehot = jnp.concatenate([h.astype(BF16) for h in hits], axis=1)
        v = hits[0] * prob_rows[0]
        for h, p in zip(hits[1:], prob_rows[1:]):
            v = v + h * p
        return onehot, jnp.sum(v, axis=1, keepdims=True)

    for j in range(GATHER_BLOCKS):
        k = g * GATHER_BLOCKS + j
        hk = h2_ref[0, j * tb:(j + 1) * tb, :]
        tok_rows = [slice(halves * j + h, halves * j + h + 1) for h in range(halves)]
        starts, pieces = [], []
        overflow = None
        for e in range(n_exp):
            ws, need = window(e, k)
            onehot, v = one_hot([pos_ref[e, r, :] for r in tok_rows],
                                [prob_ref[e, r, :] for r in tok_rows], ws)
            val_ref[0, pl.ds(ws, ch), e:e + 1] += v
            starts.append(ws)
            pieces.append(onehot)
            over = need > ch
            overflow = over if overflow is None else jnp.logical_or(overflow, over)
        gathered = _dot(jnp.concatenate(pieces, axis=0), hk).astype(BF16)
        for e in range(n_exp):
            xe_ref[0, e, pl.ds(starts[e], ch), :] += gathered[e * ch:(e + 1) * ch, :]

        @pl.when(overflow)
        def _():
            def per_expert(e, carry):
                ws, need = window(e, k)

                def chunk(c, carry2):
                    wsc = pl.multiple_of(ws + c * ch, sub)
                    onehot, v = one_hot([pos_ref[pl.ds(e, 1), r, :][0] for r in tok_rows],
                                        [prob_ref[pl.ds(e, 1), r, :][0] for r in tok_rows], wsc)
                    xe_ref[0, e, pl.ds(wsc, ch), :] += _dot(onehot, hk).astype(BF16)
                    val_ref[0, pl.ds(wsc, ch), :] += jnp.where(expert_lane == e, v, 0.0)
                    return carry2

                lax.fori_loop(1, (need + ch - 1) // ch, chunk, 0)
                return carry

            lax.fori_loop(0, n_exp, per_expert, 0)


def _gather_call(pfx, h2_3, pos3, prob3, *, n_exp, cap):
    bsz, seq, d = h2_3.shape
    nb = seq // TOKEN_BLOCK
    gb = GATHER_BLOCKS
    rows_per_step = gb * TOKEN_BLOCK // V7X_LANES
    slots = cap + GATHER_CHUNK
    grid_spec = pltpu.PrefetchScalarGridSpec(
        num_scalar_prefetch=1,
        grid=(bsz, nb // gb),
        in_specs=[pl.BlockSpec((1, gb * TOKEN_BLOCK, d), lambda b, g, p: (b, g, 0)),
                  pl.BlockSpec((n_exp, rows_per_step, V7X_LANES), lambda b, g, p: (b, g, 0)),
                  pl.BlockSpec((n_exp, rows_per_step, V7X_LANES), lambda b, g, p: (b, g, 0))],
        out_specs=[_resident((1, n_exp, slots, d), lambda b, g, p: (b, 0, 0, 0)),
                   pl.BlockSpec((1, slots, n_exp), lambda b, g, p: (b, 0, 0))],
    )
    return pl.pallas_call(
        functools.partial(_gather_kernel, n_exp=n_exp, nb=nb, cap=cap),
        grid_spec=grid_spec,
        out_shape=[jax.ShapeDtypeStruct((bsz, n_exp, slots, d), BF16),
                   jax.ShapeDtypeStruct((bsz, slots, n_exp), F32)],
        compiler_params=_params(("arbitrary", "arbitrary")),
        name="gather",
    )(pfx, h2_3, pos3, prob3)


def _ffn_kernel(xe_ref, val_ref, wg_ref, wu_ref, wd_ref, ye_ref, acc_ref):
    e = pl.program_id(0)
    f = pl.program_id(1)
    bsz, _, cap, d = xe_ref.shape

    @pl.when(f == 0)
    def _():
        acc_ref[...] = jnp.zeros_like(acc_ref)

    xe = xe_ref[...].reshape(bsz * cap, d)
    a = _dot(xe, wg_ref[0].astype(BF16))
    u = _dot(xe, wu_ref[0].astype(BF16))
    half_a = 0.5 * a
    hm = (half_a * (jnp.tanh(half_a) + 1.0) * u).astype(BF16)
    acc_ref[...] += _dot(hm, wd_ref[0].astype(BF16))

    @pl.when(f == pl.num_programs(1) - 1)
    def _():
        vals = val_ref[...].reshape(bsz * cap, val_ref.shape[2])
        mine = lax.broadcasted_iota(I32, (1, vals.shape[1]), 1) == e
        val = jnp.sum(jnp.where(mine, vals, 0.0), axis=1, keepdims=True)
        ye_ref[...] = (acc_ref[...] * val).astype(BF16).reshape(bsz, 1, cap, d)


def _ffn_call(xe, val, w_gate, w_up, w_down, *, cap):
    bsz, n_exp, _, d = xe.shape
    d_exp = w_gate.shape[2]
    fc = FFN_CHUNK
    return pl.pallas_call(
        _ffn_kernel,
        grid=(n_exp, d_exp // fc),
        in_specs=[pl.BlockSpec((bsz, 1, cap, d), lambda e, f: (0, e, 0, 0)),
                  pl.BlockSpec((bsz, cap, n_exp), lambda e, f: (0, 0, 0)),
                  pl.BlockSpec((1, d, fc), lambda e, f: (e, 0, f)),
                  pl.BlockSpec((1, d, fc), lambda e, f: (e, 0, f)),
                  pl.BlockSpec((1, fc, d), lambda e, f: (e, f, 0))],
        out_specs=pl.BlockSpec((bsz, 1, cap, d), lambda e, f: (0, e, 0, 0)),
        out_shape=jax.ShapeDtypeStruct((bsz, n_exp, cap, d), BF16),
        scratch_shapes=[pltpu.VMEM((bsz * cap, d), F32)],
        compiler_params=_params(("arbitrary", "arbitrary")),
        name="ffn",
    )(xe, val, w_gate, w_up, w_down)


def _combine_kernel(pfx_ref, x1_ref, ye_ref, posq_ref, spread_ref, mod_ref, gfin_ref,
                    o_ref, ycat_ref, scat_ref, acc_ref, *, n_exp, nb, cap):
    b = pl.program_id(0)
    tb = TOKEN_BLOCK
    win = COMBINE_WINDOW
    align = V7X_SUBLANES_BF16
    gate_f = mod_ref[0, 5:6, :]
    lane = lax.broadcasted_iota(I32, (1, V7X_LANES), 1)
    per = V7X_LANES // win

    def finish(rows, moe):
        x2 = x1_ref[rows, :] + gate_f * moe
        ms = jnp.mean(x2 * x2, axis=-1, keepdims=True)
        o_ref[rows, :] = x2 * lax.rsqrt(ms + RMS_EPS) * gfin_ref[...]

    redo = []
    for s in range(COMBINE_BLOCKS):
        k = pl.program_id(1) * COMBINE_BLOCKS + s
        rows = slice(s * tb, (s + 1) * tb)
        p0s, ends, wss = [], [], []
        fits = None
        for e in range(n_exp):
            base = (b * n_exp + e) * (nb + 1) + k
            p0 = pfx_ref[base]
            end = pfx_ref[base + 1]
            ws = pl.multiple_of(jnp.minimum(p0 & -align, cap - win), align)
            ok = end <= ws + win
            fits = ok if fits is None else jnp.logical_and(fits, ok)
            p0s.append(p0)
            ends.append(end)
            wss.append(ws)
        slot_of_col = _dot(posq_ref[rows, :], spread_ref[...])
        targets = []
        for e0 in range(0, n_exp, per):
            target = lane + (wss[e0] + 1)
            for j in range(1, per):
                target = jnp.where(lane >= j * win, lane + (wss[e0 + j] + 1 - j * win), target)
            targets.append(target)
            for j in range(per):
                e = e0 + j
                ycat_ref[s, e * win:(e + 1) * win, :] = ye_ref[0, e, pl.ds(wss[e], win), :]
        target_row = jnp.concatenate(targets, axis=1).astype(F32)
        scat_ref[s] = jnp.where(slot_of_col == target_row, 1.0, 0.0).astype(BF16)
        finish(rows, _dot(scat_ref[s], ycat_ref[s]))
        redo.append((rows, jnp.logical_not(fits), p0s, ends))

    for rows, misfit, p0s, ends in redo:
        @pl.when(misfit)
        def _():
            acc_ref[...] = jnp.zeros_like(acc_ref)
            cols = lax.broadcasted_iota(I32, (tb, tb), 1) + 1
            for e in range(n_exp):
                ws1 = pl.multiple_of(jnp.minimum(p0s[e] & -align, cap - tb), align)
                ws2 = pl.multiple_of(jnp.minimum(ws1 + tb, cap - tb), align)
                high = posq_ref[rows, e:e + 1].astype(F32)
                low = posq_ref[rows, n_exp + e:n_exp + e + 1].astype(F32)
                pos_col = (high * (1 << SLOT_DIGIT_BITS) + low).astype(I32)

                @pl.when(ends[e] > p0s[e])
                def _():
                    onehot = jnp.where(pos_col == cols + ws1, 1.0, 0.0).astype(BF16)
                    acc_ref[...] += _dot(onehot, ye_ref[0, e, pl.ds(ws1, tb), :])

                @pl.when(ends[e] > ws1 + tb)
                def _():
                    c2 = cols + ws2
                    hit = jnp.logical_and(pos_col == c2, c2 > ws1 + tb)
                    onehot = jnp.where(hit, 1.0, 0.0).astype(BF16)
                    acc_ref[...] += _dot(onehot, ye_ref[0, e, pl.ds(ws2, tb), :])
            finish(rows, acc_ref[...])


def _combine_call(pfx, x1, ye, posq, mod3, g_final, *, seq):
    t, d = x1.shape
    bsz, n_exp, cap, _ = ye.shape
    tb = TOKEN_BLOCK
    nb = seq // tb
    step_rows = COMBINE_BLOCKS * tb
    steps = nb // COMBINE_BLOCKS
    row = lambda b, k, p: (b * steps + k, 0)
    digit = 1 << SLOT_DIGIT_BITS
    assert cap <= digit * digit
    spread = np.kron(np.eye(n_exp), np.ones((1, COMBINE_WINDOW)))
    spread = jnp.asarray(np.concatenate([digit * spread, spread], axis=0), F32).astype(BF16)
    grid_spec = pltpu.PrefetchScalarGridSpec(
        num_scalar_prefetch=1,
        grid=(bsz, steps),
        in_specs=[pl.BlockSpec((step_rows, d), row),
                  _resident((1, n_exp, cap, d), lambda b, k, p: (b, 0, 0, 0)),
                  pl.BlockSpec((step_rows, 2 * n_exp), row),
                  pl.BlockSpec(spread.shape, lambda b, k, p: (0, 0)),
                  pl.BlockSpec((1, N_MOD, d), lambda b, k, p: (b, 0, 0)),
                  pl.BlockSpec((1, d), lambda b, k, p: (0, 0))],
        out_specs=pl.BlockSpec((step_rows, d), row),
        scratch_shapes=[pltpu.VMEM((COMBINE_BLOCKS, n_exp * COMBINE_WINDOW, d), BF16),
                        pltpu.VMEM((COMBINE_BLOCKS, tb, n_exp * COMBINE_WINDOW), BF16),
                        pltpu.VMEM((tb, d), F32)],
    )
    return pl.pallas_call(
        functools.partial(_combine_kernel, n_exp=n_exp, nb=nb, cap=cap),
        grid_spec=grid_spec,
        out_shape=jax.ShapeDtypeStruct((t, d), F32),
        compiler_params=_params(("arbitrary", "arbitrary")),
        name="combine",
    )(pfx, x1, ye, posq, spread, mod3, g_final)


def _layer(x2, c_t, w_ada, b_ada, g_norm_mix, w_in, b_gate, w_fourier, w_conv, w_conv_out,
           w_o, g_norm_moe, w_router, b_router, w_gate_e, w_up_e, w_down_e, *, bsz, seq):
    t, d = x2.shape
    n_exp = w_router.shape[1]
    cap = EC_CAPACITY * seq // n_exp
    n1 = FFT_N1
    n2 = seq // n1
    fc, f1, f2 = _dft_tables(seq)

    mod = _mod_call(c_t, w_ada, b_ada.reshape(1, -1))
    mod3 = mod.reshape(bsz, N_MOD, d)

    wre, wim, cv, ga, gb = _proj_call(
        x2, mod3, g_norm_mix.reshape(1, d), w_in.astype(BF16), b_gate.reshape(1, -1), fc, w_conv,
        seq=seq)

    tr, ti = _fft1_call(wre, wim, f1)
    a2 = _fft2_call(tr, ti, f2).reshape(t, D_FOURIER)

    x1, h2, probs_t = _mix_call(
        a2, cv, ga, gb, x2, mod3, w_fourier.astype(BF16), w_conv_out.astype(BF16),
        w_o.astype(BF16), g_norm_moe.reshape(1, d), w_router.T, b_router.reshape(n_exp, 1),
        seq=seq)

    prob3 = probs_t
    pos, raw = _select_call(prob3, cap)
    stride = TOKEN_BLOCK // V7X_LANES
    starts = raw[:, ::stride, 0]
    pfx = jnp.concatenate([starts, jnp.full((bsz * n_exp, 1), cap, I32)], axis=1).reshape(-1)
    slot1 = pos.reshape(bsz, n_exp, seq) + 1
    digits = jnp.concatenate([slot1 >> SLOT_DIGIT_BITS, slot1 & ((1 << SLOT_DIGIT_BITS) - 1)],
                             axis=1)
    posq = jnp.transpose(digits, (0, 2, 1)).reshape(t, 2 * n_exp).astype(BF16)

    xe, val = _gather_call(pfx, h2.reshape(bsz, seq, d), pos, prob3, n_exp=n_exp, cap=cap)
    ye = _ffn_call(xe, val, w_gate_e, w_up_e, w_down_e, cap=cap)
    return ye, pfx, x1, posq, mod3


def kernel(x, c, w_ada, b_ada, g_norm_mix, w_in, b_gate, w_fourier, w_conv, w_conv_out, w_o,
           g_norm_moe, w_router, b_router, w_gate_e, w_up_e, w_down_e, g_final):
    bsz, seq, d = x.shape
    assert w_ada.shape[0] == 1
    assert seq % (FFT_N1 * V7X_SUBLANES_F32) == 0 and seq % ROW_TILE == 0
    assert (EC_CAPACITY * seq // w_router.shape[2]) >= TOKEN_BLOCK
    x2 = x.reshape(bsz * seq, d)
    ye, pfx, x1, posq, mod3 = _layer(
        x2, c.T, w_ada[0], b_ada[0], g_norm_mix[0], w_in[0], b_gate[0], w_fourier[0],
        w_conv[0], w_conv_out[0], w_o[0], g_norm_moe[0], w_router[0], b_router[0],
        w_gate_e[0], w_up_e[0], w_down_e[0], bsz=bsz, seq=seq)
    out = _combine_call(pfx, x1, ye, posq, mod3, g_final.reshape(1, d), seq=seq)
    return out.reshape(bsz, seq, d)
```
